```python
import math
import jax
import jax.numpy as jnp
from jax import lax
import numpy as np

D_MODEL = 2048
BATCH = 4
SEQ = 2048
DEPTH = 4
DEC_BATCH = 128
DEC_SEQ = 4
PAST_LEN = 16384
PAGE_SIZE = 128

N_META = 16
N_MIXERS = 3
NORM_EPS = 1e-6

SSD_D_INNER = 2 * D_MODEL
SSD_HEAD_DIM = 64
SSD_HEADS = SSD_D_INNER // SSD_HEAD_DIM
SSD_STATE = 128
SSD_GROUPS = 8
SSD_HPG = SSD_HEADS // SSD_GROUPS
SSD_GN = SSD_GROUPS * SSD_STATE
SSD_CONV_K = 4
SSD_CONV_DIM = SSD_D_INNER + 2 * SSD_GN
SSD_IN_DIM = SSD_D_INNER + SSD_CONV_DIM + SSD_HEADS
SSD_CHUNK = 256

S5_GROUP_SIZE = 16
S5_GROUPS = D_MODEL // S5_GROUP_SIZE
S5_STATE = 64

RWKV_HEAD_DIM = 64
RWKV_HEADS = D_MODEL // RWKV_HEAD_DIM
RWKV_W_LORA = 96
RWKV_A_LORA = 96
RWKV_G_LORA = 256
RWKV_LN_EPS = 64e-5

FFN_HIDDEN = -(-8 * D_MODEL // (3 * 256)) * 256

N_SSD = (DEPTH + N_MIXERS - 1) // N_MIXERS
N_S5 = (DEPTH - 1 + N_MIXERS - 1) // N_MIXERS
N_RWKV = (DEPTH - 2 + N_MIXERS - 1) // N_MIXERS

kernel_name = 'hybrid_ssd_s5_rwkv7_decoder_step'


def rmsnorm(x, w):
    xf = x.astype(jnp.float32)
    y = xf * lax.rsqrt(jnp.mean(xf * xf, axis=-1, keepdims=True) + NORM_EPS)
    return (y * w.astype(jnp.float32)).astype(x.dtype)


def swiglu_ffn(x, w_gu, w_down):
    gu = x @ w_gu
    return (jax.nn.silu(gu[..., :FFN_HIDDEN]) * gu[..., FFN_HIDDEN:]) @ w_down


def ssd_chunked(x, dt, a, bm, cm, h0):
    bsz, l = x.shape[0], x.shape[1]
    q = min(SSD_CHUNK, l)
    pad = (-l) % q
    nc = (l + pad) // q

    def chunks(t):
        t = jnp.pad(t.astype(jnp.float32), [(0, 0), (0, pad)] + [(0, 0)] * (t.ndim - 2))
        return t.reshape((bsz, nc, q) + t.shape[2:])

    x, dt, bm, cm = chunks(x), chunks(dt), chunks(bm), chunks(cm)
    cum = jnp.cumsum(dt * a, axis=2)
    seg = cum[:, :, :, None] - cum[:, :, None, :]
    causal = jnp.tril(jnp.ones((q, q), dtype=bool))[:, :, None, None]
    lmat = jnp.exp(jnp.where(causal, seg, -jnp.inf))
    cb = jnp.einsum('bcign,bcjgn->bcijg', cm, bm)
    y_diag = jnp.einsum('bcijgr,bcjgrp->bcigrp', cb[..., None] * lmat * dt[:, :, None], x)
    decay_end = jnp.exp(cum[:, :, -1:] - cum)
    states = jnp.einsum('bcjgn,bcjgrp->bcgrpn', bm, x * (decay_end * dt)[..., None])
    chunk_decay = jnp.exp(cum[:, :, -1])

    def step(h, inp):
        st, dec = inp
        return h * dec[..., None, None] + st, h

    h_last, h_start = lax.scan(step, h0, (jnp.moveaxis(states, 1, 0), jnp.moveaxis(chunk_decay, 1, 0)))
    h_start = jnp.moveaxis(h_start, 0, 1)
    y_off = jnp.einsum('bcign,bcgrpn->bcigrp', cm, h_start) * jnp.exp(cum)[..., None]
    y = (y_diag + y_off).reshape((bsz, nc * q) + x.shape[3:])
    return y[:, :l], h_last


def ssd_mixer(u, conv_prev, h0, segments, w_in, conv_w, conv_b, dt_bias, a_log, d_skip, norm_w, w_out):
    bsz, l, _ = u.shape
    f32 = jnp.float32
    proj = u @ w_in
    z = proj[..., :SSD_D_INNER]
    xbc = proj[..., SSD_D_INNER:SSD_D_INNER + SSD_CONV_DIM]
    dt_raw = proj[..., SSD_D_INNER + SSD_CONV_DIM:]
    xpad = jnp.concatenate([conv_prev.astype(xbc.dtype), xbc], axis=1)
    conv_new = xpad[:, xpad.shape[1] - (SSD_CONV_K - 1):]
    xc = lax.conv_general_dilated(xpad, conv_w[:, None, :].astype(xpad.dtype), (1,), 'VALID',
                                  dimension_numbers=('NWC', 'WIO', 'NWC'),
                                  feature_group_count=SSD_CONV_DIM)
    xc = jax.nn.silu(xc + conv_b).astype(f32)
    xs = xc[..., :SSD_D_INNER].reshape(bsz, l, SSD_GROUPS, SSD_HPG, SSD_HEAD_DIM)
    bm = xc[..., SSD_D_INNER:SSD_D_INNER + SSD_GN].reshape(bsz, l, SSD_GROUPS, SSD_STATE)
    cm = xc[..., SSD_D_INNER + SSD_GN:].reshape(bsz, l, SSD_GROUPS, SSD_STATE)
    dt = jax.nn.softplus(dt_raw.astype(f32) + dt_bias.astype(f32)).reshape(bsz, l, SSD_GROUPS, SSD_HPG)
    a = -jnp.exp(a_log.astype(f32)).reshape(SSD_GROUPS, SSD_HPG)
    h = h0.astype(f32).reshape(bsz, SSD_GROUPS, SSD_HPG, SSD_HEAD_DIM, SSD_STATE)
    ys, start = [], 0
    for seg in segments:
        y_seg, h = ssd_chunked(xs[:, start:start + seg], dt[:, start:start + seg], a,
                               bm[:, start:start + seg], cm[:, start:start + seg], h)
        ys.append(y_seg)
        start += seg
    y = jnp.concatenate(ys, axis=1) + xs * d_skip.astype(f32).reshape(SSD_GROUPS, SSD_HPG)[:, :, None]
    y = y.reshape(bsz, l, SSD_D_INNER) * jax.nn.silu(z.astype(f32))
    yg = y.reshape(bsz, l, SSD_GROUPS, SSD_D_INNER // SSD_GROUPS)
    yg = yg * lax.rsqrt(jnp.mean(yg * yg, axis=-1, keepdims=True) + NORM_EPS)
    y = yg.reshape(bsz, l, SSD_D_INNER) * norm_w.astype(f32)
    out = y.astype(u.dtype) @ w_out
    return out, h.reshape(bsz, SSD_HEADS, SSD_HEAD_DIM, SSD_STATE), conv_new


def s5_combine(e1, e2):
    a1r, a1i, b1r, b1i = e1
    a2r, a2i, b2r, b2i = e2
    return (a2r * a1r - a2i * a1i,
            a2r * a1i + a2i * a1r,
            a2r * b1r - a2i * b1i + b2r,
            a2r * b1i + a2i * b1r + b2i)


def s5_mixer(u, h0_re, h0_im, w_in, lam_re, lam_im, log_dt, b_re, b_im, c_re, c_im, d_skip, w_out):
    bsz, l, _ = u.shape
    f32 = jnp.float32
    v = (u @ w_in).astype(f32)
    vg = v.reshape(bsz, l, S5_GROUPS, S5_GROUP_SIZE)
    dt = jnp.exp(log_dt.astype(f32))[:, None]
    lr, li = lam_re.astype(f32), lam_im.astype(f32)
    mag = jnp.exp(lr * dt)
    ang = li * dt
    ab_re, ab_im = mag * jnp.cos(ang), mag * jnp.sin(ang)
    den = lr * lr + li * li
    f_re = ((ab_re - 1.0) * lr + ab_im * li) / den
    f_im = (ab_im * lr - (ab_re - 1.0) * li) / den
    br, bi = b_re.astype(f32), b_im.astype(f32)
    bb_re = f_re[..., None] * br - f_im[..., None] * bi
    bb_im = f_re[..., None] * bi + f_im[..., None] * br
    bu_re = jnp.einsum('gps,blgs->blgp', bb_re, vg)
    bu_im = jnp.einsum('gps,blgs->blgp', bb_im, vg)
    h0r, h0i = h0_re.astype(f32), h0_im.astype(f32)
    bu_re = bu_re.at[:, 0].add(ab_re * h0r - ab_im * h0i)
    bu_im = bu_im.at[:, 0].add(ab_re * h0i + ab_im * h0r)
    a_re = jnp.broadcast_to(ab_re, bu_re.shape)
    a_im = jnp.broadcast_to(ab_im, bu_im.shape)
    _, _, h_re, h_im = lax.associative_scan(s5_combine, (a_re, a_im, bu_re, bu_im), axis=1)
    y = (jnp.einsum('gsp,blgp->blgs', c_re.astype(f32), h_re)
         - jnp.einsum('gsp,blgp->blgs', c_im.astype(f32), h_im))
    y = y.reshape(bsz, l, D_MODEL) + d_skip.astype(f32) * v
    zg = jax.nn.gelu(y).astype(u.dtype) @ w_out
    out = zg[..., :D_MODEL] * jax.nn.sigmoid(zg[..., D_MODEL:])
    return out, h_re[:, -1], h_im[:, -1]


def rwkv7_mixer(u, shift_prev, s0, mu, wr, wk, wv, wo, w0, w1, w2, a0, a1, a2, g1, g2,
                k_k, k_a, r_k, ln_w, ln_b):
    bsz, l, _ = u.shape
    f32 = jnp.float32
    prev = jnp.concatenate([shift_prev.astype(u.dtype)[:, None], u[:, :-1]], axis=1)
    mixed = u[None] + (prev - u)[None] * mu[:, None, None, :]
    xr, xw, xk, xv, xa, xg = mixed[0], mixed[1], mixed[2], mixed[3], mixed[4], mixed[5]
    r = (xr @ wr).astype(f32)
    k = (xk @ wk).astype(f32)
    v = (xv @ wv).astype(f32)
    w = -jax.nn.softplus(-(w0 + jnp.tanh(xw @ w1) @ w2).astype(f32)) - 0.5
    a = jax.nn.sigmoid((a0 + (xa @ a1) @ a2).astype(f32))
    g = (jax.nn.sigmoid(xg @ g1) @ g2).astype(f32)
    hd = (bsz, l, RWKV_HEADS, RWKV_HEAD_DIM)
    kk = (k * k_k.astype(f32)).reshape(hd)
    kk = kk / jnp.maximum(jnp.sqrt(jnp.sum(kk * kk, axis=-1, keepdims=True)), 1e-12)
    k = (k * (1.0 + (a - 1.0) * k_a.astype(f32))).reshape(hd)
    r, v, a = r.reshape(hd), v.reshape(hd), a.reshape(hd)
    decay = jnp.exp(-jnp.exp(w)).reshape(hd)

    def step(s, inp):
        r_t, k_t, v_t, dec_t, kk_t, kka_t = inp
        sa = jnp.einsum('bhij,bhj->bhi', s, -kk_t)
        s = (s * dec_t[:, :, None, :] + sa[..., None] * kka_t[:, :, None, :]
             + v_t[..., None] * k_t[:, :, None, :])
        return s, jnp.einsum('bhij,bhj->bhi', s, r_t)

    seq = tuple(jnp.moveaxis(t, 1, 0) for t in (r, k, v, decay, kk, kk * a))
    s_last, ys = lax.scan(step, s0.astype(f32), seq)
    y = jnp.moveaxis(ys, 0, 1)
    mean = jnp.mean(y, axis=-1, keepdims=True)
    var = jnp.mean(jnp.square(y - mean), axis=-1, keepdims=True)
    y = ((y - mean) * lax.rsqrt(var + RWKV_LN_EPS)).reshape(bsz, l, D_MODEL) * ln_w.astype(f32) + ln_b.astype(f32)
    bonus = jnp.sum(r * k * r_k.astype(f32).reshape(RWKV_HEADS, RWKV_HEAD_DIM), axis=-1, keepdims=True) * v
    out = ((y + bonus.reshape(bsz, l, D_MODEL)) * g).astype(u.dtype) @ wo
    return out, s_last, u[:, -1]


def setup_inputs(seed: int = 0) -> dict:
    key = jax.random.key(seed)
    ks = iter(jax.random.split(key, 64))
    f32 = jnp.float32

    def nrm(shape, scale):
        return scale * jax.random.normal(next(ks), shape, f32)

    def uni(shape, lo, hi):
        return jax.random.uniform(next(ks), shape, f32, lo, hi)

    d = D_MODEL
    ns, n5, nr = N_SSD, N_S5, N_RWKV
    dt0 = jnp.exp(uni((ns, SSD_HEADS), math.log(1e-3), math.log(1e-1)))
    return {
        'x_prompt': nrm((BATCH, SEQ, d), 1.0),
        'x_sample': nrm((DEC_BATCH, DEC_SEQ, d), 1.0),
        'state_ssd': nrm((ns, DEC_BATCH, SSD_HEADS, SSD_HEAD_DIM, SSD_STATE), 0.1),
        'state_ssd_conv': nrm((ns, DEC_BATCH, SSD_CONV_K - 1, SSD_CONV_DIM), 1.0),
        'state_s5_re': nrm((n5, DEC_BATCH, S5_GROUPS, S5_STATE), 0.5),
        'state_s5_im': nrm((n5, DEC_BATCH, S5_GROUPS, S5_STATE), 0.5),
        'state_rwkv': nrm((nr, DEC_BATCH, RWKV_HEADS, RWKV_HEAD_DIM, RWKV_HEAD_DIM), 0.3),
        'state_rwkv_shift': nrm((nr, DEC_BATCH, d), 1.0),
        'meta_tokens': nrm((N_META, d), 1.0),
        'norm1_w': 1.0 + nrm((DEPTH, d), 0.02),
        'norm2_w': 1.0 + nrm((DEPTH, d), 0.02),
        'normf_w': 1.0 + nrm((d,), 0.02),
        'ffn_w_gu': nrm((DEPTH, d, 2 * FFN_HIDDEN), d ** -0.5),
        'ffn_w_down': nrm((DEPTH, FFN_HIDDEN, d), FFN_HIDDEN ** -0.5),
        'ssd_w_in': nrm((ns, d, SSD_IN_DIM), d ** -0.5),
        'ssd_conv_w': nrm((ns, SSD_CONV_K, SSD_CONV_DIM), SSD_CONV_K ** -0.5),
        'ssd_conv_b': nrm((ns, SSD_CONV_DIM), 0.02),
        'ssd_dt_bias': dt0 + jnp.log(-jnp.expm1(-dt0)),
        'ssd_a_log': jnp.log(uni((ns, SSD_HEADS), 1.0, 16.0)),
        'ssd_d': 1.0 + nrm((ns, SSD_HEADS), 0.1),
        'ssd_norm_w': 1.0 + nrm((ns, SSD_D_INNER), 0.02),
        'ssd_w_out': nrm((ns, SSD_D_INNER, d), SSD_D_INNER ** -0.5),
        's5_w_in': nrm((n5, d, d), d ** -0.5),
        's5_lam_re': -0.5 + nrm((n5, S5_GROUPS, S5_STATE), 0.01),
        's5_lam_im': math.pi * jnp.arange(S5_STATE, dtype=f32) + nrm((n5, S5_GROUPS, S5_STATE), 0.01),
        's5_log_dt': uni((n5, S5_GROUPS), math.log(1e-3), math.log(1e-1)),
        's5_b_re': nrm((n5, S5_GROUPS, S5_STATE, S5_GROUP_SIZE), (2 * S5_GROUP_SIZE) ** -0.5),
        's5_b_im': nrm((n5, S5_GROUPS, S5_STATE, S5_GROUP_SIZE), (2 * S5_GROUP_SIZE) ** -0.5),
        's5_c_re': nrm((n5, S5_GROUPS, S5_GROUP_SIZE, S5_STATE), (2 * S5_STATE) ** -0.5),
        's5_c_im': nrm((n5, S5_GROUPS, S5_GROUP_SIZE, S5_STATE), (2 * S5_STATE) ** -0.5),
        's5_d': nrm((n5, d), 0.5),
        's5_w_out': nrm((n5, d, 2 * d), d ** -0.5),
        'rwkv_mu': uni((nr, 6, d), 0.0, 1.0),
        'rwkv_wr': nrm((nr, d, d), d ** -0.5),
        'rwkv_wk': nrm((nr, d, d), d ** -0.5),
        'rwkv_wv': nrm((nr, d, d), d ** -0.5),
        'rwkv_wo': nrm((nr, d, d), d ** -0.5),
        'rwkv_w0': uni((nr, d), -6.0, -1.0),
        'rwkv_w1': nrm((nr, d, RWKV_W_LORA), d ** -0.5),
        'rwkv_w2': nrm((nr, RWKV_W_LORA, d), 0.1 * RWKV_W_LORA ** -0.5),
        'rwkv_a0': nrm((nr, d), 0.1),
        'rwkv_a1': nrm((nr, d, RWKV_A_LORA), d ** -0.5),
        'rwkv_a2': nrm((nr, RWKV_A_LORA, d), 0.1 * RWKV_A_LORA ** -0.5),
        'rwkv_g1': nrm((nr, d, RWKV_G_LORA), d ** -0.5),
        'rwkv_g2': nrm((nr, RWKV_G_LORA, d), RWKV_G_LORA ** -0.5),
        'rwkv_k_k': 0.85 + nrm((nr, d), 0.02),
        'rwkv_k_a': 1.0 + nrm((nr, d), 0.02),
        'rwkv_r_k': nrm((nr, d), 0.1),
        'rwkv_ln_w': 1.0 + nrm((nr, d), 0.02),
        'rwkv_ln_b': nrm((nr, d), 0.02),
    }


def reference(x_prompt, x_sample, state_ssd, state_ssd_conv, state_s5_re, state_s5_im, state_rwkv,
              state_rwkv_shift, meta_tokens, norm1_w, norm2_w, normf_w, ffn_w_gu, ffn_w_down,
              ssd_w_in, ssd_conv_w, ssd_conv_b, ssd_dt_bias, ssd_a_log, ssd_d, ssd_norm_w, ssd_w_out,
              s5_w_in, s5_lam_re, s5_lam_im, s5_log_dt, s5_b_re, s5_b_im, s5_c_re, s5_c_im, s5_d, s5_w_out,
              rwkv_mu, rwkv_wr, rwkv_wk, rwkv_wv, rwkv_wo, rwkv_w0, rwkv_w1, rwkv_w2, rwkv_a0, rwkv_a1,
              rwkv_a2, rwkv_g1, rwkv_g2, rwkv_k_k, rwkv_k_a, rwkv_r_k, rwkv_ln_w, rwkv_ln_b):
    f32 = jnp.float32
    bsz = x_prompt.shape[0]
    meta = jnp.broadcast_to(meta_tokens.astype(x_prompt.dtype)[None], (bsz, N_META, D_MODEL))
    hp = jnp.concatenate([meta, x_prompt], axis=1)
    hs = x_sample
    lp, ls = hp.shape[1], hs.shape[1]
    p_ssd, p_conv, p_s5r, p_s5i, p_rwkv, p_shift = [], [], [], [], [], []
    s_ssd, s_conv, s_s5r, s_s5i, s_rwkv, s_shift = [], [], [], [], [], []
    for i in range(DEPTH):
        kind, j = i % N_MIXERS, i // N_MIXERS
        up = rmsnorm(hp, norm1_w[i])
        us = rmsnorm(hs, norm1_w[i])
        if kind == 0:
            prm = (ssd_w_in[j], ssd_conv_w[j], ssd_conv_b[j], ssd_dt_bias[j], ssd_a_log[j],
                   ssd_d[j], ssd_norm_w[j], ssd_w_out[j])
            op, st_p, cv_p = ssd_mixer(up, jnp.zeros((bsz, SSD_CONV_K - 1, SSD_CONV_DIM), up.dtype),
                                       jnp.zeros((bsz, SSD_HEADS, SSD_HEAD_DIM, SSD_STATE), f32),
                                       (N_META, lp - N_META), *prm)
            os_, st_s, cv_s = ssd_mixer(us, state_ssd_conv[j], state_ssd[j], (ls,), *prm)
            p_ssd.append(st_p); p_conv.append(cv_p); s_ssd.append(st_s); s_conv.append(cv_s)
        elif kind == 1:
            prm = (s5_w_in[j], s5_lam_re[j], s5_lam_im[j], s5_log_dt[j], s5_b_re[j], s5_b_im[j],
                   s5_c_re[j], s5_c_im[j], s5_d[j], s5_w_out[j])
            zero_h = jnp.zeros((bsz, S5_GROUPS, S5_STATE), f32)
            op, hr_p, hi_p = s5_mixer(up, zero_h, zero_h, *prm)
            os_, hr_s, hi_s = s5_mixer(us, state_s5_re[j], state_s5_im[j], *prm)
            p_s5r.append(hr_p); p_s5i.append(hi_p); s_s5r.append(hr_s); s_s5i.append(hi_s)
        else:
            prm = (rwkv_mu[j], rwkv_wr[j], rwkv_wk[j], rwkv_wv[j], rwkv_wo[j], rwkv_w0[j], rwkv_w1[j],
                   rwkv_w2[j], rwkv_a0[j], rwkv_a1[j], rwkv_a2[j], rwkv_g1[j], rwkv_g2[j], rwkv_k_k[j],
                   rwkv_k_a[j], rwkv_r_k[j], rwkv_ln_w[j], rwkv_ln_b[j])
            op, sp, shp = rwkv7_mixer(up, jnp.zeros((bsz, D_MODEL), up.dtype),
                                      jnp.zeros((bsz, RWKV_HEADS, RWKV_HEAD_DIM, RWKV_HEAD_DIM), f32), *prm)
            os_, ss, shs = rwkv7_mixer(us, state_rwkv_shift[j], state_rwkv[j], *prm)
            p_rwkv.append(sp); p_shift.append(shp); s_rwkv.append(ss); s_shift.append(shs)
        hp = hp + op.astype(hp.dtype)
        hs = hs + os_.astype(hs.dtype)
        hp = hp + swiglu_ffn(rmsnorm(hp, norm2_w[i]), ffn_w_gu[i], ffn_w_down[i]).astype(hp.dtype)
        hs = hs + swiglu_ffn(rmsnorm(hs, norm2_w[i]), ffn_w_gu[i], ffn_w_down[i]).astype(hs.dtype)
    y_prompt = rmsnorm(hp, normf_w)[:, N_META:]
    y_sample = rmsnorm(hs, normf_w)
    return (y_prompt, y_sample,
            jnp.stack(p_ssd), jnp.stack(p_conv), jnp.stack(p_s5r), jnp.stack(p_s5i),
            jnp.stack(p_rwkv), jnp.stack(p_shift),
            jnp.stack(s_ssd), jnp.stack(s_conv), jnp.stack(s_s5r), jnp.stack(s_s5i),
            jnp.stack(s_rwkv), jnp.stack(s_shift))
```

```python
import functools
import math

import jax
import jax.numpy as jnp
from jax import lax
from jax.experimental import pallas as pl
from jax.experimental.pallas import tpu as pltpu

F32 = jnp.float32
BF16 = jnp.bfloat16

D_MODEL = 2048
BATCH = 4
SEQ = 2048
DEPTH = 4
DEC_BATCH = 128
DEC_SEQ = 4
N_META = 16
N_MIXERS = 3
NORM_EPS = 1e-6

SSD_D_INNER = 2 * D_MODEL
SSD_HEAD_DIM = 64
SSD_HEADS = SSD_D_INNER // SSD_HEAD_DIM
SSD_STATE = 128
SSD_GROUPS = 8
SSD_HPG = SSD_HEADS // SSD_GROUPS
SSD_GN = SSD_GROUPS * SSD_STATE
SSD_CONV_K = 4
SSD_CONV_DIM = SSD_D_INNER + 2 * SSD_GN
SSD_ZX_DIM = SSD_D_INNER + SSD_CONV_DIM
SSD_GROUP_W = SSD_D_INNER // SSD_GROUPS

S5_GROUP_SIZE = 16
S5_GROUPS = D_MODEL // S5_GROUP_SIZE
S5_STATE = 64
S5_HDIM = S5_GROUPS * S5_STATE
S5_SBLK = 512
S5_NSB = S5_HDIM // S5_SBLK
S5_VBLK = S5_SBLK // S5_STATE * S5_GROUP_SIZE

RWKV_HEAD_DIM = 64
RWKV_HEADS = D_MODEL // RWKV_HEAD_DIM
RWKV_PAIRS = RWKV_HEADS // 2
RWKV_LORA_PAD = 128
RWKV_G_LORA = 256
RWKV_LN_EPS = 64e-5

FFN_HIDDEN = -(-8 * D_MODEL // (3 * 256)) * 256

LANES = 128
SUBLANES = 8
ROW_BLK = 128
SAMPLE_ROWS_PER_SEQ = SUBLANES
SAMPLE_BLKS = DEC_BATCH * SAMPLE_ROWS_PER_SEQ // ROW_BLK
MAIN_BLK0 = SAMPLE_BLKS
MAIN_BLKS_PER_B = SEQ // ROW_BLK
META_BLK0 = MAIN_BLK0 + BATCH * MAIN_BLKS_PER_B
N_BLKS = META_BLK0 + BATCH
T_ROWS = N_BLKS * ROW_BLK
PROMPT_CHUNKS = 1 + MAIN_BLKS_PER_B
META_PAD = ROW_BLK - N_META

MM_TM = 512
MM_TN = 512
VMEM_LIMIT = 48 * 2 ** 20


def _prompt_blk(b, c):
    return jnp.where(c == 0, META_BLK0 + b, MAIN_BLK0 + MAIN_BLKS_PER_B * b + c - 1)


def _cparams(*sem):
    return pltpu.CompilerParams(dimension_semantics=sem, vmem_limit_bytes=VMEM_LIMIT)


def _silu(x):
    return x * jax.nn.sigmoid(x)


def _softplus(x):
    return jnp.maximum(x, 0.0) + jnp.log(1.0 + jnp.exp(-jnp.abs(x)))


def _rms(x, w):
    return x * lax.rsqrt(jnp.mean(x * x, axis=-1, keepdims=True) + NORM_EPS) * w


def _dot(a, b):
    return jnp.dot(a, b, preferred_element_type=F32)


def _dot_nt(a, b):
    return lax.dot_general(a, b, (((1,), (1,)), ((), ())), preferred_element_type=F32)


def _dot_tn(a, b):
    return lax.dot_general(a, b, (((0,), (0,)), ((), ())), preferred_element_type=F32)


def _split_bf16(x):
    hi = x.astype(BF16)
    return hi, (x - hi.astype(F32)).astype(BF16)


def _drop_ref(kernel_fn, idx):
    def wrapped(*refs):
        return kernel_fn(*refs[:idx], *refs[idx + 1:])
    return wrapped


def _mm_kernel(*refs, normalize, glu, has_res):
    it = iter(refs)
    x_ref = next(it)
    nw_ref = next(it) if normalize else None
    w_ref = next(it)
    w2_ref = next(it) if glu else None
    res_ref = next(it) if has_res else None
    o_ref = next(it)
    xn_ref = next(it) if normalize else None
    if normalize:
        @pl.when(pl.program_id(1) == 0)
        def _():
            xn_ref[...] = _rms(x_ref[...], nw_ref[...]).astype(BF16)
        xb = xn_ref[...]
    else:
        xb = x_ref[...].astype(BF16)
    acc = _dot(xb, w_ref[...])
    if glu:
        acc = acc * jax.nn.sigmoid(_dot(xb, w2_ref[...]))
    if has_res:
        acc = acc + res_ref[...]
    o_ref[...] = acc.astype(o_ref.dtype)


def _matmul(x, w, n_out, *, norm_w=None, res=None, glu_off=None, out_dtype=F32, tn=MM_TN):
    m, k = x.shape
    normalize = norm_w is not None
    glu = glu_off is not None
    has_res = res is not None
    in_specs = [pl.BlockSpec((MM_TM, k), lambda i, j: (i, 0))]
    args = [x]
    if normalize:
        in_specs.append(pl.BlockSpec((1, k), lambda i, j: (0, 0)))
        args.append(norm_w.reshape(1, k))
    in_specs.append(pl.BlockSpec((k, tn), lambda i, j: (0, j)))
    args.append(w)
    if glu:
        in_specs.append(pl.BlockSpec((k, tn), lambda i, j: (0, j + glu_off)))
        args.append(w)
    if has_res:
        in_specs.append(pl.BlockSpec((MM_TM, tn), lambda i, j: (i, j)))
        args.append(res)
    return pl.pallas_call(
        functools.partial(_mm_kernel, normalize=normalize, glu=glu, has_res=has_res),
        grid=(m // MM_TM, n_out // tn),
        in_specs=in_specs,
        out_specs=pl.BlockSpec((MM_TM, tn), lambda i, j: (i, j)),
        out_shape=jax.ShapeDtypeStruct((m, n_out), out_dtype),
        scratch_shapes=[pltpu.VMEM((MM_TM, k), BF16)] if normalize else [],
        compiler_params=_cparams("parallel", "arbitrary"),
    )(*args)


def _rmsnorm_kernel(x_ref, w_ref, o_ref):
    o_ref[...] = _rms(x_ref[...], w_ref[...])


def _rmsnorm(x, w):
    m, k = x.shape
    return pl.pallas_call(
        _rmsnorm_kernel,
        grid=(m // MM_TM,),
        in_specs=[pl.BlockSpec((MM_TM, k), lambda i: (i, 0)), pl.BlockSpec((1, k), lambda i: (0, 0))],
        out_specs=pl.BlockSpec((MM_TM, k), lambda i: (i, 0)),
        out_shape=jax.ShapeDtypeStruct((m, k), F32),
        compiler_params=_cparams("parallel"),
    )(x, w.reshape(1, k))


FFN_TF = 512
FFN_NF = FFN_HIDDEN // FFN_TF


def _ffn_kernel(x_ref, nw_ref, wg_ref, wu_ref, wd_ref, o_ref, xn_ref):
    f = pl.program_id(1)

    @pl.when(f == 0)
    def _():
        x = x_ref[...]
        xn_ref[...] = _rms(x, nw_ref[...]).astype(BF16)
        o_ref[...] = x

    xb = xn_ref[...]
    act = _silu(_dot(xb, wg_ref[...])) * _dot(xb, wu_ref[...])
    o_ref[...] += _dot(act.astype(BF16), wd_ref[...])


def _ffn(h, norm_w, w_gu, w_down):
    m = h.shape[0]
    return pl.pallas_call(
        _ffn_kernel,
        grid=(m // MM_TM, FFN_NF),
        in_specs=[
            pl.BlockSpec((MM_TM, D_MODEL), lambda i, f: (i, 0)),
            pl.BlockSpec((1, D_MODEL), lambda i, f: (0, 0)),
            pl.BlockSpec((D_MODEL, FFN_TF), lambda i, f: (0, f)),
            pl.BlockSpec((D_MODEL, FFN_TF), lambda i, f: (0, f + FFN_NF)),
            pl.BlockSpec((FFN_TF, D_MODEL), lambda i, f: (f, 0)),
        ],
        out_specs=pl.BlockSpec((MM_TM, D_MODEL), lambda i, f: (i, 0)),
        out_shape=jax.ShapeDtypeStruct((m, D_MODEL), F32),
        scratch_shapes=[pltpu.VMEM((MM_TM, D_MODEL), BF16)],
        compiler_params=_cparams("parallel", "arbitrary"),
    )(h, norm_w.reshape(1, D_MODEL), w_gu, w_gu, w_down)


def _dt_kernel(x_ref, nw_ref, wt_ref, bias_ref, o_ref):
    xb = _rms(x_ref[...], nw_ref[...]).astype(BF16)
    o_ref[...] = _softplus(_dot_nt(wt_ref[...], xb) + bias_ref[...])


def _ssd_dt(h, norm_w, w_dt_t, dt_bias):
    m = h.shape[0]
    return pl.pallas_call(
        _dt_kernel,
        grid=(m // MM_TM,),
        in_specs=[
            pl.BlockSpec((MM_TM, D_MODEL), lambda i: (i, 0)),
            pl.BlockSpec((1, D_MODEL), lambda i: (0, 0)),
            pl.BlockSpec((SSD_HEADS, D_MODEL), lambda i: (0, 0)),
            pl.BlockSpec((SSD_HEADS, 1), lambda i: (0, 0)),
        ],
        out_specs=pl.BlockSpec((SSD_HEADS, MM_TM), lambda i: (0, i)),
        out_shape=jax.ShapeDtypeStruct((SSD_HEADS, m), F32),
        compiler_params=_cparams("parallel"),
    )(h, norm_w.reshape(1, D_MODEL), w_dt_t, dt_bias.reshape(SSD_HEADS, 1))


SSD_Q = ROW_BLK


def _ssd_kernel(z_ref, x_ref, b_ref, c_ref, valid_ref, validt_ref, dtt_ref, alog_ref, dskip_ref,
                cwx_ref, cwb_ref, cwc_ref, cbx_ref, cbb_ref, cbc_ref, cpx_ref, cpb_ref, cpc_ref,
                nw_ref, h0_ref, y_ref, hl_ref, st_ref, xpx_ref, xpb_ref, xpc_ref, *, rows, nch):
    c = pl.program_id(2)
    q = SSD_Q

    @pl.when(c == 0)
    def _():
        st_ref[...] = h0_ref[0]
        xpx_ref[0:SUBLANES] = cpx_ref[0]
        xpb_ref[0:SUBLANES] = cpb_ref[0]
        xpc_ref[0:SUBLANES] = cpc_ref[0]

    valid = valid_ref[...]

    def conv(raw_ref, xp_ref, w_ref, bias_ref):
        raw = raw_ref[...] * valid
        xp_ref[SUBLANES:SUBLANES + rows] = raw
        acc = bias_ref[...]
        for k in range(SSD_CONV_K):
            acc = acc + w_ref[k:k + 1, :] * xp_ref[pl.ds(SUBLANES - SSD_CONV_K + 1 + k, rows), :]
        if nch > 1:
            xp_ref[0:SUBLANES] = raw[rows - SUBLANES:rows]
        out = _silu(acc)
        if rows < q:
            out = jnp.concatenate([out, jnp.zeros((q - rows, out.shape[1]), F32)], axis=0)
        return out

    xs = conv(x_ref, xpx_ref, cwx_ref, cbx_ref)
    bm = conv(b_ref, xpb_ref, cwb_ref, cbb_ref).astype(BF16)
    cm = conv(c_ref, xpc_ref, cwc_ref, cbc_ref).astype(BF16)

    dtt = dtt_ref[0, 0] * validt_ref[0]
    a_col = -jnp.exp(alog_ref[0])
    lane = lax.broadcasted_iota(jnp.int32, (SSD_HPG, q), 1)
    cum = dtt * a_col
    s = 1
    while s < q:
        cum = cum + jnp.where(lane >= s, pltpu.roll(cum, s, axis=1), 0.0)
        s *= 2

    row_i = lax.broadcasted_iota(jnp.int32, (q, q), 0)
    col_i = lax.broadcasted_iota(jnp.int32, (q, q), 1)
    eye = row_i == col_i
    causal = row_i >= col_i

    def to_col(rowvec):
        return jnp.sum(jnp.where(eye, rowvec, 0.0), axis=1, keepdims=True)

    cb = _dot_nt(cm, bm)
    head_a = lax.broadcasted_iota(jnp.int32, (1, LANES), 1) < SSD_HEAD_DIM
    row_a = lax.broadcasted_iota(jnp.int32, (2 * SSD_HEAD_DIM, 1), 0) < SSD_HEAD_DIM
    dskip = dskip_ref[0]
    ys = []
    for p in range(SSD_HPG // 2):
        xh = xs[:, LANES * p:LANES * (p + 1)]
        xhb = xh.astype(BF16)
        st = st_ref[2 * p:2 * p + 2].reshape(2 * SSD_HEAD_DIM, SSD_STATE)
        yo = _dot_nt(cm, st.astype(BF16))
        yd, e_col, w_col, e_last = [], [], [], []
        for r in (2 * p, 2 * p + 1):
            cum_row = cum[r:r + 1, :]
            dt_row = dtt[r:r + 1, :]
            cum_col = to_col(cum_row)
            dt_col = to_col(dt_row)
            lmat = jnp.exp(jnp.where(causal, cum_col - cum_row, -jnp.inf))
            yd.append(_dot((cb * lmat * dt_row).astype(BF16), xhb))
            c_last = cum_row[:, q - 1:q]
            e_col.append(jnp.exp(cum_col))
            w_col.append(jnp.exp(c_last - cum_col) * dt_col)
            e_last.append(jnp.exp(c_last))
        d2 = jnp.where(head_a, dskip[:, 2 * p:2 * p + 1], dskip[:, 2 * p + 1:2 * p + 2])
        y2 = (jnp.where(head_a, yd[0], yd[1]) + yo * jnp.where(head_a, e_col[0], e_col[1]) + xh * d2)
        xw = (xh * jnp.where(head_a, w_col[0], w_col[1])).astype(BF16)
        st_new = st * jnp.where(row_a, e_last[0], e_last[1]) + _dot_tn(xw, bm)
        st_ref[2 * p:2 * p + 2] = st_new.reshape(2, SSD_HEAD_DIM, SSD_STATE)
        ys.append(y2)
    y = jnp.concatenate(ys, axis=1)[:rows]
    y = y * _silu(z_ref[...])
    y_ref[...] = _rms(y, nw_ref[...]).astype(y_ref.dtype)

    @pl.when(c == nch - 1)
    def _():
        hl_ref[0] = st_ref[...]


def _ssd_core(proj, valid, validt, dtt, a_log, d_skip, conv_w, conv_b, conv_prev8, norm_w, h0, *,
              prompt, y_init=None):
    nseq = h0.shape[0]
    if prompt:
        rows, nch = ROW_BLK, PROMPT_CHUNKS
        blk = _prompt_blk
    else:
        rows, nch = SAMPLE_ROWS_PER_SEQ, 1
        blk = lambda b, c: b
    gw = SSD_GROUP_W
    x_off = SSD_D_INNER // gw
    b_off = (2 * SSD_D_INNER) // SSD_STATE
    c_off = b_off + SSD_GROUPS
    cb_off = SSD_D_INNER // SSD_STATE
    cc_off = cb_off + SSD_GROUPS
    row_spec = lambda w, off: pl.BlockSpec((rows, w), lambda b, g, c: (blk(b, c), g + off))
    par_spec = lambda r, w, off: pl.BlockSpec((r, w), lambda b, g, c: (0, g + off))
    prev_spec = lambda w, off: pl.BlockSpec((1, SUBLANES, w), lambda b, g, c: (b, 0, g + off))
    st_spec = pl.BlockSpec((1, SSD_HPG, SSD_HEAD_DIM, SSD_STATE), lambda b, g, c: (b, g, 0, 0))
    in_specs = [
        row_spec(gw, 0), row_spec(gw, x_off), row_spec(SSD_STATE, b_off), row_spec(SSD_STATE, c_off),
        pl.BlockSpec((rows, 1), lambda b, g, c: (blk(b, c), 0)),
        pl.BlockSpec((1, 1, SSD_Q), lambda b, g, c: (blk(b, c), 0, 0)),
        pl.BlockSpec((1, 1, SSD_HPG, SSD_Q), lambda b, g, c: (blk(b, c), g, 0, 0)),
        pl.BlockSpec((1, SSD_HPG, 1), lambda b, g, c: (g, 0, 0)),
        pl.BlockSpec((1, 1, SSD_HPG), lambda b, g, c: (g, 0, 0)),
        par_spec(SSD_CONV_K, gw, 0), par_spec(SSD_CONV_K, SSD_STATE, cb_off),
        par_spec(SSD_CONV_K, SSD_STATE, cc_off),
        par_spec(1, gw, 0), par_spec(1, SSD_STATE, cb_off), par_spec(1, SSD_STATE, cc_off),
        prev_spec(gw, 0), prev_spec(SSD_STATE, cb_off), prev_spec(SSD_STATE, cc_off),
        par_spec(1, gw, 0),
        st_spec,
    ]
    args = [proj, proj, proj, proj, valid, validt, dtt,
            a_log.reshape(SSD_GROUPS, SSD_HPG, 1), d_skip.reshape(SSD_GROUPS, 1, SSD_HPG),
            conv_w, conv_w, conv_w, conv_b, conv_b, conv_b, conv_prev8, conv_prev8, conv_prev8,
            norm_w, h0]
    kernel_fn = functools.partial(_ssd_kernel, rows=rows, nch=nch)
    aliases = {}
    if y_init is not None:
        in_specs.append(pl.BlockSpec(memory_space=pl.ANY))
        args.append(y_init)
        aliases = {len(args) - 1: 0}
        kernel_fn = _drop_ref(kernel_fn, len(args) - 1)
    return pl.pallas_call(
        kernel_fn,
        grid=(nseq, SSD_GROUPS, nch),
        in_specs=in_specs,
        out_specs=[row_spec(gw, 0), st_spec],
        input_output_aliases=aliases,
        out_shape=[jax.ShapeDtypeStruct((T_ROWS, SSD_D_INNER), BF16),
                   jax.ShapeDtypeStruct(h0.shape, F32)],
        scratch_shapes=[
            pltpu.VMEM((SSD_HPG, SSD_HEAD_DIM, SSD_STATE), F32),
            pltpu.VMEM((SUBLANES + rows, gw), F32),
            pltpu.VMEM((SUBLANES + rows, SSD_STATE), F32),
            pltpu.VMEM((SUBLANES + rows, SSD_STATE), F32),
        ],
        compiler_params=_cparams("parallel", "parallel", "arbitrary"),
    )(*args)


def _s5_abar(lam_re, lam_im, log_dt):
    dt = jnp.exp(log_dt)
    mag = jnp.exp(lam_re * dt)
    ang = lam_im * dt
    return mag * jnp.cos(ang), mag * jnp.sin(ang)


def _s5_pow_kernel(lr_ref, li_ref, ldt_ref, pr_ref, pi_ref):
    ar, ai = _s5_abar(lr_ref[...], li_ref[...], ldt_ref[...])
    row = lax.broadcasted_iota(jnp.int32, (SUBLANES, S5_HDIM), 0)
    pr, pi = ar, ai
    out_r = jnp.broadcast_to(ar, (SUBLANES, S5_HDIM))
    out_i = jnp.broadcast_to(ai, (SUBLANES, S5_HDIM))
    for k in range(1, SUBLANES):
        pr, pi = pr * ar - pi * ai, pr * ai + pi * ar
        out_r = jnp.where(row == k, pr, out_r)
        out_i = jnp.where(row == k, pi, out_i)
    pr_ref[...] = out_r
    pi_ref[...] = out_i


def _s5_bbar_kernel(lr_ref, li_ref, ldt_ref, br_ref, bi_ref, or_ref, oi_ref):
    lr, li = lr_ref[...], li_ref[...]
    ar, ai = _s5_abar(lr, li, ldt_ref[...])
    den = lr * lr + li * li
    f_re = ((ar - 1.0) * lr + ai * li) / den
    f_im = (ai * lr - (ar - 1.0) * li) / den
    br, bi = br_ref[...], bi_ref[...]
    or_ref[...] = f_re * br - f_im * bi
    oi_ref[...] = f_re * bi + f_im * br


def _s5_params(lam_re, lam_im, log_dt, b_re, b_im):
    ldt = jnp.repeat(log_dt, S5_STATE)
    row = lambda a: a.reshape(1, S5_HDIM)
    col = lambda a: a.reshape(S5_HDIM, 1)
    full = lambda shape: pl.BlockSpec(shape, lambda: (0,) * len(shape))
    pow_re, pow_im = pl.pallas_call(
        _s5_pow_kernel,
        in_specs=[full((1, S5_HDIM))] * 3,
        out_specs=[full((SUBLANES, S5_HDIM))] * 2,
        out_shape=[jax.ShapeDtypeStruct((SUBLANES, S5_HDIM), F32)] * 2,
    )(row(lam_re), row(lam_im), row(ldt))
    bb_re, bb_im = pl.pallas_call(
        _s5_bbar_kernel,
        in_specs=[full((S5_HDIM, 1))] * 3 + [full((S5_HDIM, S5_GROUP_SIZE))] * 2,
        out_specs=[full((S5_HDIM, S5_GROUP_SIZE))] * 2,
        out_shape=[jax.ShapeDtypeStruct((S5_HDIM, S5_GROUP_SIZE), F32)] * 2,
    )(col(lam_re), col(lam_im), col(ldt), b_re.reshape(S5_HDIM, S5_GROUP_SIZE),
      b_im.reshape(S5_HDIM, S5_GROUP_SIZE))
    return pow_re, pow_im, bb_re, bb_im


def _s5_block_diag(bb, c):
    gpb = S5_SBLK // S5_STATE
    eye = jnp.eye(gpb, dtype=F32)
    bb = bb.reshape(S5_NSB, gpb, S5_STATE, S5_GROUP_SIZE)
    w_in = eye[None, :, None, :, None] * jnp.transpose(bb, (0, 3, 1, 2))[:, None]
    w_in = w_in.reshape(S5_NSB, S5_VBLK, S5_SBLK)
    c = c.reshape(S5_NSB, gpb, S5_GROUP_SIZE, S5_STATE)
    w_out = eye[None, :, None, :, None] * jnp.transpose(c, (0, 1, 3, 2))[:, :, :, None, :]
    w_out = w_out.reshape(S5_NSB, S5_SBLK, S5_VBLK)
    return w_in.astype(BF16), w_out.astype(BF16)


def _s5_kernel(v_ref, valid_ref, wir_ref, wii_ref, wor_ref, woi_ref, pr_ref, pi_ref, d_ref,
               h0r_ref, h0i_ref, y_ref, hlr_ref, hli_ref, cr_ref, ci_ref, hr_ref, hi_ref, *,
               per_tile, nch, last_row):
    ntiles = ROW_BLK // SUBLANES
    if not per_tile:
        c = pl.program_id(2)

        @pl.when(c == 0)
        def _():
            cr_ref[...] = h0r_ref[0]
            ci_ref[...] = h0i_ref[0]

    v = v_ref[...]
    vb = (v * valid_ref[...]).astype(BF16)
    bu_r = _dot(vb, wir_ref[0])
    bu_i = _dot(vb, wii_ref[0])
    pr, pi = pr_ref[...], pi_ref[...]
    row = lax.broadcasted_iota(jnp.int32, (SUBLANES, S5_SBLK), 0)
    levels = []
    for s in (1, 2, 4):
        levels.append((s, jnp.where(row >= s, pr[s - 1:s], 0.0), jnp.where(row >= s, pi[s - 1:s], 0.0)))
    if not per_tile:
        car_r, car_i = cr_ref[...], ci_ref[...]
    for i in range(ntiles):
        xr = bu_r[SUBLANES * i:SUBLANES * (i + 1)]
        xi = bu_i[SUBLANES * i:SUBLANES * (i + 1)]
        for s, ar, ai in levels:
            sr = pltpu.roll(xr, s, axis=0)
            si = pltpu.roll(xi, s, axis=0)
            xr, xi = xr + ar * sr - ai * si, xi + ar * si + ai * sr
        if per_tile:
            car_r, car_i = h0r_ref[i], h0i_ref[i]
        hr = xr + pr * car_r - pi * car_i
        hi = xi + pr * car_i + pi * car_r
        hr_ref[SUBLANES * i:SUBLANES * (i + 1)] = hr
        hi_ref[SUBLANES * i:SUBLANES * (i + 1)] = hi
        if per_tile:
            hlr_ref[i] = hr[last_row:last_row + 1]
            hli_ref[i] = hi[last_row:last_row + 1]
        else:
            car_r, car_i = hr[last_row:last_row + 1], hi[last_row:last_row + 1]
    y = (_dot(hr_ref[...].astype(BF16), wor_ref[0]) - _dot(hi_ref[...].astype(BF16), woi_ref[0])
         + d_ref[...] * v)
    y_ref[...] = jax.nn.gelu(y).astype(y_ref.dtype)
    if not per_tile:
        cr_ref[...] = car_r
        ci_ref[...] = car_i

        @pl.when(c == nch - 1)
        def _():
            hlr_ref[0] = car_r
            hli_ref[0] = car_i


def _s5_core(v, valid, w_in_re, w_in_im, w_out_re, w_out_im, pow_re, pow_im, d_skip, h0_re, h0_im, *,
             prompt, y_init=None):
    nseq = h0_re.shape[0]
    if prompt:
        grid = (nseq, S5_NSB, PROMPT_CHUNKS)
        blk = _prompt_blk
        sidx = lambda b, s, c: (b, 0, s)
        spb = 1
        kw = dict(per_tile=False, nch=PROMPT_CHUNKS, last_row=SUBLANES - 1)
        sem = ("parallel", "parallel", "arbitrary")
    else:
        grid = (SAMPLE_BLKS, S5_NSB, 1)
        blk = lambda b, c: b
        sidx = lambda b, s, c: (b, 0, s)
        spb = ROW_BLK // SAMPLE_ROWS_PER_SEQ
        kw = dict(per_tile=True, nch=1, last_row=DEC_SEQ - 1)
        sem = ("parallel", "parallel", "arbitrary")
    row_spec = pl.BlockSpec((ROW_BLK, S5_VBLK), lambda b, s, c: (blk(b, c), s))
    st_spec = pl.BlockSpec((spb, 1, S5_SBLK), sidx)
    w_in_spec = pl.BlockSpec((1, S5_VBLK, S5_SBLK), lambda b, s, c: (s, 0, 0))
    w_out_spec = pl.BlockSpec((1, S5_SBLK, S5_VBLK), lambda b, s, c: (s, 0, 0))
    pow_spec = pl.BlockSpec((SUBLANES, S5_SBLK), lambda b, s, c: (0, s))
    in_specs = [row_spec, pl.BlockSpec((ROW_BLK, 1), lambda b, s, c: (blk(b, c), 0)),
                w_in_spec, w_in_spec, w_out_spec, w_out_spec, pow_spec, pow_spec,
                pl.BlockSpec((1, S5_VBLK), lambda b, s, c: (0, s)), st_spec, st_spec]
    args = [v, valid, w_in_re, w_in_im, w_out_re, w_out_im, pow_re, pow_im,
            d_skip.reshape(1, D_MODEL), h0_re, h0_im]
    kernel_fn = functools.partial(_s5_kernel, **kw)
    aliases = {}
    if y_init is not None:
        in_specs.append(pl.BlockSpec(memory_space=pl.ANY))
        args.append(y_init)
        aliases = {len(args) - 1: 0}
        kernel_fn = _drop_ref(kernel_fn, len(args) - 1)
    return pl.pallas_call(
        kernel_fn,
        grid=grid,
        in_specs=in_specs,
        out_specs=[row_spec, st_spec, st_spec],
        input_output_aliases=aliases,
        out_shape=[jax.ShapeDtypeStruct((T_ROWS, D_MODEL), BF16),
                   jax.ShapeDtypeStruct(h0_re.shape, F32), jax.ShapeDtypeStruct(h0_im.shape, F32)],
        scratch_shapes=[pltpu.VMEM((1, S5_SBLK), F32), pltpu.VMEM((1, S5_SBLK), F32),
                        pltpu.VMEM((ROW_BLK, S5_SBLK), F32), pltpu.VMEM((ROW_BLK, S5_SBLK), F32)],
        compiler_params=_cparams(*sem),
    )(*args)


def _rwkv_proj_kernel(u_ref, p_ref, mu_ref, wr_ref, wk_ref, wv_ref, r_ref, k_ref, v_ref,
                      xr_ref, xk_ref, xv_ref):
    @pl.when(pl.program_id(1) == 0)
    def _():
        u = u_ref[...]
        d = p_ref[...] - u
        xr_ref[...] = (u + d * mu_ref[0:1, :]).astype(BF16)
        xk_ref[...] = (u + d * mu_ref[2:3, :]).astype(BF16)
        xv_ref[...] = (u + d * mu_ref[3:4, :]).astype(BF16)

    r_ref[...] = _dot(xr_ref[...], wr_ref[...])
    k_ref[...] = _dot(xk_ref[...], wk_ref[...])
    v_ref[...] = _dot(xv_ref[...], wv_ref[...])


def _rwkv_proj(u, prev, mu, wr, wk, wv):
    m = u.shape[0]
    row = pl.BlockSpec((MM_TM, D_MODEL), lambda i, j: (i, 0))
    wsp = pl.BlockSpec((D_MODEL, MM_TN), lambda i, j: (0, j))
    osp = pl.BlockSpec((MM_TM, MM_TN), lambda i, j: (i, j))
    return pl.pallas_call(
        _rwkv_proj_kernel,
        grid=(m // MM_TM, D_MODEL // MM_TN),
        in_specs=[row, row, pl.BlockSpec((6, D_MODEL), lambda i, j: (0, 0)), wsp, wsp, wsp],
        out_specs=[osp, osp, osp],
        out_shape=[jax.ShapeDtypeStruct((m, D_MODEL), F32)] * 3,
        scratch_shapes=[pltpu.VMEM((MM_TM, D_MODEL), BF16)] * 3,
        compiler_params=_cparams("parallel", "arbitrary"),
    )(u, prev, mu, wr, wk, wv)


RWKV_LORA_TM = 256


def _rwkv_lora_kernel(u_ref, p_ref, mu_ref, w1_ref, w2_ref, a1_ref, a2_ref, g1_ref, g2_ref,
                      w0_ref, a0_ref, dec_ref, a_ref, g_ref):
    u = u_ref[...]
    d = p_ref[...] - u
    xw = (u + d * mu_ref[1:2, :]).astype(BF16)
    xa = (u + d * mu_ref[4:5, :]).astype(BF16)
    xg = (u + d * mu_ref[5:6, :]).astype(BF16)
    wpre = w0_ref[...] + _dot(jnp.tanh(_dot(xw, w1_ref[...])).astype(BF16), w2_ref[...])
    w = -_softplus(-wpre) - 0.5
    dec_ref[...] = jnp.exp(-jnp.exp(w))
    a_ref[...] = jax.nn.sigmoid(a0_ref[...] + _dot(_dot(xa, a1_ref[...]).astype(BF16), a2_ref[...]))
    g_ref[...] = _dot(jax.nn.sigmoid(_dot(xg, g1_ref[...])).astype(BF16), g2_ref[...])


def _rwkv_lora(u, prev, mu, w1, w2, a1, a2, g1, g2, w0, a0):
    m = u.shape[0]
    row = pl.BlockSpec((RWKV_LORA_TM, D_MODEL), lambda i: (i, 0))
    full = lambda a: pl.BlockSpec(a.shape, lambda i: (0, 0))
    vec = pl.BlockSpec((1, D_MODEL), lambda i: (0, 0))
    return pl.pallas_call(
        _rwkv_lora_kernel,
        grid=(m // RWKV_LORA_TM,),
        in_specs=[row, row, pl.BlockSpec((6, D_MODEL), lambda i: (0, 0)),
                  full(w1), full(w2), full(a1), full(a2), full(g1), full(g2), vec, vec],
        out_specs=[row, row, row],
        out_shape=[jax.ShapeDtypeStruct((m, D_MODEL), F32)] * 3,
        compiler_params=_cparams("parallel"),
    )(u, prev, mu, w1, w2, a1, a2, g1, g2, w0.reshape(1, D_MODEL), a0.reshape(1, D_MODEL))


def _block_ones():
    r = lax.broadcasted_iota(jnp.int32, (LANES, LANES), 0) // RWKV_HEAD_DIM
    c = lax.broadcasted_iota(jnp.int32, (LANES, LANES), 1) // RWKV_HEAD_DIM
    return (r == c).astype(BF16)


def _head_sum(x, bo):
    hi, lo = _split_bf16(x)
    return _dot(hi, bo) + _dot(lo, bo)


RWKV_PG = 8
RWKV_NPG = RWKV_PAIRS // RWKV_PG
RWKV_PG_W = RWKV_PG * LANES


class _RwkvConsts:
    def __init__(self):
        self.bo = _block_ones()
        lane = lax.broadcasted_iota(jnp.int32, (RWKV_HEAD_DIM, LANES), 1)
        row = lax.broadcasted_iota(jnp.int32, (RWKV_HEAD_DIM, LANES), 0)
        self.diag = ((lane % RWKV_HEAD_DIM) == row).astype(BF16)
        self.head_a = lane < RWKV_HEAD_DIM
        self.row8 = lax.broadcasted_iota(jnp.int32, (SUBLANES, LANES), 0)


def _rwkv_step(s2, s2b, kk_t, dec_t, kka_t, k_t, v_t, r_lhs, cst):
    bo = cst.bo
    sa = _dot_nt(s2b, bo * (-kk_t).astype(BF16))
    v_hi, v_lo = _split_bf16(v_t)
    v2 = _dot_nt(cst.diag, bo * v_hi) + _dot_nt(cst.diag, bo * v_lo)
    s2n = s2 * dec_t + sa * kka_t + v2 * k_t
    s2nb = s2n.astype(BF16)
    zero = jnp.zeros_like(s2nb)
    wy = jnp.concatenate([jnp.where(cst.head_a, s2nb, zero), jnp.where(cst.head_a, zero, s2nb)], axis=0)
    return s2n, s2nb, _dot_nt(r_lhs, wy)


def _rwkv_steps(s2, tiles, nsteps, cst):
    kk8, dec8, kka8, k8, v8, r8 = tiles
    s2b = s2.astype(BF16)
    y = jnp.zeros((SUBLANES, LANES), F32)
    for i in range(nsteps):
        r_lhs = jnp.where(cst.row8 == i, r8, 0.0).astype(BF16)
        s2, s2b, yi = _rwkv_step(s2, s2b, kk8[i:i + 1], dec8[i:i + 1], kka8[i:i + 1], k8[i:i + 1],
                                 v8[i:i + 1], r_lhs, cst)
        y = y + yi
    return s2, y


def _rwkv_prepare(k_ref, a_ref, v_ref, valid, kk_p, ka_p, kk_s, kka_s, km_s, vm_s, bo):
    for p in range(RWKV_PG):
        sl = slice(LANES * p, LANES * (p + 1))
        k, a = k_ref[:, sl], a_ref[:, sl]
        kkr = k * kk_p[:, sl]
        kk = kkr / jnp.maximum(jnp.sqrt(_head_sum(kkr * kkr, bo)), 1e-12)
        kk_s[:, sl] = kk
        kka_s[:, sl] = kk * a
        km_s[:, sl] = k * (1.0 + (a - 1.0) * ka_p[:, sl])
        vm_s[:, sl] = v_ref[:, sl] * valid


def _rwkv_finish(yraw_s, r_ref, km_s, vm_s, g_ref, rk_p, lnw_p, lnb_p, o_ref, bo):
    inv = 1.0 / RWKV_HEAD_DIM
    for p in range(RWKV_PG):
        sl = slice(LANES * p, LANES * (p + 1))
        y = yraw_s[:, sl]
        d = y - _head_sum(y, bo) * inv
        yn = d * lax.rsqrt(_head_sum(d * d, bo) * inv + RWKV_LN_EPS) * lnw_p[:, sl] + lnb_p[:, sl]
        bonus = _head_sum(r_ref[:, sl] * km_s[:, sl] * rk_p[:, sl], bo) * vm_s[:, sl]
        o_ref[:, sl] = ((yn + bonus) * g_ref[:, sl]).astype(o_ref.dtype)


def _rwkv_tiles(base, p, refs):
    return [ref[pl.ds(base, SUBLANES), pl.ds(LANES * p, LANES)] for ref in refs]


def _rwkv_prompt_kernel(r_ref, k_ref, v_ref, dec_ref, a_ref, g_ref, valid_ref, kk_p, ka_p, rk_p,
                        lnw_p, lnb_p, s0_ref, o_ref, sl_ref, s_ref, kk_s, kka_s, km_s, vm_s, yraw_s):
    c = pl.program_id(2)

    @pl.when(c == 0)
    def _():
        s_ref[...] = s0_ref[0]

    cst = _RwkvConsts()
    _rwkv_prepare(k_ref, a_ref, v_ref, valid_ref[...], kk_p, ka_p, kk_s, kka_s, km_s, vm_s, cst.bo)

    def body(ti, carry):
        base = pl.multiple_of(ti * SUBLANES, SUBLANES)
        for p in range(RWKV_PG):
            tiles = _rwkv_tiles(base, p, (kk_s, dec_ref, kka_s, km_s, vm_s, r_ref))
            s2, y = _rwkv_steps(s_ref[p], tiles, SUBLANES, cst)
            s_ref[p] = s2
            yraw_s[pl.ds(base, SUBLANES), pl.ds(LANES * p, LANES)] = y
        return carry

    lax.fori_loop(0, ROW_BLK // SUBLANES, body, 0)
    _rwkv_finish(yraw_s, r_ref, km_s, vm_s, g_ref, rk_p, lnw_p, lnb_p, o_ref, cst.bo)

    @pl.when(c == PROMPT_CHUNKS - 1)
    def _():
        sl_ref[0] = s_ref[...]


RWKV_SAMPLE_SEQS = 8
RWKV_SAMPLE_ROWS = RWKV_SAMPLE_SEQS * SAMPLE_ROWS_PER_SEQ


def _rwkv_sample_kernel(r_ref, k_ref, v_ref, dec_ref, a_ref, g_ref, valid_ref, kk_p, ka_p, rk_p,
                        lnw_p, lnb_p, s0_ref, o_ref, sl_ref, kk_s, kka_s, km_s, vm_s, yraw_s):
    cst = _RwkvConsts()
    _rwkv_prepare(k_ref, a_ref, v_ref, valid_ref[...], kk_p, ka_p, kk_s, kka_s, km_s, vm_s, cst.bo)

    def body(i, carry):
        base = pl.multiple_of(i * SAMPLE_ROWS_PER_SEQ, SAMPLE_ROWS_PER_SEQ)
        for p in range(RWKV_PG):
            tiles = _rwkv_tiles(base, p, (kk_s, dec_ref, kka_s, km_s, vm_s, r_ref))
            s2, y = _rwkv_steps(s0_ref[i, p], tiles, DEC_SEQ, cst)
            sl_ref[i, p] = s2
            yraw_s[pl.ds(base, SUBLANES), pl.ds(LANES * p, LANES)] = y
        return carry

    lax.fori_loop(0, RWKV_SAMPLE_SEQS, body, 0)
    _rwkv_finish(yraw_s, r_ref, km_s, vm_s, g_ref, rk_p, lnw_p, lnb_p, o_ref, cst.bo)


def _rwkv_core(r, k, v, dec, a, g, valid, k_k, k_a, r_k, ln_w, ln_b, s0, *, prompt, y_init=None):
    nseq = s0.shape[0]
    vec = lambda a_: a_.reshape(1, D_MODEL)
    if prompt:
        rows = ROW_BLK
        grid = (nseq, RWKV_NPG, PROMPT_CHUNKS)
        rmap = lambda b, q, c: (_prompt_blk(b, c), q)
        vmap = lambda b, q, c: (_prompt_blk(b, c), 0)
        spb = 1
        kernel_fn = _rwkv_prompt_kernel
        scratch = [pltpu.VMEM((RWKV_PG, RWKV_HEAD_DIM, LANES), F32)]
    else:
        rows = RWKV_SAMPLE_ROWS
        grid = (nseq // RWKV_SAMPLE_SEQS, RWKV_NPG, 1)
        rmap = lambda b, q, c: (b, q)
        vmap = lambda b, q, c: (b, 0)
        spb = RWKV_SAMPLE_SEQS
        kernel_fn = _rwkv_sample_kernel
        scratch = []
    sem = ("parallel", "parallel", "arbitrary")
    row_spec = pl.BlockSpec((rows, RWKV_PG_W), rmap)
    vec_spec = pl.BlockSpec((1, RWKV_PG_W), lambda b, q, c: (0, q))
    st_spec = pl.BlockSpec((spb, RWKV_PG, RWKV_HEAD_DIM, LANES), lambda b, q, c: (b, q, 0, 0))
    in_specs = [row_spec] * 6 + [pl.BlockSpec((rows, 1), vmap)] + [vec_spec] * 5 + [st_spec]
    args = [r, k, v, dec, a, g, valid, vec(k_k), vec(k_a), vec(r_k), vec(ln_w), vec(ln_b), s0]
    aliases = {}
    if y_init is not None:
        in_specs.append(pl.BlockSpec(memory_space=pl.ANY))
        args.append(y_init)
        aliases = {len(args) - 1: 0}
        kernel_fn = _drop_ref(kernel_fn, len(args) - 1)
    return pl.pallas_call(
        kernel_fn,
        grid=grid,
        in_specs=in_specs,
        out_specs=[row_spec, st_spec],
        input_output_aliases=aliases,
        out_shape=[jax.ShapeDtypeStruct((T_ROWS, D_MODEL), BF16), jax.ShapeDtypeStruct(s0.shape, F32)],
        scratch_shapes=scratch + [pltpu.VMEM((rows, RWKV_PG_W), F32)] * 5,
        compiler_params=_cparams(*sem),
    )(*args)


def _row_ids():
    sample_t0 = jnp.arange(DEC_BATCH) * SAMPLE_ROWS_PER_SEQ
    main0 = MAIN_BLK0 * ROW_BLK + jnp.arange(BATCH) * SEQ
    meta0 = META_BLK0 * ROW_BLK + jnp.arange(BATCH) * ROW_BLK + META_PAD
    return sample_t0, main0, meta0


def _valid_mask():
    r = jnp.arange(T_ROWS)
    sample = (r < MAIN_BLK0 * ROW_BLK) & (r % SAMPLE_ROWS_PER_SEQ < DEC_SEQ)
    main = (r >= MAIN_BLK0 * ROW_BLK) & (r < META_BLK0 * ROW_BLK)
    meta = (r >= META_BLK0 * ROW_BLK) & (r % ROW_BLK >= META_PAD)
    return (sample | main | meta).astype(F32)


def _pad_conv_prev(prev):
    return jnp.pad(prev, ((0, 0), (SUBLANES - SSD_CONV_K + 1, 0), (0, 0)))


def _ssd_layer(h, valid, norm_w, w_in, conv_w, conv_b, dt_bias, a_log, d_skip, gnorm_w, w_out,
               state, state_conv):
    sample_t0, main0, meta0 = _row_ids()
    w_in_b = w_in.astype(BF16)
    proj = _matmul(h, w_in_b, SSD_ZX_DIM, norm_w=norm_w)
    dtt = _ssd_dt(h, norm_w, w_in_b[:, SSD_ZX_DIM:].T, dt_bias)
    xbc = lambda rows: proj[rows][..., SSD_D_INNER:]
    tail = jnp.arange(SSD_CONV_K - 1)
    validt = valid.reshape(N_BLKS, 1, ROW_BLK)
    dtt_p = jnp.transpose(dtt.reshape(SSD_GROUPS, SSD_HPG, N_BLKS, ROW_BLK), (2, 0, 1, 3))
    gn = gnorm_w.reshape(1, SSD_D_INNER)
    cb = conv_b.reshape(1, SSD_CONV_DIM)
    h0_meta = jnp.zeros((BATCH, SSD_HEADS, SSD_HEAD_DIM, SSD_STATE), F32)
    prev_p = jnp.zeros((BATCH, SUBLANES, SSD_CONV_DIM), F32)
    y, p_state = _ssd_core(proj, valid.reshape(T_ROWS, 1), validt, dtt_p, a_log, d_skip, conv_w, cb,
                           prev_p, gn, h0_meta, prompt=True)
    ns = SAMPLE_ROWS_PER_SEQ
    dtt_s = dtt[:, :DEC_BATCH * ns].reshape(SSD_GROUPS, SSD_HPG, DEC_BATCH, ns)
    dtt_s = jnp.pad(jnp.transpose(dtt_s, (2, 0, 1, 3)), ((0, 0), (0, 0), (0, 0), (0, SSD_Q - ns)))
    validt_s = jnp.pad(valid[:DEC_BATCH * ns].reshape(DEC_BATCH, 1, ns), ((0, 0), (0, 0), (0, SSD_Q - ns)))
    y, s_state = _ssd_core(proj, valid.reshape(T_ROWS, 1), validt_s, dtt_s, a_log, d_skip, conv_w, cb,
                           _pad_conv_prev(state_conv), gn, state, prompt=False, y_init=y)
    h = _matmul(y, w_out.astype(BF16), D_MODEL, res=h)
    p_conv = xbc(main0[:, None] + SEQ - (SSD_CONV_K - 1) + tail[None, :])
    s_conv = xbc(sample_t0[:, None] + DEC_SEQ - (SSD_CONV_K - 1) + tail[None, :])
    return h, p_state, p_conv, s_state, s_conv


def _s5_layer(h, valid, norm_w, w_in, lam_re, lam_im, log_dt, b_re, b_im, c_re, c_im, d_skip, w_out,
              state_re, state_im):
    v = _matmul(h, w_in.astype(BF16), D_MODEL, norm_w=norm_w)
    pow_re, pow_im, bb_re, bb_im = _s5_params(lam_re, lam_im, log_dt, b_re, b_im)
    wi_re, wo_re = _s5_block_diag(bb_re, c_re)
    wi_im, wo_im = _s5_block_diag(bb_im, c_im)
    vcol = valid.reshape(T_ROWS, 1)
    zero = jnp.zeros((BATCH, 1, S5_HDIM), F32)
    y, p_re, p_im = _s5_core(v, vcol, wi_re, wi_im, wo_re, wo_im, pow_re, pow_im, d_skip, zero, zero,
                             prompt=True)
    y, s_re, s_im = _s5_core(v, vcol, wi_re, wi_im, wo_re, wo_im, pow_re, pow_im, d_skip,
                             state_re.reshape(DEC_BATCH, 1, S5_HDIM),
                             state_im.reshape(DEC_BATCH, 1, S5_HDIM), prompt=False, y_init=y)
    h = _matmul(y, w_out.astype(BF16), D_MODEL, res=h, glu_off=D_MODEL // MM_TN)
    shp = lambda a, n: a.reshape(n, S5_GROUPS, S5_STATE)
    return h, shp(p_re, BATCH), shp(p_im, BATCH), shp(s_re, DEC_BATCH), shp(s_im, DEC_BATCH)


def _to_pairs(s):
    n = s.shape[0]
    s = s.reshape(n, RWKV_PAIRS, 2, RWKV_HEAD_DIM, RWKV_HEAD_DIM)
    return jnp.transpose(s, (0, 1, 3, 2, 4)).reshape(n, RWKV_PAIRS, RWKV_HEAD_DIM, LANES)


def _from_pairs(s):
    n = s.shape[0]
    s = s.reshape(n, RWKV_PAIRS, RWKV_HEAD_DIM, 2, RWKV_HEAD_DIM)
    return jnp.transpose(s, (0, 1, 3, 2, 4)).reshape(n, RWKV_HEADS, RWKV_HEAD_DIM, RWKV_HEAD_DIM)


def _pad_lora(w_down, w_up):
    n = w_down.shape[1]
    return (jnp.pad(w_down, ((0, 0), (0, RWKV_LORA_PAD - n))).astype(BF16),
            jnp.pad(w_up, ((0, RWKV_LORA_PAD - n), (0, 0))).astype(BF16))


def _rwkv_layer(h, valid, norm_w, mu, wr, wk, wv, wo, w0, w1, w2, a0, a1, a2, g1, g2, k_k, k_a, r_k,
                ln_w, ln_b, state, state_shift):
    sample_t0, main0, meta0 = _row_ids()
    u = _rmsnorm(h, norm_w)
    prev = jnp.concatenate([jnp.zeros((1, D_MODEL), F32), u[:-1]], axis=0)
    prev = prev.at[sample_t0].set(state_shift)
    prev = prev.at[main0].set(u[meta0 + N_META - 1])
    prev = prev.at[meta0].set(0.0)
    r, k, v = _rwkv_proj(u, prev, mu, wr.astype(BF16), wk.astype(BF16), wv.astype(BF16))
    w1p, w2p = _pad_lora(w1, w2)
    a1p, a2p = _pad_lora(a1, a2)
    dec, a, g = _rwkv_lora(u, prev, mu, w1p, w2p, a1p, a2p, g1.astype(BF16), g2.astype(BF16), w0, a0)
    vcol = valid.reshape(T_ROWS, 1)
    s0_p = jnp.zeros((BATCH, RWKV_PAIRS, RWKV_HEAD_DIM, LANES), F32)
    y, p_state = _rwkv_core(r, k, v, dec, a, g, vcol, k_k, k_a, r_k, ln_w, ln_b, s0_p, prompt=True)
    y, s_state = _rwkv_core(r, k, v, dec, a, g, vcol, k_k, k_a, r_k, ln_w, ln_b, _to_pairs(state),
                            prompt=False, y_init=y)
    h = _matmul(y, wo.astype(BF16), D_MODEL, res=h)
    return (h, _from_pairs(p_state), u[main0 + SEQ - 1], _from_pairs(s_state),
            u[sample_t0 + DEC_SEQ - 1])


def kernel(x_prompt, x_sample, state_ssd, state_ssd_conv, state_s5_re, state_s5_im, state_rwkv,
           state_rwkv_shift, meta_tokens, norm1_w, norm2_w, normf_w, ffn_w_gu, ffn_w_down,
           ssd_w_in, ssd_conv_w, ssd_conv_b, ssd_dt_bias, ssd_a_log, ssd_d, ssd_norm_w, ssd_w_out,
           s5_w_in, s5_lam_re, s5_lam_im, s5_log_dt, s5_b_re, s5_b_im, s5_c_re, s5_c_im, s5_d, s5_w_out,
           rwkv_mu, rwkv_wr, rwkv_wk, rwkv_wv, rwkv_wo, rwkv_w0, rwkv_w1, rwkv_w2, rwkv_a0, rwkv_a1,
           rwkv_a2, rwkv_g1, rwkv_g2, rwkv_k_k, rwkv_k_a, rwkv_r_k, rwkv_ln_w, rwkv_ln_b):
    valid = _valid_mask()
    sample = jnp.pad(x_sample, ((0, 0), (0, SAMPLE_ROWS_PER_SEQ - DEC_SEQ), (0, 0)))
    meta = jnp.pad(jnp.broadcast_to(meta_tokens[None], (BATCH, N_META, D_MODEL)),
                   ((0, 0), (META_PAD, 0), (0, 0)))
    h = jnp.concatenate([sample.reshape(-1, D_MODEL), x_prompt.reshape(-1, D_MODEL),
                         meta.reshape(-1, D_MODEL)], axis=0).astype(F32)
    outs = {name: [] for name in ("p_ssd", "p_conv", "p_s5r", "p_s5i", "p_rwkv", "p_shift",
                                  "s_ssd", "s_conv", "s_s5r", "s_s5i", "s_rwkv", "s_shift")}
    for i in range(DEPTH):
        kind, j = i % N_MIXERS, i // N_MIXERS
        if kind == 0:
            h, ps, pc, ss, sc = _ssd_layer(
                h, valid, norm1_w[i], ssd_w_in[j], ssd_conv_w[j], ssd_conv_b[j], ssd_dt_bias[j],
                ssd_a_log[j], ssd_d[j], ssd_norm_w[j], ssd_w_out[j], state_ssd[j], state_ssd_conv[j])
            outs["p_ssd"].append(ps); outs["p_conv"].append(pc)
            outs["s_ssd"].append(ss); outs["s_conv"].append(sc)
        elif kind == 1:
            h, pr, pi, sr, si = _s5_layer(
                h, valid, norm1_w[i], s5_w_in[j], s5_lam_re[j], s5_lam_im[j], s5_log_dt[j], s5_b_re[j],
                s5_b_im[j], s5_c_re[j], s5_c_im[j], s5_d[j], s5_w_out[j], state_s5_re[j], state_s5_im[j])
            outs["p_s5r"].append(pr); outs["p_s5i"].append(pi)
            outs["s_s5r"].append(sr); outs["s_s5i"].append(si)
        else:
            h, ps, psh, ss, ssh = _rwkv_layer(
                h, valid, norm1_w[i], rwkv_mu[j], rwkv_wr[j], rwkv_wk[j], rwkv_wv[j], rwkv_wo[j],
                rwkv_w0[j], rwkv_w1[j], rwkv_w2[j], rwkv_a0[j], rwkv_a1[j], rwkv_a2[j], rwkv_g1[j],
                rwkv_g2[j], rwkv_k_k[j], rwkv_k_a[j], rwkv_r_k[j], rwkv_ln_w[j], rwkv_ln_b[j],
                state_rwkv[j], state_rwkv_shift[j])
            outs["p_rwkv"].append(ps); outs["p_shift"].append(psh)
            outs["s_rwkv"].append(ss); outs["s_shift"].append(ssh)
        h = _ffn(h, norm2_w[i], ffn_w_gu[i].astype(BF16), ffn_w_down[i].astype(BF16))
    yf = _rmsnorm(h, normf_w)
    y_sample = yf[:DEC_BATCH * SAMPLE_ROWS_PER_SEQ].reshape(DEC_BATCH, SAMPLE_ROWS_PER_SEQ, D_MODEL)
    y_prompt = yf[MAIN_BLK0 * ROW_BLK:META_BLK0 * ROW_BLK].reshape(BATCH, SEQ, D_MODEL)
    st = lambda name: jnp.stack(outs[name])
    return (y_prompt, y_sample[:, :DEC_SEQ],
            st("p_ssd"), st("p_conv"), st("p_s5r"), st("p_s5i"), st("p_rwkv"), st("p_shift"),
            st("s_ssd"), st("s_conv"), st("s_s5r"), st("s_s5i"), st("s_rwkv"), st("s_shift"))
```

```python
import functools
import math

import jax
import jax.numpy as jnp
from jax import lax
from jax.experimental import pallas as pl
from jax.experimental.pallas import tpu as pltpu

F32 = jnp.float32
BF16 = jnp.bfloat16

D_MODEL = 2048
BATCH = 4
SEQ = 2048
DEPTH = 4
DEC_BATCH = 128
DEC_SEQ = 4
N_META = 16
N_MIXERS = 3
NORM_EPS = 1e-6

SSD_D_INNER = 2 * D_MODEL
SSD_HEAD_DIM = 64
SSD_HEADS = SSD_D_INNER // SSD_HEAD_DIM
SSD_STATE = 128
SSD_GROUPS = 8
SSD_HPG = SSD_HEADS // SSD_GROUPS
SSD_GN = SSD_GROUPS * SSD_STATE
SSD_CONV_K = 4
SSD_CONV_DIM = SSD_D_INNER + 2 * SSD_GN
SSD_ZX_DIM = SSD_D_INNER + SSD_CONV_DIM
SSD_GROUP_W = SSD_D_INNER // SSD_GROUPS

S5_GROUP_SIZE = 16
S5_GROUPS = D_MODEL // S5_GROUP_SIZE
S5_STATE = 64
S5_HDIM = S5_GROUPS * S5_STATE
S5_SBLK = 512
S5_NSB = S5_HDIM // S5_SBLK
S5_VBLK = S5_SBLK // S5_STATE * S5_GROUP_SIZE

RWKV_HEAD_DIM = 64
RWKV_HEADS = D_MODEL // RWKV_HEAD_DIM
RWKV_PAIRS = RWKV_HEADS // 2
RWKV_LORA_PAD = 128
RWKV_G_LORA = 256
RWKV_LN_EPS = 64e-5

FFN_HIDDEN = -(-8 * D_MODEL // (3 * 256)) * 256

LANES = 128
SUBLANES = 8
ROW_BLK = 128
SAMPLE_ROWS_PER_SEQ = SUBLANES
SAMPLE_BLKS = DEC_BATCH * SAMPLE_ROWS_PER_SEQ // ROW_BLK
MAIN_BLK0 = SAMPLE_BLKS
MAIN_BLKS_PER_B = SEQ // ROW_BLK
META_BLK0 = MAIN_BLK0 + BATCH * MAIN_BLKS_PER_B
N_BLKS = META_BLK0 + BATCH
T_ROWS = N_BLKS * ROW_BLK
PROMPT_CHUNKS = 1 + MAIN_BLKS_PER_B
META_PAD = ROW_BLK - N_META

MM_TM = 512
MM_TN = 512
VMEM_LIMIT = 48 * 2 ** 20


def _prompt_blk(b, c):
    return jnp.where(c == 0, META_BLK0 + b, MAIN_BLK0 + MAIN_BLKS_PER_B * b + c - 1)


def _cparams(*sem):
    return pltpu.CompilerParams(dimension_semantics=sem, vmem_limit_bytes=VMEM_LIMIT)


def _silu(x):
    return x * jax.nn.sigmoid(x)


def _softplus(x):
    return jnp.maximum(x, 0.0) + jnp.log(1.0 + jnp.exp(-jnp.abs(x)))


def _rms(x, w):
    return x * lax.rsqrt(jnp.mean(x * x, axis=-1, keepdims=True) + NORM_EPS) * w


def _dot(a, b):
    return jnp.dot(a, b, preferred_element_type=F32)


def _dot_nt(a, b):
    return lax.dot_general(a, b, (((1,), (1,)), ((), ())), preferred_element_type=F32)


def _dot_tn(a, b):
    return lax.dot_general(a, b, (((0,), (0,)), ((), ())), preferred_element_type=F32)


def _split_bf16(x):
    hi = x.astype(BF16)
    return hi, (x - hi.astype(F32)).astype(BF16)


def _drop_ref(kernel_fn, idx):
    def wrapped(*refs):
        return kernel_fn(*refs[:idx], *refs[idx + 1:])
    return wrapped


def _mm_kernel(*refs, normalize, glu, has_res):
    it = iter(refs)
    x_ref = next(it)
    nw_ref = next(it) if normalize else None
    w_ref = next(it)
    w2_ref = next(it) if glu else None
    res_ref = next(it) if has_res else None
    o_ref = next(it)
    xn_ref = next(it) if normalize else None
    if normalize:
        @pl.when(pl.program_id(1) == 0)
        def _():
            xn_ref[...] = _rms(x_ref[...], nw_ref[...]).astype(BF16)
        xb = xn_ref[...]
    else:
        xb = x_ref[...].astype(BF16)
    acc = _dot(xb, w_ref[...])
    if glu:
        acc = acc * jax.nn.sigmoid(_dot(xb, w2_ref[...]))
    if has_res:
        acc = acc + res_ref[...]
    o_ref[...] = acc.astype(o_ref.dtype)


def _matmul(x, w, n_out, *, norm_w=None, res=None, glu_off=None, out_dtype=F32, tn=MM_TN):
    m, k = x.shape
    normalize = norm_w is not None
    glu = glu_off is not None
    has_res = res is not None
    in_specs = [pl.BlockSpec((MM_TM, k), lambda i, j: (i, 0))]
    args = [x]
    if normalize:
        in_specs.append(pl.BlockSpec((1, k), lambda i, j: (0, 0)))
        args.append(norm_w.reshape(1, k))
    in_specs.append(pl.BlockSpec((k, tn), lambda i, j: (0, j)))
    args.append(w)
    if glu:
        in_specs.append(pl.BlockSpec((k, tn), lambda i, j: (0, j + glu_off)))
        args.append(w)
    if has_res:
        in_specs.append(pl.BlockSpec((MM_TM, tn), lambda i, j: (i, j)))
        args.append(res)
    return pl.pallas_call(
        functools.partial(_mm_kernel, normalize=normalize, glu=glu, has_res=has_res),
        grid=(m // MM_TM, n_out // tn),
        in_specs=in_specs,
        out_specs=pl.BlockSpec((MM_TM, tn), lambda i, j: (i, j)),
        out_shape=jax.ShapeDtypeStruct((m, n_out), out_dtype),
        scratch_shapes=[pltpu.VMEM((MM_TM, k), BF16)] if normalize else [],
        compiler_params=_cparams("parallel", "arbitrary"),
        name="matmul_n%d%s%s%s" % (n_out, "_norm" * normalize, "_glu" * glu, "_res" * has_res),
    )(*args)


def _rmsnorm_kernel(x_ref, w_ref, o_ref):
    o_ref[...] = _rms(x_ref[...], w_ref[...])


def _rmsnorm(x, w):
    m, k = x.shape
    return pl.pallas_call(
        _rmsnorm_kernel,
        grid=(m // MM_TM,),
        in_specs=[pl.BlockSpec((MM_TM, k), lambda i: (i, 0)), pl.BlockSpec((1, k), lambda i: (0, 0))],
        out_specs=pl.BlockSpec((MM_TM, k), lambda i: (i, 0)),
        out_shape=jax.ShapeDtypeStruct((m, k), F32),
        compiler_params=_cparams("parallel"),
        name="rmsnorm",
    )(x, w.reshape(1, k))


FFN_TF = 512
FFN_NF = FFN_HIDDEN // FFN_TF


def _ffn_kernel(x_ref, nw_ref, wg_ref, wu_ref, wd_ref, o_ref, xn_ref):
    f = pl.program_id(1)

    @pl.when(f == 0)
    def _():
        x = x_ref[...]
        xn_ref[...] = _rms(x, nw_ref[...]).astype(BF16)
        o_ref[...] = x

    xb = xn_ref[...]
    act = _silu(_dot(xb, wg_ref[...])) * _dot(xb, wu_ref[...])
    o_ref[...] += _dot(act.astype(BF16), wd_ref[...])


def _ffn(h, norm_w, w_gu, w_down):
    m = h.shape[0]
    return pl.pallas_call(
        _ffn_kernel,
        grid=(m // MM_TM, FFN_NF),
        in_specs=[
            pl.BlockSpec((MM_TM, D_MODEL), lambda i, f: (i, 0)),
            pl.BlockSpec((1, D_MODEL), lambda i, f: (0, 0)),
            pl.BlockSpec((D_MODEL, FFN_TF), lambda i, f: (0, f)),
            pl.BlockSpec((D_MODEL, FFN_TF), lambda i, f: (0, f + FFN_NF)),
            pl.BlockSpec((FFN_TF, D_MODEL), lambda i, f: (f, 0)),
        ],
        out_specs=pl.BlockSpec((MM_TM, D_MODEL), lambda i, f: (i, 0)),
        out_shape=jax.ShapeDtypeStruct((m, D_MODEL), F32),
        scratch_shapes=[pltpu.VMEM((MM_TM, D_MODEL), BF16)],
        compiler_params=_cparams("parallel", "arbitrary"),
        name="ffn",
    )(h, norm_w.reshape(1, D_MODEL), w_gu, w_gu, w_down)


def _dt_kernel(x_ref, nw_ref, wt_ref, bias_ref, o_ref):
    xb = _rms(x_ref[...], nw_ref[...]).astype(BF16)
    o_ref[...] = _softplus(_dot_nt(wt_ref[...], xb) + bias_ref[...])


def _ssd_dt(h, norm_w, w_dt_t, dt_bias):
    m = h.shape[0]
    return pl.pallas_call(
        _dt_kernel,
        grid=(m // MM_TM,),
        in_specs=[
            pl.BlockSpec((MM_TM, D_MODEL), lambda i: (i, 0)),
            pl.BlockSpec((1, D_MODEL), lambda i: (0, 0)),
            pl.BlockSpec((SSD_HEADS, D_MODEL), lambda i: (0, 0)),
            pl.BlockSpec((SSD_HEADS, 1), lambda i: (0, 0)),
        ],
        out_specs=pl.BlockSpec((SSD_HEADS, MM_TM), lambda i: (0, i)),
        out_shape=jax.ShapeDtypeStruct((SSD_HEADS, m), F32),
        compiler_params=_cparams("parallel"),
        name="ssd_dt",
    )(h, norm_w.reshape(1, D_MODEL), w_dt_t, dt_bias.reshape(SSD_HEADS, 1))


SSD_Q = ROW_BLK


def _ssd_kernel(z_ref, x_ref, b_ref, c_ref, valid_ref, validt_ref, dtt_ref, alog_ref, dskip_ref,
                cwx_ref, cwb_ref, cwc_ref, cbx_ref, cbb_ref, cbc_ref, cpx_ref, cpb_ref, cpc_ref,
                nw_ref, h0_ref, y_ref, hl_ref, st_ref, xpx_ref, xpb_ref, xpc_ref, *, rows, nch):
    c = pl.program_id(2)
    q = SSD_Q

    @pl.when(c == 0)
    def _():
        st_ref[...] = h0_ref[0]
        xpx_ref[0:SUBLANES] = cpx_ref[0]
        xpb_ref[0:SUBLANES] = cpb_ref[0]
        xpc_ref[0:SUBLANES] = cpc_ref[0]

    valid = valid_ref[...]

    def conv(raw_ref, xp_ref, w_ref, bias_ref):
        raw = raw_ref[...] * valid
        xp_ref[SUBLANES:SUBLANES + rows] = raw
        acc = bias_ref[...]
        for k in range(SSD_CONV_K):
            acc = acc + w_ref[k:k + 1, :] * xp_ref[pl.ds(SUBLANES - SSD_CONV_K + 1 + k, rows), :]
        if nch > 1:
            xp_ref[0:SUBLANES] = raw[rows - SUBLANES:rows]
        out = _silu(acc)
        if rows < q:
            out = jnp.concatenate([out, jnp.zeros((q - rows, out.shape[1]), F32)], axis=0)
        return out

    xs = conv(x_ref, xpx_ref, cwx_ref, cbx_ref)
    bm = conv(b_ref, xpb_ref, cwb_ref, cbb_ref).astype(BF16)
    cm = conv(c_ref, xpc_ref, cwc_ref, cbc_ref).astype(BF16)

    dtt = dtt_ref[0, 0] * validt_ref[0]
    a_col = -jnp.exp(alog_ref[0])
    lane = lax.broadcasted_iota(jnp.int32, (SSD_HPG, q), 1)
    cum = dtt * a_col
    s = 1
    while s < q:
        cum = cum + jnp.where(lane >= s, pltpu.roll(cum, s, axis=1), 0.0)
        s *= 2

    row_i = lax.broadcasted_iota(jnp.int32, (q, q), 0)
    col_i = lax.broadcasted_iota(jnp.int32, (q, q), 1)
    eye = row_i == col_i
    causal = row_i >= col_i

    def to_col(rowvec):
        return jnp.sum(jnp.where(eye, rowvec, 0.0), axis=1, keepdims=True)

    cb = _dot_nt(cm, bm)
    head_a = lax.broadcasted_iota(jnp.int32, (1, LANES), 1) < SSD_HEAD_DIM
    row_a = lax.broadcasted_iota(jnp.int32, (2 * SSD_HEAD_DIM, 1), 0) < SSD_HEAD_DIM
    dskip = dskip_ref[0]
    ys = []
    for p in range(SSD_HPG // 2):
        xh = xs[:, LANES * p:LANES * (p + 1)]
        xhb = xh.astype(BF16)
        st = st_ref[2 * p:2 * p + 2].reshape(2 * SSD_HEAD_DIM, SSD_STATE)
        yo = _dot_nt(cm, st.astype(BF16))
        yd, e_col, w_col, e_last = [], [], [], []
        for r in (2 * p, 2 * p + 1):
            cum_row = cum[r:r + 1, :]
            dt_row = dtt[r:r + 1, :]
            cum_col = to_col(cum_row)
            dt_col = to_col(dt_row)
            lmat = jnp.exp(jnp.where(causal, cum_col - cum_row, -jnp.inf))
            yd.append(_dot((cb * lmat * dt_row).astype(BF16), xhb))
            c_last = cum_row[:, q - 1:q]
            e_col.append(jnp.exp(cum_col))
            w_col.append(jnp.exp(c_last - cum_col) * dt_col)
            e_last.append(jnp.exp(c_last))
        d2 = jnp.where(head_a, dskip[:, 2 * p:2 * p + 1], dskip[:, 2 * p + 1:2 * p + 2])
        y2 = (jnp.where(head_a, yd[0], yd[1]) + yo * jnp.where(head_a, e_col[0], e_col[1]) + xh * d2)
        xw = (xh * jnp.where(head_a, w_col[0], w_col[1])).astype(BF16)
        st_new = st * jnp.where(row_a, e_last[0], e_last[1]) + _dot_tn(xw, bm)
        st_ref[2 * p:2 * p + 2] = st_new.reshape(2, SSD_HEAD_DIM, SSD_STATE)
        ys.append(y2)
    y = jnp.concatenate(ys, axis=1)[:rows]
    y = y * _silu(z_ref[...])
    y_ref[...] = _rms(y, nw_ref[...]).astype(y_ref.dtype)

    @pl.when(c == nch - 1)
    def _():
        hl_ref[0] = st_ref[...]


def _ssd_core(proj, valid, validt, dtt, a_log, d_skip, conv_w, conv_b, conv_prev8, norm_w, h0, *,
              prompt, y_init=None):
    nseq = h0.shape[0]
    if prompt:
        rows, nch = ROW_BLK, PROMPT_CHUNKS
        blk = _prompt_blk
    else:
        rows, nch = SAMPLE_ROWS_PER_SEQ, 1
        blk = lambda b, c: b
    gw = SSD_GROUP_W
    x_off = SSD_D_INNER // gw
    b_off = (2 * SSD_D_INNER) // SSD_STATE
    c_off = b_off + SSD_GROUPS
    cb_off = SSD_D_INNER // SSD_STATE
    cc_off = cb_off + SSD_GROUPS
    row_spec = lambda w, off: pl.BlockSpec((rows, w), lambda b, g, c: (blk(b, c), g + off))
    par_spec = lambda r, w, off: pl.BlockSpec((r, w), lambda b, g, c: (0, g + off))
    prev_spec = lambda w, off: pl.BlockSpec((1, SUBLANES, w), lambda b, g, c: (b, 0, g + off))
    st_spec = pl.BlockSpec((1, SSD_HPG, SSD_HEAD_DIM, SSD_STATE), lambda b, g, c: (b, g, 0, 0))
    in_specs = [
        row_spec(gw, 0), row_spec(gw, x_off), row_spec(SSD_STATE, b_off), row_spec(SSD_STATE, c_off),
        pl.BlockSpec((rows, 1), lambda b, g, c: (blk(b, c), 0)),
        pl.BlockSpec((1, 1, SSD_Q), lambda b, g, c: (blk(b, c), 0, 0)),
        pl.BlockSpec((1, 1, SSD_HPG, SSD_Q), lambda b, g, c: (blk(b, c), g, 0, 0)),
        pl.BlockSpec((1, SSD_HPG, 1), lambda b, g, c: (g, 0, 0)),
        pl.BlockSpec((1, 1, SSD_HPG), lambda b, g, c: (g, 0, 0)),
        par_spec(SSD_CONV_K, gw, 0), par_spec(SSD_CONV_K, SSD_STATE, cb_off),
        par_spec(SSD_CONV_K, SSD_STATE, cc_off),
        par_spec(1, gw, 0), par_spec(1, SSD_STATE, cb_off), par_spec(1, SSD_STATE, cc_off),
        prev_spec(gw, 0), prev_spec(SSD_STATE, cb_off), prev_spec(SSD_STATE, cc_off),
        par_spec(1, gw, 0),
        st_spec,
    ]
    args = [proj, proj, proj, proj, valid, validt, dtt,
            a_log.reshape(SSD_GROUPS, SSD_HPG, 1), d_skip.reshape(SSD_GROUPS, 1, SSD_HPG),
            conv_w, conv_w, conv_w, conv_b, conv_b, conv_b, conv_prev8, conv_prev8, conv_prev8,
            norm_w, h0]
    kernel_fn = functools.partial(_ssd_kernel, rows=rows, nch=nch)
    aliases = {}
    if y_init is not None:
        in_specs.append(pl.BlockSpec(memory_space=pl.ANY))
        args.append(y_init)
        aliases = {len(args) - 1: 0}
        kernel_fn = _drop_ref(kernel_fn, len(args) - 1)
    return pl.pallas_call(
        kernel_fn,
        grid=(nseq, SSD_GROUPS, nch),
        in_specs=in_specs,
        out_specs=[row_spec(gw, 0), st_spec],
        input_output_aliases=aliases,
        out_shape=[jax.ShapeDtypeStruct((T_ROWS, SSD_D_INNER), BF16),
                   jax.ShapeDtypeStruct(h0.shape, F32)],
        scratch_shapes=[
            pltpu.VMEM((SSD_HPG, SSD_HEAD_DIM, SSD_STATE), F32),
            pltpu.VMEM((SUBLANES + rows, gw), F32),
            pltpu.VMEM((SUBLANES + rows, SSD_STATE), F32),
            pltpu.VMEM((SUBLANES + rows, SSD_STATE), F32),
        ],
        compiler_params=_cparams("parallel", "parallel", "arbitrary"),
        name="ssd_core_prompt" if prompt else "ssd_core_sample",
    )(*args)


def _s5_abar(lam_re, lam_im, log_dt):
    dt = jnp.exp(log_dt)
    mag = jnp.exp(lam_re * dt)
    ang = lam_im * dt
    return mag * jnp.cos(ang), mag * jnp.sin(ang)


def _s5_pow_kernel(lr_ref, li_ref, ldt_ref, pr_ref, pi_ref):
    ar, ai = _s5_abar(lr_ref[...], li_ref[...], ldt_ref[...])
    row = lax.broadcasted_iota(jnp.int32, (SUBLANES, S5_HDIM), 0)
    pr, pi = ar, ai
    out_r = jnp.broadcast_to(ar, (SUBLANES, S5_HDIM))
    out_i = jnp.broadcast_to(ai, (SUBLANES, S5_HDIM))
    for k in range(1, SUBLANES):
        pr, pi = pr * ar - pi * ai, pr * ai + pi * ar
        out_r = jnp.where(row == k, pr, out_r)
        out_i = jnp.where(row == k, pi, out_i)
    pr_ref[...] = out_r
    pi_ref[...] = out_i


def _s5_bbar_kernel(lr_ref, li_ref, ldt_ref, br_ref, bi_ref, or_ref, oi_ref):
    lr, li = lr_ref[...], li_ref[...]
    ar, ai = _s5_abar(lr, li, ldt_ref[...])
    den = lr * lr + li * li
    f_re = ((ar - 1.0) * lr + ai * li) / den
    f_im = (ai * lr - (ar - 1.0) * li) / den
    br, bi = br_ref[...], bi_ref[...]
    or_ref[...] = f_re * br - f_im * bi
    oi_ref[...] = f_re * bi + f_im * br


def _s5_params(lam_re, lam_im, log_dt, b_re, b_im):
    ldt = jnp.repeat(log_dt, S5_STATE)
    row = lambda a: a.reshape(1, S5_HDIM)
    col = lambda a: a.reshape(S5_HDIM, 1)
    full = lambda shape: pl.BlockSpec(shape, lambda: (0,) * len(shape))
    pow_re, pow_im = pl.pallas_call(
        _s5_pow_kernel,
        in_specs=[full((1, S5_HDIM))] * 3,
        out_specs=[full((SUBLANES, S5_HDIM))] * 2,
        out_shape=[jax.ShapeDtypeStruct((SUBLANES, S5_HDIM), F32)] * 2,
    )(row(lam_re), row(lam_im), row(ldt))
    bb_re, bb_im = pl.pallas_call(
        _s5_bbar_kernel,
        in_specs=[full((S5_HDIM, 1))] * 3 + [full((S5_HDIM, S5_GROUP_SIZE))] * 2,
        out_specs=[full((S5_HDIM, S5_GROUP_SIZE))] * 2,
        out_shape=[jax.ShapeDtypeStruct((S5_HDIM, S5_GROUP_SIZE), F32)] * 2,
    )(col(lam_re), col(lam_im), col(ldt), b_re.reshape(S5_HDIM, S5_GROUP_SIZE),
      b_im.reshape(S5_HDIM, S5_GROUP_SIZE))
    return pow_re, pow_im, bb_re, bb_im


def _s5_block_diag(bb, c):
    gpb = S5_SBLK // S5_STATE
    eye = jnp.eye(gpb, dtype=F32)
    bb = bb.reshape(S5_NSB, gpb, S5_STATE, S5_GROUP_SIZE)
    w_in = eye[None, :, None, :, None] * jnp.transpose(bb, (0, 3, 1, 2))[:, None]
    w_in = w_in.reshape(S5_NSB, S5_VBLK, S5_SBLK)
    c = c.reshape(S5_NSB, gpb, S5_GROUP_SIZE, S5_STATE)
    w_out = eye[None, :, None, :, None] * jnp.transpose(c, (0, 1, 3, 2))[:, :, :, None, :]
    w_out = w_out.reshape(S5_NSB, S5_SBLK, S5_VBLK)
    return w_in.astype(BF16), w_out.astype(BF16)


def _s5_kernel(v_ref, valid_ref, wir_ref, wii_ref, wor_ref, woi_ref, pr_ref, pi_ref, d_ref,
               h0r_ref, h0i_ref, y_ref, hlr_ref, hli_ref, cr_ref, ci_ref, hr_ref, hi_ref, *,
               per_tile, nch, last_row):
    ntiles = ROW_BLK // SUBLANES
    if not per_tile:
        c = pl.program_id(2)

        @pl.when(c == 0)
        def _():
            cr_ref[...] = h0r_ref[0]
            ci_ref[...] = h0i_ref[0]

    v = v_ref[...]
    vb = (v * valid_ref[...]).astype(BF16)
    bu_r = _dot(vb, wir_ref[0])
    bu_i = _dot(vb, wii_ref[0])
    pr, pi = pr_ref[...], pi_ref[...]
    row = lax.broadcasted_iota(jnp.int32, (SUBLANES, S5_SBLK), 0)
    levels = []
    for s in (1, 2, 4):
        levels.append((s, jnp.where(row >= s, pr[s - 1:s], 0.0), jnp.where(row >= s, pi[s - 1:s], 0.0)))
    if not per_tile:
        car_r, car_i = cr_ref[...], ci_ref[...]
    for i in range(ntiles):
        xr = bu_r[SUBLANES * i:SUBLANES * (i + 1)]
        xi = bu_i[SUBLANES * i:SUBLANES * (i + 1)]
        for s, ar, ai in levels:
            sr = pltpu.roll(xr, s, axis=0)
            si = pltpu.roll(xi, s, axis=0)
            xr, xi = xr + ar * sr - ai * si, xi + ar * si + ai * sr
        if per_tile:
            car_r, car_i = h0r_ref[i], h0i_ref[i]
        hr = xr + pr * car_r - pi * car_i
        hi = xi + pr * car_i + pi * car_r
        hr_ref[SUBLANES * i:SUBLANES * (i + 1)] = hr
        hi_ref[SUBLANES * i:SUBLANES * (i + 1)] = hi
        if per_tile:
            hlr_ref[i] = hr[last_row:last_row + 1]
            hli_ref[i] = hi[last_row:last_row + 1]
        else:
            car_r, car_i = hr[last_row:last_row + 1], hi[last_row:last_row + 1]
    y = (_dot(hr_ref[...].astype(BF16), wor_ref[0]) - _dot(hi_ref[...].astype(BF16), woi_ref[0])
         + d_ref[...] * v)
    y_ref[...] = jax.nn.gelu(y).astype(y_ref.dtype)
    if not per_tile:
        cr_ref[...] = car_r
        ci_ref[...] = car_i

        @pl.when(c == nch - 1)
        def _():
            hlr_ref[0] = car_r
            hli_ref[0] = car_i


def _s5_core(v, valid, w_in_re, w_in_im, w_out_re, w_out_im, pow_re, pow_im, d_skip, h0_re, h0_im, *,
             prompt, y_init=None):
    nseq = h0_re.shape[0]
    if prompt:
        grid = (nseq, S5_NSB, PROMPT_CHUNKS)
        blk = _prompt_blk
        sidx = lambda b, s, c: (b, 0, s)
        spb = 1
        kw = dict(per_tile=False, nch=PROMPT_CHUNKS, last_row=SUBLANES - 1)
        sem = ("parallel", "parallel", "arbitrary")
    else:
        grid = (SAMPLE_BLKS, S5_NSB, 1)
        blk = lambda b, c: b
        sidx = lambda b, s, c: (b, 0, s)
        spb = ROW_BLK // SAMPLE_ROWS_PER_SEQ
        kw = dict(per_tile=True, nch=1, last_row=DEC_SEQ - 1)
        sem = ("parallel", "parallel", "arbitrary")
    row_spec = pl.BlockSpec((ROW_BLK, S5_VBLK), lambda b, s, c: (blk(b, c), s))
    st_spec = pl.BlockSpec((spb, 1, S5_SBLK), sidx)
    w_in_spec = pl.BlockSpec((1, S5_VBLK, S5_SBLK), lambda b, s, c: (s, 0, 0))
    w_out_spec = pl.BlockSpec((1, S5_SBLK, S5_VBLK), lambda b, s, c: (s, 0, 0))
    pow_spec = pl.BlockSpec((SUBLANES, S5_SBLK), lambda b, s, c: (0, s))
    in_specs = [row_spec, pl.BlockSpec((ROW_BLK, 1), lambda b, s, c: (blk(b, c), 0)),
                w_in_spec, w_in_spec, w_out_spec, w_out_spec, pow_spec, pow_spec,
                pl.BlockSpec((1, S5_VBLK), lambda b, s, c: (0, s)), st_spec, st_spec]
    args = [v, valid, w_in_re, w_in_im, w_out_re, w_out_im, pow_re, pow_im,
            d_skip.reshape(1, D_MODEL), h0_re, h0_im]
    kernel_fn = functools.partial(_s5_kernel, **kw)
    aliases = {}
    if y_init is not None:
        in_specs.append(pl.BlockSpec(memory_space=pl.ANY))
        args.append(y_init)
        aliases = {len(args) - 1: 0}
        kernel_fn = _drop_ref(kernel_fn, len(args) - 1)
    return pl.pallas_call(
        kernel_fn,
        grid=grid,
        in_specs=in_specs,
        out_specs=[row_spec, st_spec, st_spec],
        input_output_aliases=aliases,
        out_shape=[jax.ShapeDtypeStruct((T_ROWS, D_MODEL), BF16),
                   jax.ShapeDtypeStruct(h0_re.shape, F32), jax.ShapeDtypeStruct(h0_im.shape, F32)],
        scratch_shapes=[pltpu.VMEM((1, S5_SBLK), F32), pltpu.VMEM((1, S5_SBLK), F32),
                        pltpu.VMEM((ROW_BLK, S5_SBLK), F32), pltpu.VMEM((ROW_BLK, S5_SBLK), F32)],
        compiler_params=_cparams(*sem),
        name="s5_core_prompt" if prompt else "s5_core_sample",
    )(*args)


def _rwkv_proj_kernel(u_ref, p_ref, mu_ref, wr_ref, wk_ref, wv_ref, r_ref, k_ref, v_ref,
                      xr_ref, xk_ref, xv_ref):
    @pl.when(pl.program_id(1) == 0)
    def _():
        u = u_ref[...]
        d = p_ref[...] - u
        xr_ref[...] = (u + d * mu_ref[0:1, :]).astype(BF16)
        xk_ref[...] = (u + d * mu_ref[2:3, :]).astype(BF16)
        xv_ref[...] = (u + d * mu_ref[3:4, :]).astype(BF16)

    r_ref[...] = _dot(xr_ref[...], wr_ref[...])
    k_ref[...] = _dot(xk_ref[...], wk_ref[...])
    v_ref[...] = _dot(xv_ref[...], wv_ref[...])


def _rwkv_proj(u, prev, mu, wr, wk, wv):
    m = u.shape[0]
    row = pl.BlockSpec((MM_TM, D_MODEL), lambda i, j: (i, 0))
    wsp = pl.BlockSpec((D_MODEL, MM_TN), lambda i, j: (0, j))
    osp = pl.BlockSpec((MM_TM, MM_TN), lambda i, j: (i, j))
    return pl.pallas_call(
        _rwkv_proj_kernel,
        grid=(m // MM_TM, D_MODEL // MM_TN),
        in_specs=[row, row, pl.BlockSpec((6, D_MODEL), lambda i, j: (0, 0)), wsp, wsp, wsp],
        out_specs=[osp, osp, osp],
        out_shape=[jax.ShapeDtypeStruct((m, D_MODEL), F32)] * 3,
        scratch_shapes=[pltpu.VMEM((MM_TM, D_MODEL), BF16)] * 3,
        compiler_params=_cparams("parallel", "arbitrary"),
        name="rwkv_proj",
    )(u, prev, mu, wr, wk, wv)


RWKV_LORA_TM = 256


def _rwkv_lora_kernel(u_ref, p_ref, mu_ref, w1_ref, w2_ref, a1_ref, a2_ref, g1_ref, g2_ref,
                      w0_ref, a0_ref, lw_ref, a_ref, g_ref):
    u = u_ref[...]
    d = p_ref[...] - u
    xw = (u + d * mu_ref[1:2, :]).astype(BF16)
    xa = (u + d * mu_ref[4:5, :]).astype(BF16)
    xg = (u + d * mu_ref[5:6, :]).astype(BF16)
    wpre = w0_ref[...] + _dot(jnp.tanh(_dot(xw, w1_ref[...])).astype(BF16), w2_ref[...])
    w = -_softplus(-wpre) - 0.5
    lw_ref[...] = -jnp.exp(w)
    a_ref[...] = jax.nn.sigmoid(a0_ref[...] + _dot(_dot(xa, a1_ref[...]).astype(BF16), a2_ref[...]))
    g_ref[...] = _dot(jax.nn.sigmoid(_dot(xg, g1_ref[...])).astype(BF16), g2_ref[...])


def _rwkv_lora(u, prev, mu, w1, w2, a1, a2, g1, g2, w0, a0):
    m = u.shape[0]
    row = pl.BlockSpec((RWKV_LORA_TM, D_MODEL), lambda i: (i, 0))
    full = lambda a: pl.BlockSpec(a.shape, lambda i: (0, 0))
    vec = pl.BlockSpec((1, D_MODEL), lambda i: (0, 0))
    return pl.pallas_call(
        _rwkv_lora_kernel,
        grid=(m // RWKV_LORA_TM,),
        in_specs=[row, row, pl.BlockSpec((6, D_MODEL), lambda i: (0, 0)),
                  full(w1), full(w2), full(a1), full(a2), full(g1), full(g2), vec, vec],
        out_specs=[row, row, row],
        out_shape=[jax.ShapeDtypeStruct((m, D_MODEL), F32)] * 3,
        compiler_params=_cparams("parallel"),
        name="rwkv_lora",
    )(u, prev, mu, w1, w2, a1, a2, g1, g2, w0.reshape(1, D_MODEL), a0.reshape(1, D_MODEL))


def _block_ones():
    r = lax.broadcasted_iota(jnp.int32, (LANES, LANES), 0) // RWKV_HEAD_DIM
    c = lax.broadcasted_iota(jnp.int32, (LANES, LANES), 1) // RWKV_HEAD_DIM
    return (r == c).astype(BF16)


def _head_sum(x, bo):
    hi, lo = _split_bf16(x)
    return _dot(hi, bo) + _dot(lo, bo)


RWKV_PG = 8
RWKV_NPG = RWKV_PAIRS // RWKV_PG
RWKV_PG_W = RWKV_PG * LANES


class _RwkvConsts:
    def __init__(self):
        self.bo = _block_ones()
        lane = lax.broadcasted_iota(jnp.int32, (RWKV_HEAD_DIM, LANES), 1)
        row = lax.broadcasted_iota(jnp.int32, (RWKV_HEAD_DIM, LANES), 0)
        self.diag = ((lane % RWKV_HEAD_DIM) == row).astype(BF16)
        self.head_a = lane < RWKV_HEAD_DIM
        self.row8 = lax.broadcasted_iota(jnp.int32, (SUBLANES, LANES), 0)


def _rwkv_step(s2, s2b, kk_t, dec_t, kka_t, k_t, v_t, r_lhs, cst):
    bo = cst.bo
    sa = _dot_nt(s2b, bo * (-kk_t).astype(BF16))
    v_hi, v_lo = _split_bf16(v_t)
    v2 = _dot_nt(cst.diag, bo * v_hi) + _dot_nt(cst.diag, bo * v_lo)
    s2n = s2 * dec_t + sa * kka_t + v2 * k_t
    s2nb = s2n.astype(BF16)
    zero = jnp.zeros_like(s2nb)
    wy = jnp.concatenate([jnp.where(cst.head_a, s2nb, zero), jnp.where(cst.head_a, zero, s2nb)], axis=0)
    return s2n, s2nb, _dot_nt(r_lhs, wy)


def _rwkv_steps(s2, tiles, nsteps, cst):
    kk8, dec8, kka8, k8, v8, r8 = tiles
    s2b = s2.astype(BF16)
    y = jnp.zeros((SUBLANES, LANES), F32)
    for i in range(nsteps):
        r_lhs = jnp.where(cst.row8 == i, r8, 0.0).astype(BF16)
        s2, s2b, yi = _rwkv_step(s2, s2b, kk8[i:i + 1], dec8[i:i + 1], kka8[i:i + 1], k8[i:i + 1],
                                 v8[i:i + 1], r_lhs, cst)
        y = y + yi
    return s2, y


def _rwkv_prepare(k_ref, a_ref, v_ref, valid, kk_p, ka_p, kk_s, kka_s, km_s, vm_s, bo):
    for p in range(RWKV_PG):
        sl = slice(LANES * p, LANES * (p + 1))
        k, a = k_ref[:, sl], a_ref[:, sl]
        kkr = k * kk_p[:, sl]
        kk = kkr / jnp.maximum(jnp.sqrt(_head_sum(kkr * kkr, bo)), 1e-12)
        kk_s[:, sl] = kk
        kka_s[:, sl] = kk * a
        km_s[:, sl] = k * (1.0 + (a - 1.0) * ka_p[:, sl])
        vm_s[:, sl] = v_ref[:, sl] * valid


def _rwkv_finish(yraw_s, r_ref, km_s, vm_s, g_ref, rk_p, lnw_p, lnb_p, o_ref, bo):
    inv = 1.0 / RWKV_HEAD_DIM
    for p in range(RWKV_PG):
        sl = slice(LANES * p, LANES * (p + 1))
        y = yraw_s[:, sl]
        d = y - _head_sum(y, bo) * inv
        yn = d * lax.rsqrt(_head_sum(d * d, bo) * inv + RWKV_LN_EPS) * lnw_p[:, sl] + lnb_p[:, sl]
        bonus = _head_sum(r_ref[:, sl] * km_s[:, sl] * rk_p[:, sl], bo) * vm_s[:, sl]
        o_ref[:, sl] = ((yn + bonus) * g_ref[:, sl]).astype(o_ref.dtype)


def _rwkv_tiles(base, p, refs):
    return [ref[pl.ds(base, SUBLANES), pl.ds(LANES * p, LANES)] for ref in refs]


RWKV_C = 64
RWKV_PROMPT_CHUNKS = (ROW_BLK + SEQ) // RWKV_C
RWKV_INV_LEVELS = 6


def _prompt_blk64(b, c):
    per = ROW_BLK // RWKV_C
    return jnp.where(c < per, per * (META_BLK0 + b) + c,
                     per * (MAIN_BLK0 + MAIN_BLKS_PER_B * b) + c - per)


def _pair_bd(x, head_a):
    zero = jnp.zeros_like(x)
    return jnp.concatenate([jnp.where(head_a, x, zero), jnp.where(head_a, zero, x)], axis=0)


def _rwkv_chunk_kernel(r_ref, k_ref, v_ref, lw_ref, a_ref, g_ref, valid_ref, kk_p, ka_p, rk_p,
                       lnw_p, lnb_p, s0_ref, o_ref, sl_ref, s_ref):
    c = pl.program_id(2)
    cl = RWKV_C

    @pl.when(c == 0)
    def _():
        s_ref[...] = s0_ref[0]

    bo = _block_ones()
    lane = lax.broadcasted_iota(jnp.int32, (cl, LANES), 1)
    row = lax.broadcasted_iota(jnp.int32, (cl, LANES), 0)
    head_a = lane < RWKV_HEAD_DIM
    s_idx = lane % RWKV_HEAD_DIM
    strict = s_idx < row
    incl = s_idx <= row
    eye2 = (s_idx == row).astype(F32)
    tri = (lax.broadcasted_iota(jnp.int32, (cl, cl), 0)
           >= lax.broadcasted_iota(jnp.int32, (cl, cl), 1)).astype(BF16)
    bd = lambda x: _pair_bd(x, head_a)
    b16 = lambda x: x.astype(BF16)
    halves = lambda x: x[:cl] + x[cl:]
    valid = valid_ref[...]
    inv = 1.0 / RWKV_HEAD_DIM

    def pair_stages(p):
        sl = slice(LANES * p, LANES * (p + 1))
        k, a, r = k_ref[:, sl], a_ref[:, sl], r_ref[:, sl]
        vm = v_ref[:, sl] * valid
        lw = lw_ref[:, sl] * valid
        kkr = k * kk_p[:, sl]
        kk = kkr / jnp.maximum(jnp.sqrt(_head_sum(kkr * kkr, bo)), 1e-12) * valid
        bv = kk * a
        km = k * (1.0 + (a - 1.0) * ka_p[:, sl])
        lw_hi, lw_lo = _split_bf16(lw)
        g = _dot(tri, lw_hi) + _dot(tri, lw_lo)
        yield
        g_end = g[cl - 1:cl]
        e_neg = jnp.exp(-g)
        e_end = jnp.exp(g_end - g)
        at = -kk * jnp.exp(g - lw)
        rt = r * jnp.exp(g)
        x2 = b16(jnp.concatenate([at, rt], axis=0))
        gb = _dot_nt(x2, bd(b16(bv * e_neg)))
        gk = _dot_nt(x2, bd(b16(km * e_neg)))
        yield
        n = jnp.where(strict, gb[:cl], 0.0)
        mrb = jnp.where(incl, gb[cl:], 0.0)
        lak = jnp.where(strict, gk[:cl], 0.0)
        mrk = jnp.where(incl, gk[cl:], 0.0)
        pk, t = n, eye2
        for level in range(RWKV_INV_LEVELS):
            if level < RWKV_INV_LEVELS - 1:
                res = _dot(b16(jnp.concatenate([pk, t], axis=0)), bd(b16(pk)))
                pk, t = res[:cl], t + res[cl:]
            else:
                t = t + _dot(b16(t), bd(b16(pk)))
            yield
        n_hi, n_lo = _split_bf16(n)
        t_hi, t_lo = _split_bf16(t)
        nt = halves(_dot(jnp.concatenate([n_hi, n_lo], axis=0), bd(t_hi))) + _dot(n_hi, bd(t_lo))
        wy = _dot(b16(jnp.concatenate([lak, mrk], axis=0)), bd(b16(vm)))
        yield
        t = t + _dot(t_hi, bd(b16(eye2 - t + nt)))
        yield
        tt = jnp.concatenate(_split_bf16(t), axis=0)
        pmat = halves(_dot(tt, bd(b16(at))))
        q = halves(_dot(tt, bd(b16(wy[:cl]))))
        yield
        s2 = s_ref[p]
        p_hi, p_lo = _split_bf16(pmat)
        res = _dot_nt(jnp.concatenate([p_hi, p_lo, b16(rt)], axis=0), bd(b16(s2)))
        yield
        u = res[:cl] + res[cl:2 * cl] + q
        y = res[2 * cl:] + _dot(b16(mrb), bd(b16(u))) + wy[cl:]
        v_hi, v_lo = _split_bf16(vm)
        kg_hi, kg_lo = _split_bf16(km * e_end)
        full = _dot_tn(jnp.concatenate([b16(u), v_hi, v_lo, v_hi], axis=0),
                       jnp.concatenate([b16(bv * e_end), kg_hi, kg_hi, kg_lo], axis=0))
        yield
        s_ref[p] = s2 * jnp.exp(g_end) + jnp.where(head_a, full[:RWKV_HEAD_DIM], full[RWKV_HEAD_DIM:])
        d = y - _head_sum(y, bo) * inv
        yield
        yn = d * lax.rsqrt(_head_sum(d * d, bo) * inv + RWKV_LN_EPS) * lnw_p[:, sl] + lnb_p[:, sl]
        bonus = _head_sum(r * km * rk_p[:, sl], bo) * vm
        o_ref[:, sl] = ((yn + bonus) * g_ref[:, sl]).astype(o_ref.dtype)

    live = [pair_stages(p) for p in range(RWKV_PG)]
    while live:
        live = [gen for gen in live if next(gen, True) is None]

    @pl.when(c == RWKV_PROMPT_CHUNKS - 1)
    def _():
        sl_ref[0] = s_ref[...]


RWKV_SAMPLE_SEQS = 8
RWKV_SAMPLE_ROWS = RWKV_SAMPLE_SEQS * SAMPLE_ROWS_PER_SEQ


def _rwkv_sample_kernel(r_ref, k_ref, v_ref, lw_ref, a_ref, g_ref, valid_ref, kk_p, ka_p, rk_p,
                        lnw_p, lnb_p, s0_ref, o_ref, sl_ref, kk_s, kka_s, km_s, vm_s, yraw_s):
    cst = _RwkvConsts()
    _rwkv_prepare(k_ref, a_ref, v_ref, valid_ref[...], kk_p, ka_p, kk_s, kka_s, km_s, vm_s, cst.bo)

    def body(i, carry):
        base = pl.multiple_of(i * SAMPLE_ROWS_PER_SEQ, SAMPLE_ROWS_PER_SEQ)
        for p in range(RWKV_PG):
            tiles = _rwkv_tiles(base, p, (kk_s, lw_ref, kka_s, km_s, vm_s, r_ref))
            tiles[1] = jnp.exp(tiles[1])
            s2, y = _rwkv_steps(s0_ref[i, p], tiles, DEC_SEQ, cst)
            sl_ref[i, p] = s2
            yraw_s[pl.ds(base, SUBLANES), pl.ds(LANES * p, LANES)] = y
        return carry

    lax.fori_loop(0, RWKV_SAMPLE_SEQS, body, 0)
    _rwkv_finish(yraw_s, r_ref, km_s, vm_s, g_ref, rk_p, lnw_p, lnb_p, o_ref, cst.bo)


def _rwkv_core(r, k, v, dec, a, g, valid, k_k, k_a, r_k, ln_w, ln_b, s0, *, prompt, y_init=None):
    nseq = s0.shape[0]
    vec = lambda a_: a_.reshape(1, D_MODEL)
    if prompt:
        rows = RWKV_C
        grid = (nseq, RWKV_NPG, RWKV_PROMPT_CHUNKS)
        rmap = lambda b, q, c: (_prompt_blk64(b, c), q)
        vmap = lambda b, q, c: (_prompt_blk64(b, c), 0)
        spb = 1
        kernel_fn = _rwkv_chunk_kernel
        scratch = [pltpu.VMEM((RWKV_PG, RWKV_HEAD_DIM, LANES), F32)]
    else:
        rows = RWKV_SAMPLE_ROWS
        grid = (nseq // RWKV_SAMPLE_SEQS, RWKV_NPG, 1)
        rmap = lambda b, q, c: (b, q)
        vmap = lambda b, q, c: (b, 0)
        spb = RWKV_SAMPLE_SEQS
        kernel_fn = _rwkv_sample_kernel
        scratch = [pltpu.VMEM((rows, RWKV_PG_W), F32)] * 5
    sem = ("parallel", "parallel", "arbitrary")
    row_spec = pl.BlockSpec((rows, RWKV_PG_W), rmap)
    vec_spec = pl.BlockSpec((1, RWKV_PG_W), lambda b, q, c: (0, q))
    st_spec = pl.BlockSpec((spb, RWKV_PG, RWKV_HEAD_DIM, LANES), lambda b, q, c: (b, q, 0, 0))
    in_specs = [row_spec] * 6 + [pl.BlockSpec((rows, 1), vmap)] + [vec_spec] * 5 + [st_spec]
    args = [r, k, v, dec, a, g, valid, vec(k_k), vec(k_a), vec(r_k), vec(ln_w), vec(ln_b), s0]
    aliases = {}
    if y_init is not None:
        in_specs.append(pl.BlockSpec(memory_space=pl.ANY))
        args.append(y_init)
        aliases = {len(args) - 1: 0}
        kernel_fn = _drop_ref(kernel_fn, len(args) - 1)
    return pl.pallas_call(
        kernel_fn,
        grid=grid,
        in_specs=in_specs,
        out_specs=[row_spec, st_spec],
        input_output_aliases=aliases,
        out_shape=[jax.ShapeDtypeStruct((T_ROWS, D_MODEL), BF16), jax.ShapeDtypeStruct(s0.shape, F32)],
        scratch_shapes=scratch,
        compiler_params=_cparams(*sem),
        name="rwkv_core_prompt" if prompt else "rwkv_core_sample",
    )(*args)


def _row_ids():
    sample_t0 = jnp.arange(DEC_BATCH) * SAMPLE_ROWS_PER_SEQ
    main0 = MAIN_BLK0 * ROW_BLK + jnp.arange(BATCH) * SEQ
    meta0 = META_BLK0 * ROW_BLK + jnp.arange(BATCH) * ROW_BLK + META_PAD
    return sample_t0, main0, meta0


def _valid_mask():
    r = jnp.arange(T_ROWS)
    sample = (r < MAIN_BLK0 * ROW_BLK) & (r % SAMPLE_ROWS_PER_SEQ < DEC_SEQ)
    main = (r >= MAIN_BLK0 * ROW_BLK) & (r < META_BLK0 * ROW_BLK)
    meta = (r >= META_BLK0 * ROW_BLK) & (r % ROW_BLK >= META_PAD)
    return (sample | main | meta).astype(F32)


def _pad_conv_prev(prev):
    return jnp.pad(prev, ((0, 0), (SUBLANES - SSD_CONV_K + 1, 0), (0, 0)))


def _ssd_layer(h, valid, norm_w, w_in, conv_w, conv_b, dt_bias, a_log, d_skip, gnorm_w, w_out,
               state, state_conv):
    sample_t0, main0, meta0 = _row_ids()
    w_in_b = w_in.astype(BF16)
    proj = _matmul(h, w_in_b, SSD_ZX_DIM, norm_w=norm_w)
    dtt = _ssd_dt(h, norm_w, w_in_b[:, SSD_ZX_DIM:].T, dt_bias)
    xbc = lambda rows: proj[rows][..., SSD_D_INNER:]
    tail = jnp.arange(SSD_CONV_K - 1)
    validt = valid.reshape(N_BLKS, 1, ROW_BLK)
    dtt_p = jnp.transpose(dtt.reshape(SSD_GROUPS, SSD_HPG, N_BLKS, ROW_BLK), (2, 0, 1, 3))
    gn = gnorm_w.reshape(1, SSD_D_INNER)
    cb = conv_b.reshape(1, SSD_CONV_DIM)
    h0_meta = jnp.zeros((BATCH, SSD_HEADS, SSD_HEAD_DIM, SSD_STATE), F32)
    prev_p = jnp.zeros((BATCH, SUBLANES, SSD_CONV_DIM), F32)
    y, p_state = _ssd_core(proj, valid.reshape(T_ROWS, 1), validt, dtt_p, a_log, d_skip, conv_w, cb,
                           prev_p, gn, h0_meta, prompt=True)
    ns = SAMPLE_ROWS_PER_SEQ
    dtt_s = dtt[:, :DEC_BATCH * ns].reshape(SSD_GROUPS, SSD_HPG, DEC_BATCH, ns)
    dtt_s = jnp.pad(jnp.transpose(dtt_s, (2, 0, 1, 3)), ((0, 0), (0, 0), (0, 0), (0, SSD_Q - ns)))
    validt_s = jnp.pad(valid[:DEC_BATCH * ns].reshape(DEC_BATCH, 1, ns), ((0, 0), (0, 0), (0, SSD_Q - ns)))
    y, s_state = _ssd_core(proj, valid.reshape(T_ROWS, 1), validt_s, dtt_s, a_log, d_skip, conv_w, cb,
                           _pad_conv_prev(state_conv), gn, state, prompt=False, y_init=y)
    h = _matmul(y, w_out.astype(BF16), D_MODEL, res=h)
    p_conv = xbc(main0[:, None] + SEQ - (SSD_CONV_K - 1) + tail[None, :])
    s_conv = xbc(sample_t0[:, None] + DEC_SEQ - (SSD_CONV_K - 1) + tail[None, :])
    return h, p_state, p_conv, s_state, s_conv


def _s5_layer(h, valid, norm_w, w_in, lam_re, lam_im, log_dt, b_re, b_im, c_re, c_im, d_skip, w_out,
              state_re, state_im):
    v = _matmul(h, w_in.astype(BF16), D_MODEL, norm_w=norm_w)
    pow_re, pow_im, bb_re, bb_im = _s5_params(lam_re, lam_im, log_dt, b_re, b_im)
    wi_re, wo_re = _s5_block_diag(bb_re, c_re)
    wi_im, wo_im = _s5_block_diag(bb_im, c_im)
    vcol = valid.reshape(T_ROWS, 1)
    zero = jnp.zeros((BATCH, 1, S5_HDIM), F32)
    y, p_re, p_im = _s5_core(v, vcol, wi_re, wi_im, wo_re, wo_im, pow_re, pow_im, d_skip, zero, zero,
                             prompt=True)
    y, s_re, s_im = _s5_core(v, vcol, wi_re, wi_im, wo_re, wo_im, pow_re, pow_im, d_skip,
                             state_re.reshape(DEC_BATCH, 1, S5_HDIM),
                             state_im.reshape(DEC_BATCH, 1, S5_HDIM), prompt=False, y_init=y)
    h = _matmul(y, w_out.astype(BF16), D_MODEL, res=h, glu_off=D_MODEL // MM_TN)
    shp = lambda a, n: a.reshape(n, S5_GROUPS, S5_STATE)
    return h, shp(p_re, BATCH), shp(p_im, BATCH), shp(s_re, DEC_BATCH), shp(s_im, DEC_BATCH)


def _to_pairs(s):
    n = s.shape[0]
    s = s.reshape(n, RWKV_PAIRS, 2, RWKV_HEAD_DIM, RWKV_HEAD_DIM)
    return jnp.transpose(s, (0, 1, 3, 2, 4)).reshape(n, RWKV_PAIRS, RWKV_HEAD_DIM, LANES)


def _from_pairs(s):
    n = s.shape[0]
    s = s.reshape(n, RWKV_PAIRS, RWKV_HEAD_DIM, 2, RWKV_HEAD_DIM)
    return jnp.transpose(s, (0, 1, 3, 2, 4)).reshape(n, RWKV_HEADS, RWKV_HEAD_DIM, RWKV_HEAD_DIM)


def _pad_lora(w_down, w_up):
    n = w_down.shape[1]
    return (jnp.pad(w_down, ((0, 0), (0, RWKV_LORA_PAD - n))).astype(BF16),
            jnp.pad(w_up, ((0, RWKV_LORA_PAD - n), (0, 0))).astype(BF16))


def _rwkv_layer(h, valid, norm_w, mu, wr, wk, wv, wo, w0, w1, w2, a0, a1, a2, g1, g2, k_k, k_a, r_k,
                ln_w, ln_b, state, state_shift):
    sample_t0, main0, meta0 = _row_ids()
    u = _rmsnorm(h, norm_w)
    prev = jnp.concatenate([jnp.zeros((1, D_MODEL), F32), u[:-1]], axis=0)
    prev = prev.at[sample_t0].set(state_shift)
    prev = prev.at[main0].set(u[meta0 + N_META - 1])
    prev = prev.at[meta0].set(0.0)
    r, k, v = _rwkv_proj(u, prev, mu, wr.astype(BF16), wk.astype(BF16), wv.astype(BF16))
    w1p, w2p = _pad_lora(w1, w2)
    a1p, a2p = _pad_lora(a1, a2)
    dec, a, g = _rwkv_lora(u, prev, mu, w1p, w2p, a1p, a2p, g1.astype(BF16), g2.astype(BF16), w0, a0)
    vcol = valid.reshape(T_ROWS, 1)
    s0_p = jnp.zeros((BATCH, RWKV_PAIRS, RWKV_HEAD_DIM, LANES), F32)
    y, p_state = _rwkv_core(r, k, v, dec, a, g, vcol, k_k, k_a, r_k, ln_w, ln_b, s0_p, prompt=True)
    y, s_state = _rwkv_core(r, k, v, dec, a, g, vcol, k_k, k_a, r_k, ln_w, ln_b, _to_pairs(state),
                            prompt=False, y_init=y)
    h = _matmul(y, wo.astype(BF16), D_MODEL, res=h)
    return (h, _from_pairs(p_state), u[main0 + SEQ - 1], _from_pairs(s_state),
            u[sample_t0 + DEC_SEQ - 1])


def kernel(x_prompt, x_sample, state_ssd, state_ssd_conv, state_s5_re, state_s5_im, state_rwkv,
           state_rwkv_shift, meta_tokens, norm1_w, norm2_w, normf_w, ffn_w_gu, ffn_w_down,
           ssd_w_in, ssd_conv_w, ssd_conv_b, ssd_dt_bias, ssd_a_log, ssd_d, ssd_norm_w, ssd_w_out,
           s5_w_in, s5_lam_re, s5_lam_im, s5_log_dt, s5_b_re, s5_b_im, s5_c_re, s5_c_im, s5_d, s5_w_out,
           rwkv_mu, rwkv_wr, rwkv_wk, rwkv_wv, rwkv_wo, rwkv_w0, rwkv_w1, rwkv_w2, rwkv_a0, rwkv_a1,
           rwkv_a2, rwkv_g1, rwkv_g2, rwkv_k_k, rwkv_k_a, rwkv_r_k, rwkv_ln_w, rwkv_ln_b):
    valid = _valid_mask()
    sample = jnp.pad(x_sample, ((0, 0), (0, SAMPLE_ROWS_PER_SEQ - DEC_SEQ), (0, 0)))
    meta = jnp.pad(jnp.broadcast_to(meta_tokens[None], (BATCH, N_META, D_MODEL)),
                   ((0, 0), (META_PAD, 0), (0, 0)))
    h = jnp.concatenate([sample.reshape(-1, D_MODEL), x_prompt.reshape(-1, D_MODEL),
                         meta.reshape(-1, D_MODEL)], axis=0).astype(F32)
    outs = {name: [] for name in ("p_ssd", "p_conv", "p_s5r", "p_s5i", "p_rwkv", "p_shift",
                                  "s_ssd", "s_conv", "s_s5r", "s_s5i", "s_rwkv", "s_shift")}
    for i in range(DEPTH):
        kind, j = i % N_MIXERS, i // N_MIXERS
        if kind == 0:
            h, ps, pc, ss, sc = _ssd_layer(
                h, valid, norm1_w[i], ssd_w_in[j], ssd_conv_w[j], ssd_conv_b[j], ssd_dt_bias[j],
                ssd_a_log[j], ssd_d[j], ssd_norm_w[j], ssd_w_out[j], state_ssd[j], state_ssd_conv[j])
            outs["p_ssd"].append(ps); outs["p_conv"].append(pc)
            outs["s_ssd"].append(ss); outs["s_conv"].append(sc)
        elif kind == 1:
            h, pr, pi, sr, si = _s5_layer(
                h, valid, norm1_w[i], s5_w_in[j], s5_lam_re[j], s5_lam_im[j], s5_log_dt[j], s5_b_re[j],
                s5_b_im[j], s5_c_re[j], s5_c_im[j], s5_d[j], s5_w_out[j], state_s5_re[j], state_s5_im[j])
            outs["p_s5r"].append(pr); outs["p_s5i"].append(pi)
            outs["s_s5r"].append(sr); outs["s_s5i"].append(si)
        else:
            h, ps, psh, ss, ssh = _rwkv_layer(
                h, valid, norm1_w[i], rwkv_mu[j], rwkv_wr[j], rwkv_wk[j], rwkv_wv[j], rwkv_wo[j],
                rwkv_w0[j], rwkv_w1[j], rwkv_w2[j], rwkv_a0[j], rwkv_a1[j], rwkv_a2[j], rwkv_g1[j],
                rwkv_g2[j], rwkv_k_k[j], rwkv_k_a[j], rwkv_r_k[j], rwkv_ln_w[j], rwkv_ln_b[j],
                state_rwkv[j], state_rwkv_shift[j])
            outs["p_rwkv"].append(ps); outs["p_shift"].append(psh)
            outs["s_rwkv"].append(ss); outs["s_shift"].append(ssh)
        h = _ffn(h, norm2_w[i], ffn_w_gu[i].astype(BF16), ffn_w_down[i].astype(BF16))
    yf = _rmsnorm(h, normf_w)
    y_sample = yf[:DEC_BATCH * SAMPLE_ROWS_PER_SEQ].reshape(DEC_BATCH, SAMPLE_ROWS_PER_SEQ, D_MODEL)
    y_prompt = yf[MAIN_BLK0 * ROW_BLK:META_BLK0 * ROW_BLK].reshape(BATCH, SEQ, D_MODEL)
    st = lambda name: jnp.stack(outs[name])
    return (y_prompt, y_sample[:, :DEC_SEQ],
            st("p_ssd"), st("p_conv"), st("p_s5r"), st("p_s5i"), st("p_rwkv"), st("p_shift"),
            st("s_ssd"), st("s_conv"), st("s_s5r"), st("s_s5i"), st("s_rwkv"), st("s_shift"))
```

```python
import functools
import math

import jax
import jax.numpy as jnp
from jax import lax
from jax.experimental import pallas as pl
from jax.experimental.pallas import tpu as pltpu

F32 = jnp.float32
BF16 = jnp.bfloat16

D_MODEL = 2048
BATCH = 4
SEQ = 2048
DEPTH = 4
DEC_BATCH = 128
DEC_SEQ = 4
N_META = 16
N_MIXERS = 3
NORM_EPS = 1e-6

SSD_D_INNER = 2 * D_MODEL
SSD_HEAD_DIM = 64
SSD_HEADS = SSD_D_INNER // SSD_HEAD_DIM
SSD_STATE = 128
SSD_GROUPS = 8
SSD_HPG = SSD_HEADS // SSD_GROUPS
SSD_GN = SSD_GROUPS * SSD_STATE
SSD_CONV_K = 4
SSD_CONV_DIM = SSD_D_INNER + 2 * SSD_GN
SSD_ZX_DIM = SSD_D_INNER + SSD_CONV_DIM
SSD_GROUP_W = SSD_D_INNER // SSD_GROUPS

S5_GROUP_SIZE = 16
S5_GROUPS = D_MODEL // S5_GROUP_SIZE
S5_STATE = 64
S5_HDIM = S5_GROUPS * S5_STATE
S5_SBLK = 1024
S5_NSB = S5_HDIM // S5_SBLK
S5_VBLK = S5_SBLK // S5_STATE * S5_GROUP_SIZE

RWKV_HEAD_DIM = 64
RWKV_HEADS = D_MODEL // RWKV_HEAD_DIM
RWKV_PAIRS = RWKV_HEADS // 2
RWKV_LORA_PAD = 128
RWKV_G_LORA = 256
RWKV_LN_EPS = 64e-5

FFN_HIDDEN = -(-8 * D_MODEL // (3 * 256)) * 256

LANES = 128
SUBLANES = 8
ROW_BLK = 128
SAMPLE_ROWS_PER_SEQ = SUBLANES
SAMPLE_BLKS = DEC_BATCH * SAMPLE_ROWS_PER_SEQ // ROW_BLK
MAIN_BLK0 = SAMPLE_BLKS
MAIN_BLKS_PER_B = SEQ // ROW_BLK
META_BLK0 = MAIN_BLK0 + BATCH * MAIN_BLKS_PER_B
N_BLKS = META_BLK0 + BATCH
T_ROWS = N_BLKS * ROW_BLK
PROMPT_CHUNKS = 1 + MAIN_BLKS_PER_B
META_PAD = ROW_BLK - N_META

MM_TM = 512
MM_TN = 512
MM_TN_MID = 1024
MM_TN_WIDE = 2048
VMEM_LIMIT = 48 * 2 ** 20


def _prompt_blk(b, c):
    return jnp.where(c == 0, META_BLK0 + b, MAIN_BLK0 + MAIN_BLKS_PER_B * b + c - 1)


def _cparams(*sem):
    return pltpu.CompilerParams(dimension_semantics=sem, vmem_limit_bytes=VMEM_LIMIT)


def _silu(x):
    return x * jax.nn.sigmoid(x)


def _softplus(x):
    return jnp.maximum(x, 0.0) + jnp.log(1.0 + jnp.exp(-jnp.abs(x)))


def _rms(x, w):
    return x * lax.rsqrt(jnp.mean(x * x, axis=-1, keepdims=True) + NORM_EPS) * w


def _dot(a, b):
    return jnp.dot(a, b, preferred_element_type=F32)


def _dot_nt(a, b):
    return lax.dot_general(a, b, (((1,), (1,)), ((), ())), preferred_element_type=F32)


def _dot_tn(a, b):
    return lax.dot_general(a, b, (((0,), (0,)), ((), ())), preferred_element_type=F32)


def _split_bf16(x):
    hi = x.astype(BF16)
    return hi, (x - hi.astype(F32)).astype(BF16)


def _drop_ref(kernel_fn, idx):
    def wrapped(*refs):
        return kernel_fn(*refs[:idx], *refs[idx + 1:])
    return wrapped


def _mm_kernel(*refs, normalize, glu, has_res):
    it = iter(refs)
    x_ref = next(it)
    nw_ref = next(it) if normalize else None
    w_ref = next(it)
    w2_ref = next(it) if glu else None
    res_ref = next(it) if has_res else None
    o_ref = next(it)
    xn_ref = next(it) if normalize else None
    if normalize:
        @pl.when(pl.program_id(1) == 0)
        def _():
            xn_ref[...] = _rms(x_ref[...], nw_ref[...]).astype(BF16)
        xb = xn_ref[...]
    else:
        xb = x_ref[...].astype(BF16)
    acc = _dot(xb, w_ref[...])
    if glu:
        acc = acc * jax.nn.sigmoid(_dot(xb, w2_ref[...]))
    if has_res:
        acc = acc + res_ref[...]
    o_ref[...] = acc.astype(o_ref.dtype)


def _matmul(x, w, n_out, *, norm_w=None, res=None, glu_off=None, out_dtype=F32, tn=MM_TN):
    m, k = x.shape
    normalize = norm_w is not None
    glu = glu_off is not None
    has_res = res is not None
    in_specs = [pl.BlockSpec((MM_TM, k), lambda i, j: (i, 0))]
    args = [x]
    if normalize:
        in_specs.append(pl.BlockSpec((1, k), lambda i, j: (0, 0)))
        args.append(norm_w.reshape(1, k))
    in_specs.append(pl.BlockSpec((k, tn), lambda i, j: (0, j)))
    args.append(w)
    if glu:
        in_specs.append(pl.BlockSpec((k, tn), lambda i, j: (0, j + glu_off)))
        args.append(w)
    if has_res:
        in_specs.append(pl.BlockSpec((MM_TM, tn), lambda i, j: (i, j)))
        args.append(res)
    return pl.pallas_call(
        functools.partial(_mm_kernel, normalize=normalize, glu=glu, has_res=has_res),
        grid=(m // MM_TM, n_out // tn),
        in_specs=in_specs,
        out_specs=pl.BlockSpec((MM_TM, tn), lambda i, j: (i, j)),
        out_shape=jax.ShapeDtypeStruct((m, n_out), out_dtype),
        scratch_shapes=[pltpu.VMEM((MM_TM, k), BF16)] if normalize else [],
        compiler_params=_cparams("parallel", "arbitrary"),
        name="matmul_n%d%s%s%s" % (n_out, "_norm" * normalize, "_glu" * glu, "_res" * has_res),
    )(*args)


def _rmsnorm_kernel(x_ref, w_ref, o_ref):
    o_ref[...] = _rms(x_ref[...], w_ref[...])


def _rmsnorm(x, w):
    m, k = x.shape
    return pl.pallas_call(
        _rmsnorm_kernel,
        grid=(m // MM_TM,),
        in_specs=[pl.BlockSpec((MM_TM, k), lambda i: (i, 0)), pl.BlockSpec((1, k), lambda i: (0, 0))],
        out_specs=pl.BlockSpec((MM_TM, k), lambda i: (i, 0)),
        out_shape=jax.ShapeDtypeStruct((m, k), F32),
        compiler_params=_cparams("parallel"),
        name="rmsnorm",
    )(x, w.reshape(1, k))


FFN_TF = 512
FFN_NF = FFN_HIDDEN // FFN_TF


def _ffn_kernel(x_ref, nw_ref, wg_ref, wu_ref, wd_ref, o_ref, xn_ref):
    f = pl.program_id(1)

    @pl.when(f == 0)
    def _():
        x = x_ref[...]
        xn_ref[...] = _rms(x, nw_ref[...]).astype(BF16)
        o_ref[...] = x

    xb = xn_ref[...]
    act = _silu(_dot(xb, wg_ref[...])) * _dot(xb, wu_ref[...])
    o_ref[...] += _dot(act.astype(BF16), wd_ref[...])


def _ffn(h, norm_w, w_gu, w_down):
    m = h.shape[0]
    return pl.pallas_call(
        _ffn_kernel,
        grid=(m // MM_TM, FFN_NF),
        in_specs=[
            pl.BlockSpec((MM_TM, D_MODEL), lambda i, f: (i, 0)),
            pl.BlockSpec((1, D_MODEL), lambda i, f: (0, 0)),
            pl.BlockSpec((D_MODEL, FFN_TF), lambda i, f: (0, f)),
            pl.BlockSpec((D_MODEL, FFN_TF), lambda i, f: (0, f + FFN_NF)),
            pl.BlockSpec((FFN_TF, D_MODEL), lambda i, f: (f, 0)),
        ],
        out_specs=pl.BlockSpec((MM_TM, D_MODEL), lambda i, f: (i, 0)),
        out_shape=jax.ShapeDtypeStruct((m, D_MODEL), F32),
        scratch_shapes=[pltpu.VMEM((MM_TM, D_MODEL), BF16)],
        compiler_params=_cparams("parallel", "arbitrary"),
        name="ffn",
    )(h, norm_w.reshape(1, D_MODEL), w_gu, w_gu, w_down)


def _dt_kernel(x_ref, nw_ref, wt_ref, bias_ref, o_ref):
    xb = _rms(x_ref[...], nw_ref[...]).astype(BF16)
    o_ref[...] = _softplus(_dot_nt(wt_ref[...], xb) + bias_ref[...])


def _ssd_dt(h, norm_w, w_dt_t, dt_bias):
    m = h.shape[0]
    return pl.pallas_call(
        _dt_kernel,
        grid=(m // MM_TM,),
        in_specs=[
            pl.BlockSpec((MM_TM, D_MODEL), lambda i: (i, 0)),
            pl.BlockSpec((1, D_MODEL), lambda i: (0, 0)),
            pl.BlockSpec((SSD_HEADS, D_MODEL), lambda i: (0, 0)),
            pl.BlockSpec((SSD_HEADS, 1), lambda i: (0, 0)),
        ],
        out_specs=pl.BlockSpec((SSD_HEADS, MM_TM), lambda i: (0, i)),
        out_shape=jax.ShapeDtypeStruct((SSD_HEADS, m), F32),
        compiler_params=_cparams("parallel"),
        name="ssd_dt",
    )(h, norm_w.reshape(1, D_MODEL), w_dt_t, dt_bias.reshape(SSD_HEADS, 1))


SSD_Q = ROW_BLK
SSD_SAMPLE_SEQS = 4


def _ssd_kernel(z_ref, x_ref, b_ref, c_ref, valid_ref, validt_ref, dtt_ref, alog_ref, dskip_ref,
                cwx_ref, cwb_ref, cwc_ref, cbx_ref, cbb_ref, cbc_ref, cpx_ref, cpb_ref, cpc_ref,
                nw_ref, h0_ref, y_ref, hl_ref, st_ref, xpx_ref, xpb_ref, xpc_ref, *, rows, nch, nsb):
    c = pl.program_id(2)
    q = SSD_Q
    chained = nch > 1

    if chained:
        @pl.when(c == 0)
        def _():
            st_ref[...] = h0_ref[0]
            xpx_ref[0, 0:SUBLANES] = cpx_ref[0]
            xpb_ref[0, 0:SUBLANES] = cpb_ref[0]
            xpc_ref[0, 0:SUBLANES] = cpc_ref[0]

    def pad_rows(x):
        if rows == q:
            return x
        return jnp.concatenate([x, jnp.zeros((q - rows, x.shape[1]), x.dtype)], axis=0)

    def conv(si, raw, xp_ref, cp_ref, w_ref, bias_ref):
        if not chained:
            xp_ref[si, 0:SUBLANES] = cp_ref[si]
        xp_ref[si, SUBLANES:SUBLANES + rows] = raw
        acc = bias_ref[...]
        for k in range(SSD_CONV_K):
            acc = acc + w_ref[k:k + 1, :] * xp_ref[si, pl.ds(SUBLANES - SSD_CONV_K + 1 + k, rows), :]
        if chained:
            xp_ref[si, 0:SUBLANES] = raw[rows - SUBLANES:rows]
        return _silu(acc)

    a_col = -jnp.exp(alog_ref[0])
    lane = lax.broadcasted_iota(jnp.int32, (SSD_HPG, q), 1)
    row_i = lax.broadcasted_iota(jnp.int32, (rows, q), 0)
    col_i = lax.broadcasted_iota(jnp.int32, (rows, q), 1)
    eye = row_i == col_i
    causal = row_i >= col_i
    head_a = lax.broadcasted_iota(jnp.int32, (1, LANES), 1) < SSD_HEAD_DIM
    row_a = lax.broadcasted_iota(jnp.int32, (2 * SSD_HEAD_DIM, 1), 0) < SSD_HEAD_DIM
    dskip = dskip_ref[0]

    def to_col(rowvec):
        return jnp.sum(jnp.where(eye, rowvec, 0.0), axis=1, keepdims=True)

    def pair_stages(si, p, xs, bm_pad, cm, cb, cum, dtt, ys):
        xh = xs[:, LANES * p:LANES * (p + 1)]
        xhb = pad_rows(xh).astype(BF16)
        if chained:
            st = st_ref[2 * p:2 * p + 2]
        else:
            st = h0_ref[si, 2 * p:2 * p + 2]
        st = st.reshape(2 * SSD_HEAD_DIM, SSD_STATE)
        yo = _dot_nt(cm, st.astype(BF16))
        yd, e_col, w_col, e_last = [], [], [], []
        for r in (2 * p, 2 * p + 1):
            cum_row = cum[r:r + 1, :]
            dt_row = dtt[r:r + 1, :]
            cum_col = to_col(cum_row)
            dt_col = to_col(dt_row)
            lmat = jnp.exp(jnp.where(causal, cum_col - cum_row, -jnp.inf))
            yd.append(_dot((cb * lmat * dt_row).astype(BF16), xhb))
            c_last = cum_row[:, q - 1:q]
            e_col.append(jnp.exp(cum_col))
            w_col.append(jnp.exp(c_last - cum_col) * dt_col)
            e_last.append(jnp.exp(c_last))
        yield
        d2 = jnp.where(head_a, dskip[:, 2 * p:2 * p + 1], dskip[:, 2 * p + 1:2 * p + 2])
        ys[p] = (jnp.where(head_a, yd[0], yd[1]) + yo * jnp.where(head_a, e_col[0], e_col[1]) + xh * d2)
        xw = pad_rows(xh * jnp.where(head_a, w_col[0], w_col[1])).astype(BF16)
        st_new = st * jnp.where(row_a, e_last[0], e_last[1]) + _dot_tn(xw, bm_pad)
        yield
        st_new = st_new.reshape(2, SSD_HEAD_DIM, SSD_STATE)
        if chained:
            st_ref[2 * p:2 * p + 2] = st_new
        else:
            hl_ref[si, 2 * p:2 * p + 2] = st_new

    live, outs = [], []
    for si in range(nsb):
        rsl = slice(si * rows, (si + 1) * rows)
        valid = valid_ref[rsl]
        xs = conv(si, x_ref[rsl] * valid, xpx_ref, cpx_ref, cwx_ref, cbx_ref)
        bm = conv(si, b_ref[rsl] * valid, xpb_ref, cpb_ref, cwb_ref, cbb_ref)
        cm = conv(si, c_ref[rsl] * valid, xpc_ref, cpc_ref, cwc_ref, cbc_ref).astype(BF16)
        bm_pad = pad_rows(bm).astype(BF16)
        dtt = dtt_ref[si, 0] * validt_ref[si]
        cum = dtt * a_col
        s = 1
        while s < q:
            cum = cum + jnp.where(lane >= s, pltpu.roll(cum, s, axis=1), 0.0)
            s *= 2
        cb = _dot_nt(cm, bm_pad)
        ys = [None] * (SSD_HPG // 2)
        outs.append((rsl, ys))
        live += [pair_stages(si, p, xs, bm_pad, cm, cb, cum, dtt, ys) for p in range(SSD_HPG // 2)]
    while live:
        live = [gen for gen in live if next(gen, True) is None]
    y = jnp.concatenate([jnp.concatenate(ys, axis=1) for _, ys in outs], axis=0)
    y = y * _silu(z_ref[...])
    y_ref[...] = _rms(y, nw_ref[...]).astype(y_ref.dtype)

    if chained:
        @pl.when(c == nch - 1)
        def _():
            hl_ref[0] = st_ref[...]


def _ssd_core(proj, valid, validt, dtt, a_log, d_skip, conv_w, conv_b, conv_prev8, norm_w, h0, *,
              prompt, y_init=None, h0_layer=0):
    nseq = h0.shape[1]
    if prompt:
        rows, nch, nsb = ROW_BLK, PROMPT_CHUNKS, 1
        blk = _prompt_blk
    else:
        rows, nch, nsb = SAMPLE_ROWS_PER_SEQ, 1, SSD_SAMPLE_SEQS
        blk = lambda b, c: b
    gw = SSD_GROUP_W
    x_off = SSD_D_INNER // gw
    b_off = (2 * SSD_D_INNER) // SSD_STATE
    c_off = b_off + SSD_GROUPS
    cb_off = SSD_D_INNER // SSD_STATE
    cc_off = cb_off + SSD_GROUPS
    row_spec = lambda w, off: pl.BlockSpec((nsb * rows, w), lambda b, g, c: (blk(b, c), g + off))
    par_spec = lambda r, w, off: pl.BlockSpec((r, w), lambda b, g, c: (0, g + off))
    prev_spec = lambda w, off: pl.BlockSpec((nsb, SUBLANES, w), lambda b, g, c: (b, 0, g + off))
    st_spec = pl.BlockSpec((nsb, SSD_HPG, SSD_HEAD_DIM, SSD_STATE), lambda b, g, c: (b, g, 0, 0))
    in_specs = [
        row_spec(gw, 0), row_spec(gw, x_off), row_spec(SSD_STATE, b_off), row_spec(SSD_STATE, c_off),
        pl.BlockSpec((nsb * rows, 1), lambda b, g, c: (blk(b, c), 0)),
        pl.BlockSpec((nsb, 1, SSD_Q), lambda b, g, c: (blk(b, c), 0, 0)),
        pl.BlockSpec((nsb, 1, SSD_HPG, SSD_Q), lambda b, g, c: (blk(b, c), g, 0, 0)),
        pl.BlockSpec((1, SSD_HPG, 1), lambda b, g, c: (g, 0, 0)),
        pl.BlockSpec((1, 1, SSD_HPG), lambda b, g, c: (g, 0, 0)),
        par_spec(SSD_CONV_K, gw, 0), par_spec(SSD_CONV_K, SSD_STATE, cb_off),
        par_spec(SSD_CONV_K, SSD_STATE, cc_off),
        par_spec(1, gw, 0), par_spec(1, SSD_STATE, cb_off), par_spec(1, SSD_STATE, cc_off),
        prev_spec(gw, 0), prev_spec(SSD_STATE, cb_off), prev_spec(SSD_STATE, cc_off),
        par_spec(1, gw, 0),
        pl.BlockSpec((None, nsb, SSD_HPG, SSD_HEAD_DIM, SSD_STATE),
                     lambda b, g, c: (h0_layer, b, g, 0, 0)),
    ]
    args = [proj, proj, proj, proj, valid, validt, dtt,
            a_log.reshape(SSD_GROUPS, SSD_HPG, 1), d_skip.reshape(SSD_GROUPS, 1, SSD_HPG),
            conv_w, conv_w, conv_w, conv_b, conv_b, conv_b, conv_prev8, conv_prev8, conv_prev8,
            norm_w, h0]
    kernel_fn = functools.partial(_ssd_kernel, rows=rows, nch=nch, nsb=nsb)
    aliases = {}
    if y_init is not None:
        in_specs.append(pl.BlockSpec(memory_space=pl.ANY))
        args.append(y_init)
        aliases = {len(args) - 1: 0}
        kernel_fn = _drop_ref(kernel_fn, len(args) - 1)
    return pl.pallas_call(
        kernel_fn,
        grid=(nseq // nsb, SSD_GROUPS, nch),
        in_specs=in_specs,
        out_specs=[row_spec(gw, 0), st_spec],
        input_output_aliases=aliases,
        out_shape=[jax.ShapeDtypeStruct((T_ROWS, SSD_D_INNER), BF16),
                   jax.ShapeDtypeStruct(h0.shape[1:], F32)],
        scratch_shapes=[
            pltpu.VMEM((SSD_HPG, SSD_HEAD_DIM, SSD_STATE), F32),
            pltpu.VMEM((nsb, SUBLANES + rows, gw), F32),
            pltpu.VMEM((nsb, SUBLANES + rows, SSD_STATE), F32),
            pltpu.VMEM((nsb, SUBLANES + rows, SSD_STATE), F32),
        ],
        compiler_params=_cparams("parallel", "parallel", "arbitrary"),
        name="ssd_core_prompt" if prompt else "ssd_core_sample",
    )(*args)


def _s5_abar(lam_re, lam_im, log_dt):
    dt = jnp.exp(log_dt)
    mag = jnp.exp(lam_re * dt)
    ang = lam_im * dt
    return mag * jnp.cos(ang), mag * jnp.sin(ang)


def _s5_pow_kernel(lr_ref, li_ref, ldt_ref, pr_ref, pi_ref):
    ar, ai = _s5_abar(lr_ref[...], li_ref[...], ldt_ref[...])
    row = lax.broadcasted_iota(jnp.int32, (SUBLANES, S5_HDIM), 0)
    pr, pi = ar, ai
    out_r = jnp.broadcast_to(ar, (SUBLANES, S5_HDIM))
    out_i = jnp.broadcast_to(ai, (SUBLANES, S5_HDIM))
    for k in range(1, SUBLANES):
        pr, pi = pr * ar - pi * ai, pr * ai + pi * ar
        out_r = jnp.where(row == k, pr, out_r)
        out_i = jnp.where(row == k, pi, out_i)
    pr_ref[...] = out_r
    pi_ref[...] = out_i


def _s5_bbar_kernel(lr_ref, li_ref, ldt_ref, br_ref, bi_ref, or_ref, oi_ref):
    lr, li = lr_ref[...], li_ref[...]
    ar, ai = _s5_abar(lr, li, ldt_ref[...])
    den = lr * lr + li * li
    f_re = ((ar - 1.0) * lr + ai * li) / den
    f_im = (ai * lr - (ar - 1.0) * li) / den
    br, bi = br_ref[...], bi_ref[...]
    or_ref[...] = f_re * br - f_im * bi
    oi_ref[...] = f_re * bi + f_im * br


def _s5_params(lam_re, lam_im, log_dt, b_re, b_im):
    ldt = jnp.repeat(log_dt, S5_STATE)
    row = lambda a: a.reshape(1, S5_HDIM)
    col = lambda a: a.reshape(S5_HDIM, 1)
    full = lambda shape: pl.BlockSpec(shape, lambda: (0,) * len(shape))
    pow_re, pow_im = pl.pallas_call(
        _s5_pow_kernel,
        in_specs=[full((1, S5_HDIM))] * 3,
        out_specs=[full((SUBLANES, S5_HDIM))] * 2,
        out_shape=[jax.ShapeDtypeStruct((SUBLANES, S5_HDIM), F32)] * 2,
    )(row(lam_re), row(lam_im), row(ldt))
    bb_re, bb_im = pl.pallas_call(
        _s5_bbar_kernel,
        in_specs=[full((S5_HDIM, 1))] * 3 + [full((S5_HDIM, S5_GROUP_SIZE))] * 2,
        out_specs=[full((S5_HDIM, S5_GROUP_SIZE))] * 2,
        out_shape=[jax.ShapeDtypeStruct((S5_HDIM, S5_GROUP_SIZE), F32)] * 2,
    )(col(lam_re), col(lam_im), col(ldt), b_re.reshape(S5_HDIM, S5_GROUP_SIZE),
      b_im.reshape(S5_HDIM, S5_GROUP_SIZE))
    return pow_re, pow_im, bb_re, bb_im


def _s5_block_diag(bb, c):
    gpb = S5_SBLK // S5_STATE
    eye = jnp.eye(gpb, dtype=F32)
    bb = bb.reshape(S5_NSB, gpb, S5_STATE, S5_GROUP_SIZE)
    w_in = eye[None, :, None, :, None] * jnp.transpose(bb, (0, 3, 1, 2))[:, None]
    w_in = w_in.reshape(S5_NSB, S5_VBLK, S5_SBLK)
    c = c.reshape(S5_NSB, gpb, S5_GROUP_SIZE, S5_STATE)
    w_out = eye[None, :, None, :, None] * jnp.transpose(c, (0, 1, 3, 2))[:, :, :, None, :]
    w_out = w_out.reshape(S5_NSB, S5_SBLK, S5_VBLK)
    return w_in.astype(BF16), w_out.astype(BF16)


def _s5_kernel(v_ref, valid_ref, wir_ref, wii_ref, wor_ref, woi_ref, pr_ref, pi_ref, d_ref,
               h0r_ref, h0i_ref, y_ref, hlr_ref, hli_ref, cr_ref, ci_ref, hr_ref, hi_ref, *,
               per_tile, nch, last_row):
    ntiles = ROW_BLK // SUBLANES
    if not per_tile:
        c = pl.program_id(2)

        @pl.when(c == 0)
        def _():
            cr_ref[...] = h0r_ref[0]
            ci_ref[...] = h0i_ref[0]

    v = v_ref[...]
    vb = (v * valid_ref[...]).astype(BF16)
    bu_r = _dot(vb, wir_ref[0])
    bu_i = _dot(vb, wii_ref[0])
    pr, pi = pr_ref[...], pi_ref[...]
    row = lax.broadcasted_iota(jnp.int32, (SUBLANES, S5_SBLK), 0)
    levels = []
    for s in (1, 2, 4):
        levels.append((s, jnp.where(row >= s, pr[s - 1:s], 0.0), jnp.where(row >= s, pi[s - 1:s], 0.0)))
    if not per_tile:
        car_r, car_i = cr_ref[...], ci_ref[...]
    for i in range(ntiles):
        xr = bu_r[SUBLANES * i:SUBLANES * (i + 1)]
        xi = bu_i[SUBLANES * i:SUBLANES * (i + 1)]
        for s, ar, ai in levels:
            sr = pltpu.roll(xr, s, axis=0)
            si = pltpu.roll(xi, s, axis=0)
            xr, xi = xr + ar * sr - ai * si, xi + ar * si + ai * sr
        if per_tile:
            car_r, car_i = h0r_ref[i], h0i_ref[i]
        hr = xr + pr * car_r - pi * car_i
        hi = xi + pr * car_i + pi * car_r
        hr_ref[SUBLANES * i:SUBLANES * (i + 1)] = hr
        hi_ref[SUBLANES * i:SUBLANES * (i + 1)] = hi
        if per_tile:
            hlr_ref[i] = hr[last_row:last_row + 1]
            hli_ref[i] = hi[last_row:last_row + 1]
        else:
            car_r, car_i = hr[last_row:last_row + 1], hi[last_row:last_row + 1]
    y = (_dot(hr_ref[...].astype(BF16), wor_ref[0]) - _dot(hi_ref[...].astype(BF16), woi_ref[0])
         + d_ref[...] * v)
    y_ref[...] = jax.nn.gelu(y).astype(y_ref.dtype)
    if not per_tile:
        cr_ref[...] = car_r
        ci_ref[...] = car_i

        @pl.when(c == nch - 1)
        def _():
            hlr_ref[0] = car_r
            hli_ref[0] = car_i


def _s5_core(v, valid, w_in_re, w_in_im, w_out_re, w_out_im, pow_re, pow_im, d_skip, h0_re, h0_im, *,
             prompt, y_init=None):
    nseq = h0_re.shape[0]
    if prompt:
        grid = (nseq, S5_NSB, PROMPT_CHUNKS)
        blk = _prompt_blk
        sidx = lambda b, s, c: (b, 0, s)
        spb = 1
        kw = dict(per_tile=False, nch=PROMPT_CHUNKS, last_row=SUBLANES - 1)
        sem = ("parallel", "parallel", "arbitrary")
    else:
        grid = (SAMPLE_BLKS, S5_NSB, 1)
        blk = lambda b, c: b
        sidx = lambda b, s, c: (b, 0, s)
        spb = ROW_BLK // SAMPLE_ROWS_PER_SEQ
        kw = dict(per_tile=True, nch=1, last_row=DEC_SEQ - 1)
        sem = ("parallel", "parallel", "arbitrary")
    row_spec = pl.BlockSpec((ROW_BLK, S5_VBLK), lambda b, s, c: (blk(b, c), s))
    st_spec = pl.BlockSpec((spb, 1, S5_SBLK), sidx)
    w_in_spec = pl.BlockSpec((1, S5_VBLK, S5_SBLK), lambda b, s, c: (s, 0, 0))
    w_out_spec = pl.BlockSpec((1, S5_SBLK, S5_VBLK), lambda b, s, c: (s, 0, 0))
    pow_spec = pl.BlockSpec((SUBLANES, S5_SBLK), lambda b, s, c: (0, s))
    in_specs = [row_spec, pl.BlockSpec((ROW_BLK, 1), lambda b, s, c: (blk(b, c), 0)),
                w_in_spec, w_in_spec, w_out_spec, w_out_spec, pow_spec, pow_spec,
                pl.BlockSpec((1, S5_VBLK), lambda b, s, c: (0, s)), st_spec, st_spec]
    args = [v, valid, w_in_re, w_in_im, w_out_re, w_out_im, pow_re, pow_im,
            d_skip.reshape(1, D_MODEL), h0_re, h0_im]
    kernel_fn = functools.partial(_s5_kernel, **kw)
    aliases = {}
    if y_init is not None:
        in_specs.append(pl.BlockSpec(memory_space=pl.ANY))
        args.append(y_init)
        aliases = {len(args) - 1: 0}
        kernel_fn = _drop_ref(kernel_fn, len(args) - 1)
    return pl.pallas_call(
        kernel_fn,
        grid=grid,
        in_specs=in_specs,
        out_specs=[row_spec, st_spec, st_spec],
        input_output_aliases=aliases,
        out_shape=[jax.ShapeDtypeStruct((T_ROWS, D_MODEL), BF16),
                   jax.ShapeDtypeStruct(h0_re.shape, F32), jax.ShapeDtypeStruct(h0_im.shape, F32)],
        scratch_shapes=[pltpu.VMEM((1, S5_SBLK), F32), pltpu.VMEM((1, S5_SBLK), F32),
                        pltpu.VMEM((ROW_BLK, S5_SBLK), F32), pltpu.VMEM((ROW_BLK, S5_SBLK), F32)],
        compiler_params=_cparams(*sem),
        name="s5_core_prompt" if prompt else "s5_core_sample",
    )(*args)


def _rwkv_proj_kernel(u_ref, p_ref, mu_ref, wr_ref, wk_ref, wv_ref, r_ref, k_ref, v_ref,
                      xr_ref, xk_ref, xv_ref):
    @pl.when(pl.program_id(1) == 0)
    def _():
        u = u_ref[...]
        d = p_ref[...] - u
        xr_ref[...] = (u + d * mu_ref[0:1, :]).astype(BF16)
        xk_ref[...] = (u + d * mu_ref[2:3, :]).astype(BF16)
        xv_ref[...] = (u + d * mu_ref[3:4, :]).astype(BF16)

    r_ref[...] = _dot(xr_ref[...], wr_ref[...])
    k_ref[...] = _dot(xk_ref[...], wk_ref[...])
    v_ref[...] = _dot(xv_ref[...], wv_ref[...])


def _rwkv_proj(u, prev, mu, wr, wk, wv):
    m = u.shape[0]
    row = pl.BlockSpec((MM_TM, D_MODEL), lambda i, j: (i, 0))
    wsp = pl.BlockSpec((D_MODEL, MM_TN), lambda i, j: (0, j))
    osp = pl.BlockSpec((MM_TM, MM_TN), lambda i, j: (i, j))
    return pl.pallas_call(
        _rwkv_proj_kernel,
        grid=(m // MM_TM, D_MODEL // MM_TN),
        in_specs=[row, row, pl.BlockSpec((6, D_MODEL), lambda i, j: (0, 0)), wsp, wsp, wsp],
        out_specs=[osp, osp, osp],
        out_shape=[jax.ShapeDtypeStruct((m, D_MODEL), F32)] * 3,
        scratch_shapes=[pltpu.VMEM((MM_TM, D_MODEL), BF16)] * 3,
        compiler_params=_cparams("parallel", "arbitrary"),
        name="rwkv_proj",
    )(u, prev, mu, wr, wk, wv)


RWKV_LORA_TM = 256


def _rwkv_lora_kernel(u_ref, p_ref, mu_ref, w1_ref, w2_ref, a1_ref, a2_ref, g1_ref, g2_ref,
                      w0_ref, a0_ref, lw_ref, a_ref, g_ref):
    u = u_ref[...]
    d = p_ref[...] - u
    xw = (u + d * mu_ref[1:2, :]).astype(BF16)
    xa = (u + d * mu_ref[4:5, :]).astype(BF16)
    xg = (u + d * mu_ref[5:6, :]).astype(BF16)
    wpre = w0_ref[...] + _dot(jnp.tanh(_dot(xw, w1_ref[...])).astype(BF16), w2_ref[...])
    w = -_softplus(-wpre) - 0.5
    lw_ref[...] = -jnp.exp(w)
    a_ref[...] = jax.nn.sigmoid(a0_ref[...] + _dot(_dot(xa, a1_ref[...]).astype(BF16), a2_ref[...]))
    g_ref[...] = _dot(jax.nn.sigmoid(_dot(xg, g1_ref[...])).astype(BF16), g2_ref[...])


def _rwkv_lora(u, prev, mu, w1, w2, a1, a2, g1, g2, w0, a0):
    m = u.shape[0]
    row = pl.BlockSpec((RWKV_LORA_TM, D_MODEL), lambda i: (i, 0))
    full = lambda a: pl.BlockSpec(a.shape, lambda i: (0, 0))
    vec = pl.BlockSpec((1, D_MODEL), lambda i: (0, 0))
    return pl.pallas_call(
        _rwkv_lora_kernel,
        grid=(m // RWKV_LORA_TM,),
        in_specs=[row, row, pl.BlockSpec((6, D_MODEL), lambda i: (0, 0)),
                  full(w1), full(w2), full(a1), full(a2), full(g1), full(g2), vec, vec],
        out_specs=[row, row, row],
        out_shape=[jax.ShapeDtypeStruct((m, D_MODEL), F32)] * 3,
        compiler_params=_cparams("parallel"),
        name="rwkv_lora",
    )(u, prev, mu, w1, w2, a1, a2, g1, g2, w0.reshape(1, D_MODEL), a0.reshape(1, D_MODEL))


def _block_ones():
    r = lax.broadcasted_iota(jnp.int32, (LANES, LANES), 0) // RWKV_HEAD_DIM
    c = lax.broadcasted_iota(jnp.int32, (LANES, LANES), 1) // RWKV_HEAD_DIM
    return (r == c).astype(BF16)


def _head_sum(x, bo):
    hi, lo = _split_bf16(x)
    return _dot(hi, bo) + _dot(lo, bo)


RWKV_PG = 8
RWKV_NPG = RWKV_PAIRS // RWKV_PG
RWKV_PG_W = RWKV_PG * LANES


class _RwkvConsts:
    def __init__(self):
        self.bo = _block_ones()
        lane = lax.broadcasted_iota(jnp.int32, (RWKV_HEAD_DIM, LANES), 1)
        row = lax.broadcasted_iota(jnp.int32, (RWKV_HEAD_DIM, LANES), 0)
        self.diag = ((lane % RWKV_HEAD_DIM) == row).astype(BF16)
        self.head_a = lane < RWKV_HEAD_DIM
        self.row8 = lax.broadcasted_iota(jnp.int32, (SUBLANES, LANES), 0)


def _rwkv_step(s2, s2b, kk_t, dec_t, kka_t, k_t, v_t, r_lhs, cst):
    bo = cst.bo
    sa = _dot_nt(s2b, bo * (-kk_t).astype(BF16))
    v_hi, v_lo = _split_bf16(v_t)
    v2 = _dot_nt(cst.diag, bo * v_hi) + _dot_nt(cst.diag, bo * v_lo)
    s2n = s2 * dec_t + sa * kka_t + v2 * k_t
    s2nb = s2n.astype(BF16)
    zero = jnp.zeros_like(s2nb)
    wy = jnp.concatenate([jnp.where(cst.head_a, s2nb, zero), jnp.where(cst.head_a, zero, s2nb)], axis=0)
    return s2n, s2nb, _dot_nt(r_lhs, wy)


def _rwkv_steps(s2, tiles, nsteps, cst, store):
    kk8, dec8, kka8, k8, v8, r8 = tiles
    s2b = s2.astype(BF16)
    y = jnp.zeros((SUBLANES, LANES), F32)
    for i in range(nsteps):
        r_lhs = jnp.where(cst.row8 == i, r8, 0.0).astype(BF16)
        s2, s2b, yi = _rwkv_step(s2, s2b, kk8[i:i + 1], dec8[i:i + 1], kka8[i:i + 1], k8[i:i + 1],
                                 v8[i:i + 1], r_lhs, cst)
        y = y + yi
        yield
    store(s2, y)


def _rwkv_prepare(k_ref, a_ref, v_ref, valid, kk_p, ka_p, kk_s, kka_s, km_s, vm_s, bo):
    for p in range(RWKV_PG):
        sl = slice(LANES * p, LANES * (p + 1))
        k, a = k_ref[:, sl], a_ref[:, sl]
        kkr = k * kk_p[:, sl]
        kk = kkr / jnp.maximum(jnp.sqrt(_head_sum(kkr * kkr, bo)), 1e-12)
        kk_s[:, sl] = kk
        kka_s[:, sl] = kk * a
        km_s[:, sl] = k * (1.0 + (a - 1.0) * ka_p[:, sl])
        vm_s[:, sl] = v_ref[:, sl] * valid


def _rwkv_finish(yraw_s, r_ref, km_s, vm_s, g_ref, rk_p, lnw_p, lnb_p, o_ref, bo):
    inv = 1.0 / RWKV_HEAD_DIM
    for p in range(RWKV_PG):
        sl = slice(LANES * p, LANES * (p + 1))
        y = yraw_s[:, sl]
        d = y - _head_sum(y, bo) * inv
        yn = d * lax.rsqrt(_head_sum(d * d, bo) * inv + RWKV_LN_EPS) * lnw_p[:, sl] + lnb_p[:, sl]
        bonus = _head_sum(r_ref[:, sl] * km_s[:, sl] * rk_p[:, sl], bo) * vm_s[:, sl]
        o_ref[:, sl] = ((yn + bonus) * g_ref[:, sl]).astype(o_ref.dtype)


def _rwkv_tiles(base, p, refs):
    return [ref[pl.ds(base, SUBLANES), pl.ds(LANES * p, LANES)] for ref in refs]


RWKV_C = 64
RWKV_PROMPT_CHUNKS = (ROW_BLK + SEQ) // RWKV_C
RWKV_INV_LEVELS = 6


def _prompt_blk64(b, c):
    per = ROW_BLK // RWKV_C
    return jnp.where(c < per, per * (META_BLK0 + b) + c,
                     per * (MAIN_BLK0 + MAIN_BLKS_PER_B * b) + c - per)


def _pair_bd(x, head_a):
    zero = jnp.zeros_like(x)
    return jnp.concatenate([jnp.where(head_a, x, zero), jnp.where(head_a, zero, x)], axis=0)


def _rwkv_chunk_kernel(r_ref, k_ref, v_ref, lw_ref, a_ref, g_ref, valid_ref, kk_p, ka_p, rk_p,
                       lnw_p, lnb_p, s0_ref, o_ref, sl_ref, s_ref):
    c = pl.program_id(2)
    cl = RWKV_C

    @pl.when(c == 0)
    def _():
        s_ref[...] = s0_ref[0]

    bo = _block_ones()
    lane = lax.broadcasted_iota(jnp.int32, (cl, LANES), 1)
    row = lax.broadcasted_iota(jnp.int32, (cl, LANES), 0)
    head_a = lane < RWKV_HEAD_DIM
    s_idx = lane % RWKV_HEAD_DIM
    strict = s_idx < row
    incl = s_idx <= row
    eye2 = (s_idx == row).astype(F32)
    tri = (lax.broadcasted_iota(jnp.int32, (cl, cl), 0)
           >= lax.broadcasted_iota(jnp.int32, (cl, cl), 1)).astype(BF16)
    bd = lambda x: _pair_bd(x, head_a)
    b16 = lambda x: x.astype(BF16)
    halves = lambda x: x[:cl] + x[cl:]
    valid = valid_ref[...]
    inv = 1.0 / RWKV_HEAD_DIM

    def pair_stages(p):
        sl = slice(LANES * p, LANES * (p + 1))
        k, a, r = k_ref[:, sl], a_ref[:, sl], r_ref[:, sl]
        vm = v_ref[:, sl] * valid
        lw = lw_ref[:, sl] * valid
        kkr = k * kk_p[:, sl]
        kk = kkr / jnp.maximum(jnp.sqrt(_head_sum(kkr * kkr, bo)), 1e-12) * valid
        bv = kk * a
        km = k * (1.0 + (a - 1.0) * ka_p[:, sl])
        lw_hi, lw_lo = _split_bf16(lw)
        g = _dot(tri, lw_hi) + _dot(tri, lw_lo)
        yield
        g_end = g[cl - 1:cl]
        e_neg = jnp.exp(-g)
        e_end = jnp.exp(g_end - g)
        at = -kk * jnp.exp(g - lw)
        rt = r * jnp.exp(g)
        x2 = b16(jnp.concatenate([at, rt], axis=0))
        gb = _dot_nt(x2, bd(b16(bv * e_neg)))
        gk = _dot_nt(x2, bd(b16(km * e_neg)))
        yield
        n = jnp.where(strict, gb[:cl], 0.0)
        mrb = jnp.where(incl, gb[cl:], 0.0)
        lak = jnp.where(strict, gk[:cl], 0.0)
        mrk = jnp.where(incl, gk[cl:], 0.0)
        pk, t = n, eye2
        for level in range(RWKV_INV_LEVELS):
            if level < RWKV_INV_LEVELS - 1:
                res = _dot(b16(jnp.concatenate([pk, t], axis=0)), bd(b16(pk)))
                pk, t = res[:cl], t + res[cl:]
            else:
                t = t + _dot(b16(t), bd(b16(pk)))
            yield
        n_hi, n_lo = _split_bf16(n)
        t_hi, t_lo = _split_bf16(t)
        nt = halves(_dot(jnp.concatenate([n_hi, n_lo], axis=0), bd(t_hi))) + _dot(n_hi, bd(t_lo))
        wy = _dot(b16(jnp.concatenate([lak, mrk], axis=0)), bd(b16(vm)))
        yield
        t = t + _dot(t_hi, bd(b16(eye2 - t + nt)))
        yield
        tt = jnp.concatenate(_split_bf16(t), axis=0)
        pmat = halves(_dot(tt, bd(b16(at))))
        q = halves(_dot(tt, bd(b16(wy[:cl]))))
        yield
        s2 = s_ref[p]
        p_hi, p_lo = _split_bf16(pmat)
        res = _dot_nt(jnp.concatenate([p_hi, p_lo, b16(rt)], axis=0), bd(b16(s2)))
        yield
        u = res[:cl] + res[cl:2 * cl] + q
        y = res[2 * cl:] + _dot(b16(mrb), bd(b16(u))) + wy[cl:]
        v_hi, v_lo = _split_bf16(vm)
        kg_hi, kg_lo = _split_bf16(km * e_end)
        full = _dot_tn(jnp.concatenate([b16(u), v_hi, v_lo, v_hi], axis=0),
                       jnp.concatenate([b16(bv * e_end), kg_hi, kg_hi, kg_lo], axis=0))
        yield
        s_ref[p] = s2 * jnp.exp(g_end) + jnp.where(head_a, full[:RWKV_HEAD_DIM], full[RWKV_HEAD_DIM:])
        d = y - _head_sum(y, bo) * inv
        yield
        yn = d * lax.rsqrt(_head_sum(d * d, bo) * inv + RWKV_LN_EPS) * lnw_p[:, sl] + lnb_p[:, sl]
        bonus = _head_sum(r * km * rk_p[:, sl], bo) * vm
        o_ref[:, sl] = ((yn + bonus) * g_ref[:, sl]).astype(o_ref.dtype)

    live = [pair_stages(p) for p in range(RWKV_PG)]
    while live:
        live = [gen for gen in live if next(gen, True) is None]

    @pl.when(c == RWKV_PROMPT_CHUNKS - 1)
    def _():
        sl_ref[0] = s_ref[...]


RWKV_SAMPLE_SEQS = 8
RWKV_SAMPLE_ROWS = RWKV_SAMPLE_SEQS * SAMPLE_ROWS_PER_SEQ


def _rwkv_sample_kernel(r_ref, k_ref, v_ref, lw_ref, a_ref, g_ref, valid_ref, kk_p, ka_p, rk_p,
                        lnw_p, lnb_p, s0_ref, o_ref, sl_ref, kk_s, kka_s, km_s, vm_s, yraw_s):
    cst = _RwkvConsts()
    _rwkv_prepare(k_ref, a_ref, v_ref, valid_ref[...], kk_p, ka_p, kk_s, kka_s, km_s, vm_s, cst.bo)

    def body(i, carry):
        base = pl.multiple_of(i * SAMPLE_ROWS_PER_SEQ, SAMPLE_ROWS_PER_SEQ)

        def make(p):
            def store(s2, y):
                sl_ref[i, p] = s2
                yraw_s[pl.ds(base, SUBLANES), pl.ds(LANES * p, LANES)] = y
            tiles = _rwkv_tiles(base, p, (kk_s, lw_ref, kka_s, km_s, vm_s, r_ref))
            tiles[1] = jnp.exp(tiles[1])
            return _rwkv_steps(s0_ref[i, p], tiles, DEC_SEQ, cst, store)

        live = [make(p) for p in range(RWKV_PG)]
        while live:
            live = [gen for gen in live if next(gen, True) is None]
        return carry

    lax.fori_loop(0, RWKV_SAMPLE_SEQS, body, 0)
    _rwkv_finish(yraw_s, r_ref, km_s, vm_s, g_ref, rk_p, lnw_p, lnb_p, o_ref, cst.bo)


def _rwkv_core(r, k, v, dec, a, g, valid, k_k, k_a, r_k, ln_w, ln_b, s0, *, prompt, y_init=None):
    nseq = s0.shape[0]
    vec = lambda a_: a_.reshape(1, D_MODEL)
    if prompt:
        rows = RWKV_C
        grid = (nseq, RWKV_NPG, RWKV_PROMPT_CHUNKS)
        rmap = lambda b, q, c: (_prompt_blk64(b, c), q)
        vmap = lambda b, q, c: (_prompt_blk64(b, c), 0)
        spb = 1
        kernel_fn = _rwkv_chunk_kernel
        scratch = [pltpu.VMEM((RWKV_PG, RWKV_HEAD_DIM, LANES), F32)]
    else:
        rows = RWKV_SAMPLE_ROWS
        grid = (nseq // RWKV_SAMPLE_SEQS, RWKV_NPG, 1)
        rmap = lambda b, q, c: (b, q)
        vmap = lambda b, q, c: (b, 0)
        spb = RWKV_SAMPLE_SEQS
        kernel_fn = _rwkv_sample_kernel
        scratch = [pltpu.VMEM((rows, RWKV_PG_W), F32)] * 5
    sem = ("parallel", "parallel", "arbitrary")
    row_spec = pl.BlockSpec((rows, RWKV_PG_W), rmap)
    vec_spec = pl.BlockSpec((1, RWKV_PG_W), lambda b, q, c: (0, q))
    st_spec = pl.BlockSpec((spb, RWKV_PG, RWKV_HEAD_DIM, LANES), lambda b, q, c: (b, q, 0, 0))
    in_specs = [row_spec] * 6 + [pl.BlockSpec((rows, 1), vmap)] + [vec_spec] * 5 + [st_spec]
    args = [r, k, v, dec, a, g, valid, vec(k_k), vec(k_a), vec(r_k), vec(ln_w), vec(ln_b), s0]
    aliases = {}
    if y_init is not None:
        in_specs.append(pl.BlockSpec(memory_space=pl.ANY))
        args.append(y_init)
        aliases = {len(args) - 1: 0}
        kernel_fn = _drop_ref(kernel_fn, len(args) - 1)
    return pl.pallas_call(
        kernel_fn,
        grid=grid,
        in_specs=in_specs,
        out_specs=[row_spec, st_spec],
        input_output_aliases=aliases,
        out_shape=[jax.ShapeDtypeStruct((T_ROWS, D_MODEL), BF16), jax.ShapeDtypeStruct(s0.shape, F32)],
        scratch_shapes=scratch,
        compiler_params=_cparams(*sem),
        name="rwkv_core_prompt" if prompt else "rwkv_core_sample",
    )(*args)


def _row_ids():
    sample_t0 = jnp.arange(DEC_BATCH) * SAMPLE_ROWS_PER_SEQ
    main0 = MAIN_BLK0 * ROW_BLK + jnp.arange(BATCH) * SEQ
    meta0 = META_BLK0 * ROW_BLK + jnp.arange(BATCH) * ROW_BLK + META_PAD
    return sample_t0, main0, meta0


def _valid_mask():
    r = jnp.arange(T_ROWS)
    sample = (r < MAIN_BLK0 * ROW_BLK) & (r % SAMPLE_ROWS_PER_SEQ < DEC_SEQ)
    main = (r >= MAIN_BLK0 * ROW_BLK) & (r < META_BLK0 * ROW_BLK)
    meta = (r >= META_BLK0 * ROW_BLK) & (r % ROW_BLK >= META_PAD)
    return (sample | main | meta).astype(F32)


def _mixer_out_init(width):
    return jnp.zeros((T_ROWS, width), BF16)


def _pad_conv_prev(prev):
    return jnp.pad(prev, ((0, 0), (SUBLANES - SSD_CONV_K + 1, 0), (0, 0)))


def _ssd_layer(h, valid, norm_w, w_in, conv_w, conv_b, dt_bias, a_log, d_skip, gnorm_w, w_out,
               state_all, layer, state_conv):
    sample_t0, main0, meta0 = _row_ids()
    w_in_b = w_in.astype(BF16)
    proj = _matmul(h, w_in_b, SSD_ZX_DIM, norm_w=norm_w, tn=MM_TN_WIDE)
    dtt = _ssd_dt(h, norm_w, w_in_b[:, SSD_ZX_DIM:].T, dt_bias)
    xbc = lambda rows: proj[rows][..., SSD_D_INNER:]
    tail = jnp.arange(SSD_CONV_K - 1)
    validt = valid.reshape(N_BLKS, 1, ROW_BLK)
    dtt_p = jnp.transpose(dtt.reshape(SSD_GROUPS, SSD_HPG, N_BLKS, ROW_BLK), (2, 0, 1, 3))
    gn = gnorm_w.reshape(1, SSD_D_INNER)
    cb = conv_b.reshape(1, SSD_CONV_DIM)
    h0_meta = jnp.zeros((1, BATCH, SSD_HEADS, SSD_HEAD_DIM, SSD_STATE), F32)
    prev_p = jnp.zeros((BATCH, SUBLANES, SSD_CONV_DIM), F32)
    y, p_state = _ssd_core(proj, valid.reshape(T_ROWS, 1), validt, dtt_p, a_log, d_skip, conv_w, cb,
                           prev_p, gn, h0_meta, prompt=True, y_init=_mixer_out_init(SSD_D_INNER))
    ns = SAMPLE_ROWS_PER_SEQ
    dtt_s = dtt[:, :DEC_BATCH * ns].reshape(SSD_GROUPS, SSD_HPG, DEC_BATCH, ns)
    dtt_s = jnp.pad(jnp.transpose(dtt_s, (2, 0, 1, 3)), ((0, 0), (0, 0), (0, 0), (0, SSD_Q - ns)))
    validt_s = jnp.pad(valid[:DEC_BATCH * ns].reshape(DEC_BATCH, 1, ns), ((0, 0), (0, 0), (0, SSD_Q - ns)))
    y, s_state = _ssd_core(proj, valid.reshape(T_ROWS, 1), validt_s, dtt_s, a_log, d_skip, conv_w, cb,
                           _pad_conv_prev(state_conv), gn, state_all, prompt=False, y_init=y,
                           h0_layer=layer)
    h = _matmul(y, w_out.astype(BF16), D_MODEL, res=h, tn=MM_TN_MID)
    p_conv = xbc(main0[:, None] + SEQ - (SSD_CONV_K - 1) + tail[None, :])
    s_conv = xbc(sample_t0[:, None] + DEC_SEQ - (SSD_CONV_K - 1) + tail[None, :])
    return h, p_state, p_conv, s_state, s_conv


def _s5_layer(h, valid, norm_w, w_in, lam_re, lam_im, log_dt, b_re, b_im, c_re, c_im, d_skip, w_out,
              state_re, state_im):
    v = _matmul(h, w_in.astype(BF16), D_MODEL, norm_w=norm_w, tn=MM_TN_MID)
    pow_re, pow_im, bb_re, bb_im = _s5_params(lam_re, lam_im, log_dt, b_re, b_im)
    wi_re, wo_re = _s5_block_diag(bb_re, c_re)
    wi_im, wo_im = _s5_block_diag(bb_im, c_im)
    vcol = valid.reshape(T_ROWS, 1)
    zero = jnp.zeros((BATCH, 1, S5_HDIM), F32)
    y, p_re, p_im = _s5_core(v, vcol, wi_re, wi_im, wo_re, wo_im, pow_re, pow_im, d_skip, zero, zero,
                             prompt=True, y_init=_mixer_out_init(D_MODEL))
    y, s_re, s_im = _s5_core(v, vcol, wi_re, wi_im, wo_re, wo_im, pow_re, pow_im, d_skip,
                             state_re.reshape(DEC_BATCH, 1, S5_HDIM),
                             state_im.reshape(DEC_BATCH, 1, S5_HDIM), prompt=False, y_init=y)
    h = _matmul(y, w_out.astype(BF16), D_MODEL, res=h, glu_off=D_MODEL // MM_TN_MID, tn=MM_TN_MID)
    shp = lambda a, n: a.reshape(n, S5_GROUPS, S5_STATE)
    return h, shp(p_re, BATCH), shp(p_im, BATCH), shp(s_re, DEC_BATCH), shp(s_im, DEC_BATCH)


def _to_pairs(s):
    n = s.shape[0]
    s = s.reshape(n, RWKV_PAIRS, 2, RWKV_HEAD_DIM, RWKV_HEAD_DIM)
    return jnp.transpose(s, (0, 1, 3, 2, 4)).reshape(n, RWKV_PAIRS, RWKV_HEAD_DIM, LANES)


def _from_pairs(s):
    n = s.shape[0]
    s = s.reshape(n, RWKV_PAIRS, RWKV_HEAD_DIM, 2, RWKV_HEAD_DIM)
    return jnp.transpose(s, (0, 1, 3, 2, 4)).reshape(n, RWKV_HEADS, RWKV_HEAD_DIM, RWKV_HEAD_DIM)


def _pad_lora(w_down, w_up):
    n = w_down.shape[1]
    return (jnp.pad(w_down, ((0, 0), (0, RWKV_LORA_PAD - n))).astype(BF16),
            jnp.pad(w_up, ((0, RWKV_LORA_PAD - n), (0, 0))).astype(BF16))


def _rwkv_layer(h, valid, norm_w, mu, wr, wk, wv, wo, w0, w1, w2, a0, a1, a2, g1, g2, k_k, k_a, r_k,
                ln_w, ln_b, state, state_shift):
    sample_t0, main0, meta0 = _row_ids()
    u = _rmsnorm(h, norm_w)
    prev = jnp.concatenate([jnp.zeros((1, D_MODEL), F32), u[:-1]], axis=0)
    prev = prev.at[sample_t0].set(state_shift)
    prev = prev.at[main0].set(u[meta0 + N_META - 1])
    prev = prev.at[meta0].set(0.0)
    r, k, v = _rwkv_proj(u, prev, mu, wr.astype(BF16), wk.astype(BF16), wv.astype(BF16))
    w1p, w2p = _pad_lora(w1, w2)
    a1p, a2p = _pad_lora(a1, a2)
    dec, a, g = _rwkv_lora(u, prev, mu, w1p, w2p, a1p, a2p, g1.astype(BF16), g2.astype(BF16), w0, a0)
    vcol = valid.reshape(T_ROWS, 1)
    s0_p = jnp.zeros((BATCH, RWKV_PAIRS, RWKV_HEAD_DIM, LANES), F32)
    y, p_state = _rwkv_core(r, k, v, dec, a, g, vcol, k_k, k_a, r_k, ln_w, ln_b, s0_p, prompt=True,
                            y_init=_mixer_out_init(D_MODEL))
    y, s_state = _rwkv_core(r, k, v, dec, a, g, vcol, k_k, k_a, r_k, ln_w, ln_b, _to_pairs(state),
                            prompt=False, y_init=y)
    h = _matmul(y, wo.astype(BF16), D_MODEL, res=h, tn=MM_TN_MID)
    return (h, _from_pairs(p_state), u[main0 + SEQ - 1], _from_pairs(s_state),
            u[sample_t0 + DEC_SEQ - 1])


def kernel(x_prompt, x_sample, state_ssd, state_ssd_conv, state_s5_re, state_s5_im, state_rwkv,
           state_rwkv_shift, meta_tokens, norm1_w, norm2_w, normf_w, ffn_w_gu, ffn_w_down,
           ssd_w_in, ssd_conv_w, ssd_conv_b, ssd_dt_bias, ssd_a_log, ssd_d, ssd_norm_w, ssd_w_out,
           s5_w_in, s5_lam_re, s5_lam_im, s5_log_dt, s5_b_re, s5_b_im, s5_c_re, s5_c_im, s5_d, s5_w_out,
           rwkv_mu, rwkv_wr, rwkv_wk, rwkv_wv, rwkv_wo, rwkv_w0, rwkv_w1, rwkv_w2, rwkv_a0, rwkv_a1,
           rwkv_a2, rwkv_g1, rwkv_g2, rwkv_k_k, rwkv_k_a, rwkv_r_k, rwkv_ln_w, rwkv_ln_b):
    valid = _valid_mask()
    sample = jnp.pad(x_sample, ((0, 0), (0, SAMPLE_ROWS_PER_SEQ - DEC_SEQ), (0, 0)))
    meta = jnp.pad(jnp.broadcast_to(meta_tokens[None], (BATCH, N_META, D_MODEL)),
                   ((0, 0), (META_PAD, 0), (0, 0)))
    h = jnp.concatenate([sample.reshape(-1, D_MODEL), x_prompt.reshape(-1, D_MODEL),
                         meta.reshape(-1, D_MODEL)], axis=0).astype(F32)
    outs = {name: [] for name in ("p_ssd", "p_conv", "p_s5r", "p_s5i", "p_rwkv", "p_shift",
                                  "s_ssd", "s_conv", "s_s5r", "s_s5i", "s_rwkv", "s_shift")}
    for i in range(DEPTH):
        kind, j = i % N_MIXERS, i // N_MIXERS
        if kind == 0:
            h, ps, pc, ss, sc = _ssd_layer(
                h, valid, norm1_w[i], ssd_w_in[j], ssd_conv_w[j], ssd_conv_b[j], ssd_dt_bias[j],
                ssd_a_log[j], ssd_d[j], ssd_norm_w[j], ssd_w_out[j], state_ssd, j, state_ssd_conv[j])
            outs["p_ssd"].append(ps); outs["p_conv"].append(pc)
            outs["s_ssd"].append(ss); outs["s_conv"].append(sc)
        elif kind == 1:
            h, pr, pi, sr, si = _s5_layer(
                h, valid, norm1_w[i], s5_w_in[j], s5_lam_re[j], s5_lam_im[j], s5_log_dt[j], s5_b_re[j],
                s5_b_im[j], s5_c_re[j], s5_c_im[j], s5_d[j], s5_w_out[j], state_s5_re[j], state_s5_im[j])
            outs["p_s5r"].append(pr); outs["p_s5i"].append(pi)
            outs["s_s5r"].append(sr); outs["s_s5i"].append(si)
        else:
            h, ps, psh, ss, ssh = _rwkv_layer(
                h, valid, norm1_w[i], rwkv_mu[j], rwkv_wr[j], rwkv_wk[j], rwkv_wv[j], rwkv_wo[j],
                rwkv_w0[j], rwkv_w1[j], rwkv_w2[j], rwkv_a0[j], rwkv_a1[j], rwkv_a2[j], rwkv_g1[j],
                rwkv_g2[j], rwkv_k_k[j], rwkv_k_a[j], rwkv_r_k[j], rwkv_ln_w[j], rwkv_ln_b[j],
                state_rwkv[j], state_rwkv_shift[j])
            outs["p_rwkv"].append(ps); outs["p_shift"].append(psh)
            outs["s_rwkv"].append(ss); outs["s_shift"].append(ssh)
        h = _ffn(h, norm2_w[i], ffn_w_gu[i].astype(BF16), ffn_w_down[i].astype(BF16))
    yf = _rmsnorm(h, normf_w)
    y_sample = yf[:DEC_BATCH * SAMPLE_ROWS_PER_SEQ].reshape(DEC_BATCH, SAMPLE_ROWS_PER_SEQ, D_MODEL)
    y_prompt = yf[MAIN_BLK0 * ROW_BLK:META_BLK0 * ROW_BLK].reshape(BATCH, SEQ, D_MODEL)
    st = lambda name: jnp.stack(outs[name])
    return (y_prompt, y_sample[:, :DEC_SEQ],
            st("p_ssd"), st("p_conv"), st("p_s5r"), st("p_s5i"), st("p_rwkv"), st("p_shift"),
            st("s_ssd"), st("s_conv"), st("s_s5r"), st("s_s5i"), st("s_rwkv"), st("s_shift"))
```

```python
import functools
import math

import jax
import jax.numpy as jnp
from jax import lax
from jax.experimental import pallas as pl
from jax.experimental.pallas import tpu as pltpu

F32 = jnp.float32
BF16 = jnp.bfloat16

D_MODEL = 2048
BATCH = 4
SEQ = 2048
DEPTH = 4
DEC_BATCH = 128
DEC_SEQ = 4
N_META = 16
N_MIXERS = 3
NORM_EPS = 1e-6

SSD_D_INNER = 2 * D_MODEL
SSD_HEAD_DIM = 64
SSD_HEADS = SSD_D_INNER // SSD_HEAD_DIM
SSD_STATE = 128
SSD_GROUPS = 8
SSD_HPG = SSD_HEADS // SSD_GROUPS
SSD_GN = SSD_GROUPS * SSD_STATE
SSD_CONV_K = 4
SSD_CONV_DIM = SSD_D_INNER + 2 * SSD_GN
SSD_ZX_DIM = SSD_D_INNER + SSD_CONV_DIM
SSD_GROUP_W = SSD_D_INNER // SSD_GROUPS

S5_GROUP_SIZE = 16
S5_GROUPS = D_MODEL // S5_GROUP_SIZE
S5_STATE = 64
S5_HDIM = S5_GROUPS * S5_STATE
S5_SBLK = 1024
S5_NSB = S5_HDIM // S5_SBLK
S5_VBLK = S5_SBLK // S5_STATE * S5_GROUP_SIZE

RWKV_HEAD_DIM = 64
RWKV_HEADS = D_MODEL // RWKV_HEAD_DIM
RWKV_PAIRS = RWKV_HEADS // 2
RWKV_LORA_PAD = 128
RWKV_G_LORA = 256
RWKV_LN_EPS = 64e-5

FFN_HIDDEN = -(-8 * D_MODEL // (3 * 256)) * 256

LANES = 128
SUBLANES = 8
ROW_BLK = 128
SAMPLE_ROWS_PER_SEQ = SUBLANES
SAMPLE_BLKS = DEC_BATCH * SAMPLE_ROWS_PER_SEQ // ROW_BLK
MAIN_BLK0 = SAMPLE_BLKS
MAIN_BLKS_PER_B = SEQ // ROW_BLK
META_BLK0 = MAIN_BLK0 + BATCH * MAIN_BLKS_PER_B
N_BLKS = META_BLK0 + BATCH
T_ROWS = N_BLKS * ROW_BLK
PROMPT_CHUNKS = 1 + MAIN_BLKS_PER_B
META_PAD = ROW_BLK - N_META

MM_TM = 512
MM_TN = 512
MM_TN_MID = 1024
MM_TN_WIDE = 2048
VMEM_LIMIT = 48 * 2 ** 20


def _prompt_blk(b, c):
    return jnp.where(c == 0, META_BLK0 + b, MAIN_BLK0 + MAIN_BLKS_PER_B * b + c - 1)


def _cparams(*sem):
    return pltpu.CompilerParams(dimension_semantics=sem, vmem_limit_bytes=VMEM_LIMIT)


def _silu(x):
    return x * jax.nn.sigmoid(x)


def _softplus(x):
    return jnp.maximum(x, 0.0) + jnp.log(1.0 + jnp.exp(-jnp.abs(x)))


def _rms(x, w):
    return x * lax.rsqrt(jnp.mean(x * x, axis=-1, keepdims=True) + NORM_EPS) * w


def _dot(a, b):
    return jnp.dot(a, b, preferred_element_type=F32)


def _dot_nt(a, b):
    return lax.dot_general(a, b, (((1,), (1,)), ((), ())), preferred_element_type=F32)


def _dot_tn(a, b):
    return lax.dot_general(a, b, (((0,), (0,)), ((), ())), preferred_element_type=F32)


def _split_bf16(x):
    hi = x.astype(BF16)
    return hi, (x - hi.astype(F32)).astype(BF16)


def _drop_ref(kernel_fn, idx):
    def wrapped(*refs):
        return kernel_fn(*refs[:idx], *refs[idx + 1:])
    return wrapped


def _mm_kernel(*refs, normalize, glu, has_res):
    it = iter(refs)
    x_ref = next(it)
    nw_ref = next(it) if normalize else None
    w_ref = next(it)
    w2_ref = next(it) if glu else None
    res_ref = next(it) if has_res else None
    o_ref = next(it)
    xn_ref = next(it) if normalize else None
    if normalize:
        @pl.when(pl.program_id(1) == 0)
        def _():
            xn_ref[...] = _rms(x_ref[...], nw_ref[...]).astype(BF16)
        xb = xn_ref[...]
    else:
        xb = x_ref[...].astype(BF16)
    acc = _dot(xb, w_ref[...])
    if glu:
        acc = acc * jax.nn.sigmoid(_dot(xb, w2_ref[...]))
    if has_res:
        acc = acc + res_ref[...]
    o_ref[...] = acc.astype(o_ref.dtype)


def _matmul(x, w, n_out, *, norm_w=None, res=None, glu_off=None, out_dtype=F32, tn=MM_TN):
    m, k = x.shape
    normalize = norm_w is not None
    glu = glu_off is not None
    has_res = res is not None
    in_specs = [pl.BlockSpec((MM_TM, k), lambda i, j: (i, 0))]
    args = [x]
    if normalize:
        in_specs.append(pl.BlockSpec((1, k), lambda i, j: (0, 0)))
        args.append(norm_w.reshape(1, k))
    in_specs.append(pl.BlockSpec((k, tn), lambda i, j: (0, j)))
    args.append(w)
    if glu:
        in_specs.append(pl.BlockSpec((k, tn), lambda i, j: (0, j + glu_off)))
        args.append(w)
    if has_res:
        in_specs.append(pl.BlockSpec((MM_TM, tn), lambda i, j: (i, j)))
        args.append(res)
    return pl.pallas_call(
        functools.partial(_mm_kernel, normalize=normalize, glu=glu, has_res=has_res),
        grid=(m // MM_TM, n_out // tn),
        in_specs=in_specs,
        out_specs=pl.BlockSpec((MM_TM, tn), lambda i, j: (i, j)),
        out_shape=jax.ShapeDtypeStruct((m, n_out), out_dtype),
        scratch_shapes=[pltpu.VMEM((MM_TM, k), BF16)] if normalize else [],
        compiler_params=_cparams("parallel", "arbitrary"),
        name="matmul_n%d%s%s%s" % (n_out, "_norm" * normalize, "_glu" * glu, "_res" * has_res),
    )(*args)


def _rmsnorm_kernel(x_ref, w_ref, o_ref):
    o_ref[...] = _rms(x_ref[...], w_ref[...])


def _rmsnorm(x, w):
    m, k = x.shape
    return pl.pallas_call(
        _rmsnorm_kernel,
        grid=(m // MM_TM,),
        in_specs=[pl.BlockSpec((MM_TM, k), lambda i: (i, 0)), pl.BlockSpec((1, k), lambda i: (0, 0))],
        out_specs=pl.BlockSpec((MM_TM, k), lambda i: (i, 0)),
        out_shape=jax.ShapeDtypeStruct((m, k), F32),
        compiler_params=_cparams("parallel"),
        name="rmsnorm",
    )(x, w.reshape(1, k))


FFN_TF = 512
FFN_NF = FFN_HIDDEN // FFN_TF


def _ffn_kernel(x_ref, nw_ref, wg_ref, wu_ref, wd_ref, o_ref, xn_ref):
    f = pl.program_id(1)

    @pl.when(f == 0)
    def _():
        x = x_ref[...]
        xn_ref[...] = _rms(x, nw_ref[...]).astype(BF16)
        o_ref[...] = x

    xb = xn_ref[...]
    act = _silu(_dot(xb, wg_ref[...])) * _dot(xb, wu_ref[...])
    o_ref[...] += _dot(act.astype(BF16), wd_ref[...])


def _ffn(h, norm_w, w_gu, w_down):
    m = h.shape[0]
    return pl.pallas_call(
        _ffn_kernel,
        grid=(m // MM_TM, FFN_NF),
        in_specs=[
            pl.BlockSpec((MM_TM, D_MODEL), lambda i, f: (i, 0)),
            pl.BlockSpec((1, D_MODEL), lambda i, f: (0, 0)),
            pl.BlockSpec((D_MODEL, FFN_TF), lambda i, f: (0, f)),
            pl.BlockSpec((D_MODEL, FFN_TF), lambda i, f: (0, f + FFN_NF)),
            pl.BlockSpec((FFN_TF, D_MODEL), lambda i, f: (f, 0)),
        ],
        out_specs=pl.BlockSpec((MM_TM, D_MODEL), lambda i, f: (i, 0)),
        out_shape=jax.ShapeDtypeStruct((m, D_MODEL), F32),
        scratch_shapes=[pltpu.VMEM((MM_TM, D_MODEL), BF16)],
        compiler_params=_cparams("parallel", "arbitrary"),
        name="ffn",
    )(h, norm_w.reshape(1, D_MODEL), w_gu, w_gu, w_down)


def _dt_kernel(x_ref, nw_ref, wt_ref, bias_ref, o_ref):
    xb = _rms(x_ref[...], nw_ref[...]).astype(BF16)
    o_ref[...] = _softplus(_dot_nt(wt_ref[...], xb) + bias_ref[...])


def _ssd_dt(h, norm_w, w_dt_t, dt_bias):
    m = h.shape[0]
    return pl.pallas_call(
        _dt_kernel,
        grid=(m // MM_TM,),
        in_specs=[
            pl.BlockSpec((MM_TM, D_MODEL), lambda i: (i, 0)),
            pl.BlockSpec((1, D_MODEL), lambda i: (0, 0)),
            pl.BlockSpec((SSD_HEADS, D_MODEL), lambda i: (0, 0)),
            pl.BlockSpec((SSD_HEADS, 1), lambda i: (0, 0)),
        ],
        out_specs=pl.BlockSpec((SSD_HEADS, MM_TM), lambda i: (0, i)),
        out_shape=jax.ShapeDtypeStruct((SSD_HEADS, m), F32),
        compiler_params=_cparams("parallel"),
        name="ssd_dt",
    )(h, norm_w.reshape(1, D_MODEL), w_dt_t, dt_bias.reshape(SSD_HEADS, 1))


SSD_Q = ROW_BLK
SSD_SAMPLE_SEQS = 4
SSD_MIN_COLS = 16


def _ssd_kernel(z_ref, x_ref, b_ref, c_ref, valid_ref, validt_ref, dtt_ref, alog_ref, dskip_ref,
                cwx_ref, cwb_ref, cwc_ref, cbx_ref, cbb_ref, cbc_ref, cpx_ref, cpb_ref, cpc_ref,
                nw_ref, h0_ref, *rest, rows, nch, nsb, n_prev):
    c = pl.program_id(2)
    q = max(rows, SSD_MIN_COLS)
    chained = nch > 1
    if n_prev:
        pst_ref, y_ref, hl_all_ref, st_ref, xpx_ref, xpb_ref, xpc_ref = rest
        hl_all_ref[0:n_prev] = pst_ref[...]
        hl_ref = hl_all_ref.at[n_prev]
    else:
        y_ref, hl_all_ref, st_ref, xpx_ref, xpb_ref, xpc_ref = rest
        hl_ref = hl_all_ref.at[0]

    if chained:
        @pl.when(c == 0)
        def _():
            st_ref[...] = h0_ref[0]
            xpx_ref[0, 0:SUBLANES] = cpx_ref[0]
            xpb_ref[0, 0:SUBLANES] = cpb_ref[0]
            xpc_ref[0, 0:SUBLANES] = cpc_ref[0]

    def pad_rows(x):
        if rows == q:
            return x
        return jnp.concatenate([x, jnp.zeros((q - rows, x.shape[1]), x.dtype)], axis=0)

    def conv(si, raw, xp_ref, cp_ref, w_ref, bias_ref):
        if not chained:
            xp_ref[si, 0:SUBLANES] = cp_ref[si]
        xp_ref[si, SUBLANES:SUBLANES + rows] = raw
        acc = bias_ref[...]
        for k in range(SSD_CONV_K):
            acc = acc + w_ref[k:k + 1, :] * xp_ref[si, pl.ds(SUBLANES - SSD_CONV_K + 1 + k, rows), :]
        if chained:
            xp_ref[si, 0:SUBLANES] = raw[rows - SUBLANES:rows]
        return _silu(acc)

    a_col = -jnp.exp(alog_ref[0])
    lane = lax.broadcasted_iota(jnp.int32, (SSD_HPG, SSD_Q), 1)
    row_i = lax.broadcasted_iota(jnp.int32, (rows, q), 0)
    col_i = lax.broadcasted_iota(jnp.int32, (rows, q), 1)
    eye = row_i == col_i
    causal = row_i >= col_i
    head_a = lax.broadcasted_iota(jnp.int32, (1, LANES), 1) < SSD_HEAD_DIM
    row_a = lax.broadcasted_iota(jnp.int32, (2 * SSD_HEAD_DIM, 1), 0) < SSD_HEAD_DIM
    dskip = dskip_ref[0]

    def to_col(rowvec):
        return jnp.sum(jnp.where(eye, rowvec, 0.0), axis=1, keepdims=True)

    def pair_stages(si, p, xs, bm_pad, cm, cb, cum, dtt, ys):
        xh = xs[:, LANES * p:LANES * (p + 1)]
        xhb = pad_rows(xh).astype(BF16)
        if chained:
            st = st_ref[2 * p:2 * p + 2]
        else:
            st = h0_ref[si, 2 * p:2 * p + 2]
        st = st.reshape(2 * SSD_HEAD_DIM, SSD_STATE)
        yo = _dot_nt(cm, st.astype(BF16))
        yd, e_col, w_col, e_last = [], [], [], []
        for r in (2 * p, 2 * p + 1):
            cum_row = cum[r:r + 1, :]
            dt_row = dtt[r:r + 1, :]
            cum_col = to_col(cum_row)
            dt_col = to_col(dt_row)
            lmat = jnp.exp(jnp.where(causal, cum_col - cum_row, -jnp.inf))
            yd.append(_dot((cb * lmat * dt_row).astype(BF16), xhb))
            c_last = cum_row[:, q - 1:q]
            e_col.append(jnp.exp(cum_col))
            w_col.append(jnp.exp(c_last - cum_col) * dt_col)
            e_last.append(jnp.exp(c_last))
        yield
        d2 = jnp.where(head_a, dskip[:, 2 * p:2 * p + 1], dskip[:, 2 * p + 1:2 * p + 2])
        ys[p] = (jnp.where(head_a, yd[0], yd[1]) + yo * jnp.where(head_a, e_col[0], e_col[1]) + xh * d2)
        xw = pad_rows(xh * jnp.where(head_a, w_col[0], w_col[1])).astype(BF16)
        st_new = st * jnp.where(row_a, e_last[0], e_last[1]) + _dot_tn(xw, bm_pad)
        yield
        st_new = st_new.reshape(2, SSD_HEAD_DIM, SSD_STATE)
        if chained:
            st_ref[2 * p:2 * p + 2] = st_new
        else:
            hl_ref[si, 2 * p:2 * p + 2] = st_new

    live, outs = [], []
    for si in range(nsb):
        rsl = slice(si * rows, (si + 1) * rows)
        valid = valid_ref[rsl]
        xs = conv(si, x_ref[rsl] * valid, xpx_ref, cpx_ref, cwx_ref, cbx_ref)
        bm = conv(si, b_ref[rsl] * valid, xpb_ref, cpb_ref, cwb_ref, cbb_ref)
        cm = conv(si, c_ref[rsl] * valid, xpc_ref, cpc_ref, cwc_ref, cbc_ref).astype(BF16)
        bm_pad = pad_rows(bm).astype(BF16)
        dtt = dtt_ref[si, 0] * validt_ref[si]
        cum = dtt * a_col
        s = 1
        while s < q:
            cum = cum + jnp.where(lane >= s, pltpu.roll(cum, s, axis=1), 0.0)
            s *= 2
        dtt, cum = dtt[:, :q], cum[:, :q]
        cb = _dot_nt(cm, bm_pad)
        ys = [None] * (SSD_HPG // 2)
        outs.append((rsl, ys))
        live += [pair_stages(si, p, xs, bm_pad, cm, cb, cum, dtt, ys) for p in range(SSD_HPG // 2)]
    while live:
        live = [gen for gen in live if next(gen, True) is None]
    y = jnp.concatenate([jnp.concatenate(ys, axis=1) for _, ys in outs], axis=0)
    y = y * _silu(z_ref[...])
    y_ref[...] = _rms(y, nw_ref[...]).astype(y_ref.dtype)

    if chained:
        @pl.when(c == nch - 1)
        def _():
            hl_ref[0] = st_ref[...]


def _ssd_core(proj, valid, validt, dtt, a_log, d_skip, conv_w, conv_b, conv_prev8, norm_w, h0, *,
              prompt, y_init=None, h0_layer=0, prev_states=None):
    nseq = h0.shape[1]
    if prompt:
        rows, nch, nsb = ROW_BLK, PROMPT_CHUNKS, 1
        blk = _prompt_blk
    else:
        rows, nch, nsb = SAMPLE_ROWS_PER_SEQ, 1, SSD_SAMPLE_SEQS
        blk = lambda b, c: b
    gw = SSD_GROUP_W
    x_off = SSD_D_INNER // gw
    b_off = (2 * SSD_D_INNER) // SSD_STATE
    c_off = b_off + SSD_GROUPS
    cb_off = SSD_D_INNER // SSD_STATE
    cc_off = cb_off + SSD_GROUPS
    row_spec = lambda w, off: pl.BlockSpec((nsb * rows, w), lambda b, g, c: (blk(b, c), g + off))
    par_spec = lambda r, w, off: pl.BlockSpec((r, w), lambda b, g, c: (0, g + off))
    prev_spec = lambda w, off: pl.BlockSpec((nsb, SUBLANES, w), lambda b, g, c: (b, 0, g + off))
    st_spec = pl.BlockSpec((nsb, SSD_HPG, SSD_HEAD_DIM, SSD_STATE), lambda b, g, c: (b, g, 0, 0))
    in_specs = [
        row_spec(gw, 0), row_spec(gw, x_off), row_spec(SSD_STATE, b_off), row_spec(SSD_STATE, c_off),
        pl.BlockSpec((nsb * rows, 1), lambda b, g, c: (blk(b, c), 0)),
        pl.BlockSpec((nsb, 1, SSD_Q), lambda b, g, c: (blk(b, c), 0, 0)),
        pl.BlockSpec((nsb, 1, SSD_HPG, SSD_Q), lambda b, g, c: (blk(b, c), g, 0, 0)),
        pl.BlockSpec((1, SSD_HPG, 1), lambda b, g, c: (g, 0, 0)),
        pl.BlockSpec((1, 1, SSD_HPG), lambda b, g, c: (g, 0, 0)),
        par_spec(SSD_CONV_K, gw, 0), par_spec(SSD_CONV_K, SSD_STATE, cb_off),
        par_spec(SSD_CONV_K, SSD_STATE, cc_off),
        par_spec(1, gw, 0), par_spec(1, SSD_STATE, cb_off), par_spec(1, SSD_STATE, cc_off),
        prev_spec(gw, 0), prev_spec(SSD_STATE, cb_off), prev_spec(SSD_STATE, cc_off),
        par_spec(1, gw, 0),
        pl.BlockSpec((None, nsb, SSD_HPG, SSD_HEAD_DIM, SSD_STATE),
                     lambda b, g, c: (h0_layer, b, g, 0, 0)),
    ]
    args = [proj, proj, proj, proj, valid, validt, dtt,
            a_log.reshape(SSD_GROUPS, SSD_HPG, 1), d_skip.reshape(SSD_GROUPS, 1, SSD_HPG),
            conv_w, conv_w, conv_w, conv_b, conv_b, conv_b, conv_prev8, conv_prev8, conv_prev8,
            norm_w, h0]
    n_prev = 0 if prev_states is None else prev_states.shape[0]
    all_spec = lambda n: pl.BlockSpec((n, nsb, SSD_HPG, SSD_HEAD_DIM, SSD_STATE),
                                      lambda b, g, c: (0, b, g, 0, 0))
    if n_prev:
        in_specs.append(all_spec(n_prev))
        args.append(prev_states)
    kernel_fn = functools.partial(_ssd_kernel, rows=rows, nch=nch, nsb=nsb, n_prev=n_prev)
    aliases = {}
    if y_init is not None:
        in_specs.append(pl.BlockSpec(memory_space=pl.ANY))
        args.append(y_init)
        aliases = {len(args) - 1: 0}
        kernel_fn = _drop_ref(kernel_fn, len(args) - 1)
    return pl.pallas_call(
        kernel_fn,
        grid=(nseq // nsb, SSD_GROUPS, nch),
        in_specs=in_specs,
        out_specs=[row_spec(gw, 0), all_spec(n_prev + 1)],
        input_output_aliases=aliases,
        out_shape=[jax.ShapeDtypeStruct((T_ROWS, SSD_D_INNER), BF16),
                   jax.ShapeDtypeStruct((n_prev + 1,) + h0.shape[1:], F32)],
        scratch_shapes=[
            pltpu.VMEM((SSD_HPG, SSD_HEAD_DIM, SSD_STATE), F32),
            pltpu.VMEM((nsb, SUBLANES + rows, gw), F32),
            pltpu.VMEM((nsb, SUBLANES + rows, SSD_STATE), F32),
            pltpu.VMEM((nsb, SUBLANES + rows, SSD_STATE), F32),
        ],
        compiler_params=_cparams("parallel", "parallel", "arbitrary"),
        name="ssd_core_prompt" if prompt else "ssd_core_sample",
    )(*args)


def _s5_abar(lam_re, lam_im, log_dt):
    dt = jnp.exp(log_dt)
    mag = jnp.exp(lam_re * dt)
    ang = lam_im * dt
    return mag * jnp.cos(ang), mag * jnp.sin(ang)


def _s5_pow_kernel(lr_ref, li_ref, ldt_ref, pr_ref, pi_ref):
    ar, ai = _s5_abar(lr_ref[...], li_ref[...], ldt_ref[...])
    row = lax.broadcasted_iota(jnp.int32, (SUBLANES, S5_HDIM), 0)
    pr, pi = ar, ai
    out_r = jnp.broadcast_to(ar, (SUBLANES, S5_HDIM))
    out_i = jnp.broadcast_to(ai, (SUBLANES, S5_HDIM))
    for k in range(1, SUBLANES):
        pr, pi = pr * ar - pi * ai, pr * ai + pi * ar
        out_r = jnp.where(row == k, pr, out_r)
        out_i = jnp.where(row == k, pi, out_i)
    pr_ref[...] = out_r
    pi_ref[...] = out_i


def _s5_bbar_kernel(lr_ref, li_ref, ldt_ref, br_ref, bi_ref, or_ref, oi_ref):
    lr, li = lr_ref[...], li_ref[...]
    ar, ai = _s5_abar(lr, li, ldt_ref[...])
    den = lr * lr + li * li
    f_re = ((ar - 1.0) * lr + ai * li) / den
    f_im = (ai * lr - (ar - 1.0) * li) / den
    br, bi = br_ref[...], bi_ref[...]
    or_ref[...] = f_re * br - f_im * bi
    oi_ref[...] = f_re * bi + f_im * br


def _s5_params(lam_re, lam_im, log_dt, b_re, b_im):
    ldt = jnp.repeat(log_dt, S5_STATE)
    row = lambda a: a.reshape(1, S5_HDIM)
    col = lambda a: a.reshape(S5_HDIM, 1)
    full = lambda shape: pl.BlockSpec(shape, lambda: (0,) * len(shape))
    pow_re, pow_im = pl.pallas_call(
        _s5_pow_kernel,
        in_specs=[full((1, S5_HDIM))] * 3,
        out_specs=[full((SUBLANES, S5_HDIM))] * 2,
        out_shape=[jax.ShapeDtypeStruct((SUBLANES, S5_HDIM), F32)] * 2,
    )(row(lam_re), row(lam_im), row(ldt))
    bb_re, bb_im = pl.pallas_call(
        _s5_bbar_kernel,
        in_specs=[full((S5_HDIM, 1))] * 3 + [full((S5_HDIM, S5_GROUP_SIZE))] * 2,
        out_specs=[full((S5_HDIM, S5_GROUP_SIZE))] * 2,
        out_shape=[jax.ShapeDtypeStruct((S5_HDIM, S5_GROUP_SIZE), F32)] * 2,
    )(col(lam_re), col(lam_im), col(ldt), b_re.reshape(S5_HDIM, S5_GROUP_SIZE),
      b_im.reshape(S5_HDIM, S5_GROUP_SIZE))
    return pow_re, pow_im, bb_re, bb_im


def _s5_block_diag(bb, c):
    gpb = S5_SBLK // S5_STATE
    eye = jnp.eye(gpb, dtype=F32)
    bb = bb.reshape(S5_NSB, gpb, S5_STATE, S5_GROUP_SIZE)
    w_in = eye[None, :, None, :, None] * jnp.transpose(bb, (0, 3, 1, 2))[:, None]
    w_in = w_in.reshape(S5_NSB, S5_VBLK, S5_SBLK)
    c = c.reshape(S5_NSB, gpb, S5_GROUP_SIZE, S5_STATE)
    w_out = eye[None, :, None, :, None] * jnp.transpose(c, (0, 1, 3, 2))[:, :, :, None, :]
    w_out = w_out.reshape(S5_NSB, S5_SBLK, S5_VBLK)
    return w_in.astype(BF16), w_out.astype(BF16)


def _s5_kernel(v_ref, valid_ref, wir_ref, wii_ref, wor_ref, woi_ref, pr_ref, pi_ref, d_ref,
               h0r_ref, h0i_ref, y_ref, hlr_ref, hli_ref, cr_ref, ci_ref, hr_ref, hi_ref, *,
               per_tile, nch, last_row):
    ntiles = ROW_BLK // SUBLANES
    if not per_tile:
        c = pl.program_id(2)

        @pl.when(c == 0)
        def _():
            cr_ref[...] = h0r_ref[0]
            ci_ref[...] = h0i_ref[0]

    v = v_ref[...]
    vb = (v * valid_ref[...]).astype(BF16)
    bu_r = _dot(vb, wir_ref[0])
    bu_i = _dot(vb, wii_ref[0])
    pr, pi = pr_ref[...], pi_ref[...]
    row = lax.broadcasted_iota(jnp.int32, (SUBLANES, S5_SBLK), 0)
    levels = []
    for s in (1, 2, 4):
        levels.append((s, jnp.where(row >= s, pr[s - 1:s], 0.0), jnp.where(row >= s, pi[s - 1:s], 0.0)))
    if not per_tile:
        car_r, car_i = cr_ref[...], ci_ref[...]
    for i in range(ntiles):
        xr = bu_r[SUBLANES * i:SUBLANES * (i + 1)]
        xi = bu_i[SUBLANES * i:SUBLANES * (i + 1)]
        for s, ar, ai in levels:
            sr = pltpu.roll(xr, s, axis=0)
            si = pltpu.roll(xi, s, axis=0)
            xr, xi = xr + ar * sr - ai * si, xi + ar * si + ai * sr
        if per_tile:
            car_r, car_i = h0r_ref[i], h0i_ref[i]
        hr = xr + pr * car_r - pi * car_i
        hi = xi + pr * car_i + pi * car_r
        hr_ref[SUBLANES * i:SUBLANES * (i + 1)] = hr
        hi_ref[SUBLANES * i:SUBLANES * (i + 1)] = hi
        if per_tile:
            hlr_ref[i] = hr[last_row:last_row + 1]
            hli_ref[i] = hi[last_row:last_row + 1]
        else:
            car_r, car_i = hr[last_row:last_row + 1], hi[last_row:last_row + 1]
    y = (_dot(hr_ref[...].astype(BF16), wor_ref[0]) - _dot(hi_ref[...].astype(BF16), woi_ref[0])
         + d_ref[...] * v)
    y_ref[...] = jax.nn.gelu(y).astype(y_ref.dtype)
    if not per_tile:
        cr_ref[...] = car_r
        ci_ref[...] = car_i

        @pl.when(c == nch - 1)
        def _():
            hlr_ref[0] = car_r
            hli_ref[0] = car_i


def _s5_core(v, valid, w_in_re, w_in_im, w_out_re, w_out_im, pow_re, pow_im, d_skip, h0_re, h0_im, *,
             prompt, y_init=None):
    nseq = h0_re.shape[0]
    if prompt:
        grid = (nseq, S5_NSB, PROMPT_CHUNKS)
        blk = _prompt_blk
        sidx = lambda b, s, c: (b, 0, s)
        spb = 1
        kw = dict(per_tile=False, nch=PROMPT_CHUNKS, last_row=SUBLANES - 1)
        sem = ("parallel", "parallel", "arbitrary")
    else:
        grid = (SAMPLE_BLKS, S5_NSB, 1)
        blk = lambda b, c: b
        sidx = lambda b, s, c: (b, 0, s)
        spb = ROW_BLK // SAMPLE_ROWS_PER_SEQ
        kw = dict(per_tile=True, nch=1, last_row=DEC_SEQ - 1)
        sem = ("parallel", "parallel", "arbitrary")
    row_spec = pl.BlockSpec((ROW_BLK, S5_VBLK), lambda b, s, c: (blk(b, c), s))
    st_spec = pl.BlockSpec((spb, 1, S5_SBLK), sidx)
    w_in_spec = pl.BlockSpec((1, S5_VBLK, S5_SBLK), lambda b, s, c: (s, 0, 0))
    w_out_spec = pl.BlockSpec((1, S5_SBLK, S5_VBLK), lambda b, s, c: (s, 0, 0))
    pow_spec = pl.BlockSpec((SUBLANES, S5_SBLK), lambda b, s, c: (0, s))
    in_specs = [row_spec, pl.BlockSpec((ROW_BLK, 1), lambda b, s, c: (blk(b, c), 0)),
                w_in_spec, w_in_spec, w_out_spec, w_out_spec, pow_spec, pow_spec,
                pl.BlockSpec((1, S5_VBLK), lambda b, s, c: (0, s)), st_spec, st_spec]
    args = [v, valid, w_in_re, w_in_im, w_out_re, w_out_im, pow_re, pow_im,
            d_skip.reshape(1, D_MODEL), h0_re, h0_im]
    kernel_fn = functools.partial(_s5_kernel, **kw)
    aliases = {}
    if y_init is not None:
        in_specs.append(pl.BlockSpec(memory_space=pl.ANY))
        args.append(y_init)
        aliases = {len(args) - 1: 0}
        kernel_fn = _drop_ref(kernel_fn, len(args) - 1)
    return pl.pallas_call(
        kernel_fn,
        grid=grid,
        in_specs=in_specs,
        out_specs=[row_spec, st_spec, st_spec],
        input_output_aliases=aliases,
        out_shape=[jax.ShapeDtypeStruct((T_ROWS, D_MODEL), BF16),
                   jax.ShapeDtypeStruct(h0_re.shape, F32), jax.ShapeDtypeStruct(h0_im.shape, F32)],
        scratch_shapes=[pltpu.VMEM((1, S5_SBLK), F32), pltpu.VMEM((1, S5_SBLK), F32),
                        pltpu.VMEM((ROW_BLK, S5_SBLK), F32), pltpu.VMEM((ROW_BLK, S5_SBLK), F32)],
        compiler_params=_cparams(*sem),
        name="s5_core_prompt" if prompt else "s5_core_sample",
    )(*args)


def _rwkv_proj_kernel(u_ref, p_ref, mu_ref, wr_ref, wk_ref, wv_ref, r_ref, k_ref, v_ref,
                      xr_ref, xk_ref, xv_ref):
    @pl.when(pl.program_id(1) == 0)
    def _():
        u = u_ref[...]
        d = p_ref[...] - u
        xr_ref[...] = (u + d * mu_ref[0:1, :]).astype(BF16)
        xk_ref[...] = (u + d * mu_ref[2:3, :]).astype(BF16)
        xv_ref[...] = (u + d * mu_ref[3:4, :]).astype(BF16)

    r_ref[...] = _dot(xr_ref[...], wr_ref[...])
    k_ref[...] = _dot(xk_ref[...], wk_ref[...])
    v_ref[...] = _dot(xv_ref[...], wv_ref[...])


def _rwkv_proj(u, prev, mu, wr, wk, wv):
    m = u.shape[0]
    row = pl.BlockSpec((MM_TM, D_MODEL), lambda i, j: (i, 0))
    wsp = pl.BlockSpec((D_MODEL, MM_TN), lambda i, j: (0, j))
    osp = pl.BlockSpec((MM_TM, MM_TN), lambda i, j: (i, j))
    return pl.pallas_call(
        _rwkv_proj_kernel,
        grid=(m // MM_TM, D_MODEL // MM_TN),
        in_specs=[row, row, pl.BlockSpec((6, D_MODEL), lambda i, j: (0, 0)), wsp, wsp, wsp],
        out_specs=[osp, osp, osp],
        out_shape=[jax.ShapeDtypeStruct((m, D_MODEL), F32)] * 3,
        scratch_shapes=[pltpu.VMEM((MM_TM, D_MODEL), BF16)] * 3,
        compiler_params=_cparams("parallel", "arbitrary"),
        name="rwkv_proj",
    )(u, prev, mu, wr, wk, wv)


RWKV_LORA_TM = 256


def _rwkv_lora_kernel(u_ref, p_ref, mu_ref, w1_ref, w2_ref, a1_ref, a2_ref, g1_ref, g2_ref,
                      w0_ref, a0_ref, lw_ref, a_ref, g_ref):
    u = u_ref[...]
    d = p_ref[...] - u
    xw = (u + d * mu_ref[1:2, :]).astype(BF16)
    xa = (u + d * mu_ref[4:5, :]).astype(BF16)
    xg = (u + d * mu_ref[5:6, :]).astype(BF16)
    wpre = w0_ref[...] + _dot(jnp.tanh(_dot(xw, w1_ref[...])).astype(BF16), w2_ref[...])
    w = -_softplus(-wpre) - 0.5
    lw_ref[...] = -jnp.exp(w)
    a_ref[...] = jax.nn.sigmoid(a0_ref[...] + _dot(_dot(xa, a1_ref[...]).astype(BF16), a2_ref[...]))
    g_ref[...] = _dot(jax.nn.sigmoid(_dot(xg, g1_ref[...])).astype(BF16), g2_ref[...])


def _rwkv_lora(u, prev, mu, w1, w2, a1, a2, g1, g2, w0, a0):
    m = u.shape[0]
    row = pl.BlockSpec((RWKV_LORA_TM, D_MODEL), lambda i: (i, 0))
    full = lambda a: pl.BlockSpec(a.shape, lambda i: (0, 0))
    vec = pl.BlockSpec((1, D_MODEL), lambda i: (0, 0))
    return pl.pallas_call(
        _rwkv_lora_kernel,
        grid=(m // RWKV_LORA_TM,),
        in_specs=[row, row, pl.BlockSpec((6, D_MODEL), lambda i: (0, 0)),
                  full(w1), full(w2), full(a1), full(a2), full(g1), full(g2), vec, vec],
        out_specs=[row, row, row],
        out_shape=[jax.ShapeDtypeStruct((m, D_MODEL), F32)] * 3,
        compiler_params=_cparams("parallel"),
        name="rwkv_lora",
    )(u, prev, mu, w1, w2, a1, a2, g1, g2, w0.reshape(1, D_MODEL), a0.reshape(1, D_MODEL))


def _block_ones():
    r = lax.broadcasted_iota(jnp.int32, (LANES, LANES), 0) // RWKV_HEAD_DIM
    c = lax.broadcasted_iota(jnp.int32, (LANES, LANES), 1) // RWKV_HEAD_DIM
    return (r == c).astype(BF16)


def _head_sum(x, bo):
    hi, lo = _split_bf16(x)
    return _dot(hi, bo) + _dot(lo, bo)


RWKV_PG = 8
RWKV_NPG = RWKV_PAIRS // RWKV_PG
RWKV_PG_W = RWKV_PG * LANES


class _RwkvConsts:
    def __init__(self):
        self.bo = _block_ones()
        lane = lax.broadcasted_iota(jnp.int32, (RWKV_HEAD_DIM, LANES), 1)
        row = lax.broadcasted_iota(jnp.int32, (RWKV_HEAD_DIM, LANES), 0)
        self.diag = ((lane % RWKV_HEAD_DIM) == row).astype(BF16)
        self.head_a = lane < RWKV_HEAD_DIM
        self.row8 = lax.broadcasted_iota(jnp.int32, (SUBLANES, LANES), 0)


def _rwkv_step(s2, s2b, kk_t, dec_t, kka_t, k_t, v_t, r_lhs, cst):
    bo = cst.bo
    sa = _dot_nt(s2b, bo * (-kk_t).astype(BF16))
    v_hi, v_lo = _split_bf16(v_t)
    v2 = _dot_nt(cst.diag, bo * v_hi) + _dot_nt(cst.diag, bo * v_lo)
    s2n = s2 * dec_t + sa * kka_t + v2 * k_t
    s2nb = s2n.astype(BF16)
    zero = jnp.zeros_like(s2nb)
    wy = jnp.concatenate([jnp.where(cst.head_a, s2nb, zero), jnp.where(cst.head_a, zero, s2nb)], axis=0)
    return s2n, s2nb, _dot_nt(r_lhs, wy)


def _rwkv_steps(s2, tiles, nsteps, cst, store):
    kk8, dec8, kka8, k8, v8, r8 = tiles
    s2b = s2.astype(BF16)
    y = jnp.zeros((SUBLANES, LANES), F32)
    for i in range(nsteps):
        r_lhs = jnp.where(cst.row8 == i, r8, 0.0).astype(BF16)
        s2, s2b, yi = _rwkv_step(s2, s2b, kk8[i:i + 1], dec8[i:i + 1], kka8[i:i + 1], k8[i:i + 1],
                                 v8[i:i + 1], r_lhs, cst)
        y = y + yi
        yield
    store(s2, y)


def _rwkv_prepare(k_ref, a_ref, v_ref, valid, kk_p, ka_p, kk_s, kka_s, km_s, vm_s, bo):
    for p in range(RWKV_PG):
        sl = slice(LANES * p, LANES * (p + 1))
        k, a = k_ref[:, sl], a_ref[:, sl]
        kkr = k * kk_p[:, sl]
        kk = kkr / jnp.maximum(jnp.sqrt(_head_sum(kkr * kkr, bo)), 1e-12)
        kk_s[:, sl] = kk
        kka_s[:, sl] = kk * a
        km_s[:, sl] = k * (1.0 + (a - 1.0) * ka_p[:, sl])
        vm_s[:, sl] = v_ref[:, sl] * valid


def _rwkv_finish(yraw_s, r_ref, km_s, vm_s, g_ref, rk_p, lnw_p, lnb_p, o_ref, bo):
    inv = 1.0 / RWKV_HEAD_DIM
    for p in range(RWKV_PG):
        sl = slice(LANES * p, LANES * (p + 1))
        y = yraw_s[:, sl]
        d = y - _head_sum(y, bo) * inv
        yn = d * lax.rsqrt(_head_sum(d * d, bo) * inv + RWKV_LN_EPS) * lnw_p[:, sl] + lnb_p[:, sl]
        bonus = _head_sum(r_ref[:, sl] * km_s[:, sl] * rk_p[:, sl], bo) * vm_s[:, sl]
        o_ref[:, sl] = ((yn + bonus) * g_ref[:, sl]).astype(o_ref.dtype)


def _rwkv_tiles(base, p, refs):
    return [ref[pl.ds(base, SUBLANES), pl.ds(LANES * p, LANES)] for ref in refs]


RWKV_C = 64
RWKV_PROMPT_CHUNKS = (ROW_BLK + SEQ) // RWKV_C
RWKV_INV_LEVELS = 6


def _prompt_blk64(b, c):
    per = ROW_BLK // RWKV_C
    return jnp.where(c < per, per * (META_BLK0 + b) + c,
                     per * (MAIN_BLK0 + MAIN_BLKS_PER_B * b) + c - per)


def _pair_bd(x, head_a):
    zero = jnp.zeros_like(x)
    return jnp.concatenate([jnp.where(head_a, x, zero), jnp.where(head_a, zero, x)], axis=0)


def _rwkv_chunk_kernel(r_ref, k_ref, v_ref, lw_ref, a_ref, g_ref, valid_ref, kk_p, ka_p, rk_p,
                       lnw_p, lnb_p, s0_ref, o_ref, sl_ref, s_ref, yraw_s, *, rows, nsb, chained):
    c = pl.program_id(2)
    cl = rows
    cp = RWKV_C
    levels = max(1, (cl - 1).bit_length())

    if chained:
        @pl.when(c == 0)
        def _():
            s_ref[...] = s0_ref[0]

    bo = _block_ones()
    lane = lax.broadcasted_iota(jnp.int32, (cl, LANES), 1)
    row = lax.broadcasted_iota(jnp.int32, (cl, LANES), 0)
    head_a = lane < RWKV_HEAD_DIM
    head_a_pad = lax.broadcasted_iota(jnp.int32, (cp, LANES), 1) < RWKV_HEAD_DIM
    s_idx = lane % RWKV_HEAD_DIM
    strict = s_idx < row
    incl = s_idx <= row
    eye2 = (s_idx == row).astype(F32)
    tri = (lax.broadcasted_iota(jnp.int32, (cl, cp), 0)
           >= lax.broadcasted_iota(jnp.int32, (cl, cp), 1)).astype(BF16)
    b16 = lambda x: x.astype(BF16)
    cat = lambda *xs: jnp.concatenate(xs, axis=0)
    halves = lambda x: x[:cl] + x[cl:]

    def pad_rows(x, n):
        if x.shape[0] == n:
            return x
        return cat(x, jnp.zeros((n - x.shape[0], x.shape[1]), x.dtype))

    def split(x):
        hi = b16(x).astype(F32)
        return hi, x - hi

    bd = lambda x: _pair_bd(b16(pad_rows(x, cp)), head_a_pad)
    lhs = lambda *xs: b16(pad_rows(cat(*xs), -(-len(xs) * cl // 16) * 16))

    def pair_stages(si, p):
        sl = slice(LANES * p, LANES * (p + 1))
        rsl = slice(si * cl, (si + 1) * cl)
        valid = valid_ref[rsl]
        k, a, r = k_ref[rsl, sl], a_ref[rsl, sl], r_ref[rsl, sl]
        vm = v_ref[rsl, sl] * valid
        lw = lw_ref[rsl, sl] * valid
        kkr = k * kk_p[:, sl]
        kk = kkr / jnp.maximum(jnp.sqrt(_head_sum(kkr * kkr, bo)), 1e-12) * valid
        bv = kk * a
        km = k * (1.0 + (a - 1.0) * ka_p[:, sl])
        lw_hi, lw_lo = split(lw)
        g = _dot(tri, b16(pad_rows(lw_hi, cp))) + _dot(tri, b16(pad_rows(lw_lo, cp)))
        yield
        g_end = g[cl - 1:cl]
        e_neg = jnp.exp(-g)
        e_end = jnp.exp(g_end - g)
        at = -kk * jnp.exp(g - lw)
        rt = r * jnp.exp(g)
        x2 = lhs(at, rt)
        gb = _dot_nt(x2, bd(bv * e_neg))
        gk = _dot_nt(x2, bd(km * e_neg))
        yield
        n = jnp.where(strict, gb[:cl], 0.0)
        mrb = jnp.where(incl, gb[cl:2 * cl], 0.0)
        lak = jnp.where(strict, gk[:cl], 0.0)
        mrk = jnp.where(incl, gk[cl:2 * cl], 0.0)
        pk, t = n, eye2
        for level in range(levels):
            if level < levels - 1:
                res = _dot(lhs(pk, t), bd(pk))
                pk, t = res[:cl], t + res[cl:2 * cl]
            else:
                t = t + _dot(lhs(t), bd(pk))[:cl]
            yield
        n_hi, n_lo = split(n)
        t_hi, t_lo = split(t)
        nt = _dot(lhs(n_hi, n_lo), bd(t_hi))
        nt = nt[:cl] + nt[cl:2 * cl] + _dot(lhs(n_hi), bd(t_lo))[:cl]
        wy = _dot(lhs(lak, mrk), bd(vm))
        yield
        t = t + _dot(lhs(t_hi), bd(eye2 - t + nt))[:cl]
        yield
        tt = lhs(*split(t))
        pmat = _dot(tt, bd(at))
        pmat = pmat[:cl] + pmat[cl:2 * cl]
        q = _dot(tt, bd(wy[:cl]))
        q = q[:cl] + q[cl:2 * cl]
        yield
        s2 = s_ref[p] if chained else s0_ref[si, p]
        res = _dot_nt(lhs(*split(pmat), rt), _pair_bd(b16(s2), head_a_pad))
        yield
        u = res[:cl] + res[cl:2 * cl] + q
        y = res[2 * cl:3 * cl] + _dot(lhs(mrb), bd(u))[:cl] + wy[cl:2 * cl]
        v_hi, v_lo = split(vm)
        kg_hi, kg_lo = split(km * e_end)
        full = _dot_tn(lhs(u, v_hi, v_lo, v_hi), lhs(bv * e_end, kg_hi, kg_hi, kg_lo))
        yield
        s_new = s2 * jnp.exp(g_end) + jnp.where(head_a_pad, full[:RWKV_HEAD_DIM], full[RWKV_HEAD_DIM:])
        if chained:
            s_ref[p] = s_new
        else:
            sl_ref[si, p] = s_new
        d = y - _head_sum(y, bo) * inv
        yield
        yn = d * lax.rsqrt(_head_sum(d * d, bo) * inv + RWKV_LN_EPS) * lnw_p[:, sl] + lnb_p[:, sl]
        bonus = _head_sum(r * km * rk_p[:, sl], bo) * vm
        yraw_s[rsl, sl] = (yn + bonus) * g_ref[rsl, sl]

    inv = 1.0 / RWKV_HEAD_DIM
    live = [pair_stages(si, p) for si in range(nsb) for p in range(RWKV_PG)]
    while live:
        live = [gen for gen in live if next(gen, True) is None]
    o_ref[...] = yraw_s[...].astype(o_ref.dtype)

    if chained:
        @pl.when(c == RWKV_PROMPT_CHUNKS - 1)
        def _():
            sl_ref[0] = s_ref[...]


RWKV_SAMPLE_SEQS = 4
RWKV_SAMPLE_ROWS = RWKV_SAMPLE_SEQS * SAMPLE_ROWS_PER_SEQ


def _rwkv_sample_kernel(r_ref, k_ref, v_ref, lw_ref, a_ref, g_ref, valid_ref, kk_p, ka_p, rk_p,
                        lnw_p, lnb_p, s0_ref, o_ref, sl_ref, kk_s, kka_s, km_s, vm_s, yraw_s):
    cst = _RwkvConsts()
    _rwkv_prepare(k_ref, a_ref, v_ref, valid_ref[...], kk_p, ka_p, kk_s, kka_s, km_s, vm_s, cst.bo)

    def body(i, carry):
        base = pl.multiple_of(i * SAMPLE_ROWS_PER_SEQ, SAMPLE_ROWS_PER_SEQ)

        def make(p):
            def store(s2, y):
                sl_ref[i, p] = s2
                yraw_s[pl.ds(base, SUBLANES), pl.ds(LANES * p, LANES)] = y
            tiles = _rwkv_tiles(base, p, (kk_s, lw_ref, kka_s, km_s, vm_s, r_ref))
            tiles[1] = jnp.exp(tiles[1])
            return _rwkv_steps(s0_ref[i, p], tiles, DEC_SEQ, cst, store)

        live = [make(p) for p in range(RWKV_PG)]
        while live:
            live = [gen for gen in live if next(gen, True) is None]
        return carry

    lax.fori_loop(0, RWKV_SAMPLE_SEQS, body, 0)
    _rwkv_finish(yraw_s, r_ref, km_s, vm_s, g_ref, rk_p, lnw_p, lnb_p, o_ref, cst.bo)


def _rwkv_core(r, k, v, dec, a, g, valid, k_k, k_a, r_k, ln_w, ln_b, s0, *, prompt, y_init=None):
    nseq = s0.shape[0]
    vec = lambda a_: a_.reshape(1, D_MODEL)
    if prompt:
        rows = RWKV_C
        grid = (nseq, RWKV_NPG, RWKV_PROMPT_CHUNKS)
        rmap = lambda b, q, c: (_prompt_blk64(b, c), q)
        vmap = lambda b, q, c: (_prompt_blk64(b, c), 0)
        spb = 1
        kernel_fn = functools.partial(_rwkv_chunk_kernel, rows=RWKV_C, nsb=1, chained=True)
    else:
        rows = RWKV_SAMPLE_ROWS
        grid = (nseq // RWKV_SAMPLE_SEQS, RWKV_NPG, 1)
        rmap = lambda b, q, c: (b, q)
        vmap = lambda b, q, c: (b, 0)
        spb = RWKV_SAMPLE_SEQS
        kernel_fn = functools.partial(_rwkv_chunk_kernel, rows=SAMPLE_ROWS_PER_SEQ, nsb=spb,
                                      chained=False)
    scratch = [pltpu.VMEM((RWKV_PG, RWKV_HEAD_DIM, LANES), F32), pltpu.VMEM((rows, RWKV_PG_W), F32)]
    sem = ("parallel", "parallel", "arbitrary")
    row_spec = pl.BlockSpec((rows, RWKV_PG_W), rmap)
    vec_spec = pl.BlockSpec((1, RWKV_PG_W), lambda b, q, c: (0, q))
    st_spec = pl.BlockSpec((spb, RWKV_PG, RWKV_HEAD_DIM, LANES), lambda b, q, c: (b, q, 0, 0))
    in_specs = [row_spec] * 6 + [pl.BlockSpec((rows, 1), vmap)] + [vec_spec] * 5 + [st_spec]
    args = [r, k, v, dec, a, g, valid, vec(k_k), vec(k_a), vec(r_k), vec(ln_w), vec(ln_b), s0]
    aliases = {}
    if y_init is not None:
        in_specs.append(pl.BlockSpec(memory_space=pl.ANY))
        args.append(y_init)
        aliases = {len(args) - 1: 0}
        kernel_fn = _drop_ref(kernel_fn, len(args) - 1)
    return pl.pallas_call(
        kernel_fn,
        grid=grid,
        in_specs=in_specs,
        out_specs=[row_spec, st_spec],
        input_output_aliases=aliases,
        out_shape=[jax.ShapeDtypeStruct((T_ROWS, D_MODEL), BF16), jax.ShapeDtypeStruct(s0.shape, F32)],
        scratch_shapes=scratch,
        compiler_params=_cparams(*sem),
        name="rwkv_core_prompt" if prompt else "rwkv_core_sample",
    )(*args)


def _row_ids():
    sample_t0 = jnp.arange(DEC_BATCH) * SAMPLE_ROWS_PER_SEQ
    main0 = MAIN_BLK0 * ROW_BLK + jnp.arange(BATCH) * SEQ
    meta0 = META_BLK0 * ROW_BLK + jnp.arange(BATCH) * ROW_BLK + META_PAD
    return sample_t0, main0, meta0


def _valid_mask():
    r = jnp.arange(T_ROWS)
    sample = (r < MAIN_BLK0 * ROW_BLK) & (r % SAMPLE_ROWS_PER_SEQ < DEC_SEQ)
    main = (r >= MAIN_BLK0 * ROW_BLK) & (r < META_BLK0 * ROW_BLK)
    meta = (r >= META_BLK0 * ROW_BLK) & (r % ROW_BLK >= META_PAD)
    return (sample | main | meta).astype(F32)


def _mixer_out_init(width):
    return jnp.zeros((T_ROWS, width), BF16)


def _pad_conv_prev(prev):
    return jnp.pad(prev, ((0, 0), (SUBLANES - SSD_CONV_K + 1, 0), (0, 0)))


def _ssd_layer(h, valid, norm_w, w_in, conv_w, conv_b, dt_bias, a_log, d_skip, gnorm_w, w_out,
               state_all, layer, state_conv, prev_states=None):
    sample_t0, main0, meta0 = _row_ids()
    w_in_b = w_in.astype(BF16)
    proj = _matmul(h, w_in_b, SSD_ZX_DIM, norm_w=norm_w, tn=MM_TN_WIDE)
    dtt = _ssd_dt(h, norm_w, w_in_b[:, SSD_ZX_DIM:].T, dt_bias)
    xbc = lambda rows: proj[rows][..., SSD_D_INNER:]
    tail = jnp.arange(SSD_CONV_K - 1)
    validt = valid.reshape(N_BLKS, 1, ROW_BLK)
    dtt_p = jnp.transpose(dtt.reshape(SSD_GROUPS, SSD_HPG, N_BLKS, ROW_BLK), (2, 0, 1, 3))
    gn = gnorm_w.reshape(1, SSD_D_INNER)
    cb = conv_b.reshape(1, SSD_CONV_DIM)
    h0_meta = jnp.zeros((1, BATCH, SSD_HEADS, SSD_HEAD_DIM, SSD_STATE), F32)
    prev_p = jnp.zeros((BATCH, SUBLANES, SSD_CONV_DIM), F32)
    y, p_state = _ssd_core(proj, valid.reshape(T_ROWS, 1), validt, dtt_p, a_log, d_skip, conv_w, cb,
                           prev_p, gn, h0_meta, prompt=True, y_init=_mixer_out_init(SSD_D_INNER))
    ns = SAMPLE_ROWS_PER_SEQ
    dtt_s = dtt[:, :DEC_BATCH * ns].reshape(SSD_GROUPS, SSD_HPG, DEC_BATCH, ns)
    dtt_s = jnp.pad(jnp.transpose(dtt_s, (2, 0, 1, 3)), ((0, 0), (0, 0), (0, 0), (0, SSD_Q - ns)))
    validt_s = jnp.pad(valid[:DEC_BATCH * ns].reshape(DEC_BATCH, 1, ns), ((0, 0), (0, 0), (0, SSD_Q - ns)))
    y, s_state = _ssd_core(proj, valid.reshape(T_ROWS, 1), validt_s, dtt_s, a_log, d_skip, conv_w, cb,
                           _pad_conv_prev(state_conv), gn, state_all, prompt=False, y_init=y,
                           h0_layer=layer, prev_states=prev_states)
    h = _matmul(y, w_out.astype(BF16), D_MODEL, res=h, tn=MM_TN_MID)
    p_conv = xbc(main0[:, None] + SEQ - (SSD_CONV_K - 1) + tail[None, :])
    s_conv = xbc(sample_t0[:, None] + DEC_SEQ - (SSD_CONV_K - 1) + tail[None, :])
    return h, p_state[0], p_conv, s_state, s_conv


def _s5_layer(h, valid, norm_w, w_in, lam_re, lam_im, log_dt, b_re, b_im, c_re, c_im, d_skip, w_out,
              state_re, state_im):
    v = _matmul(h, w_in.astype(BF16), D_MODEL, norm_w=norm_w, tn=MM_TN_MID)
    pow_re, pow_im, bb_re, bb_im = _s5_params(lam_re, lam_im, log_dt, b_re, b_im)
    wi_re, wo_re = _s5_block_diag(bb_re, c_re)
    wi_im, wo_im = _s5_block_diag(bb_im, c_im)
    vcol = valid.reshape(T_ROWS, 1)
    zero = jnp.zeros((BATCH, 1, S5_HDIM), F32)
    y, p_re, p_im = _s5_core(v, vcol, wi_re, wi_im, wo_re, wo_im, pow_re, pow_im, d_skip, zero, zero,
                             prompt=True, y_init=_mixer_out_init(D_MODEL))
    y, s_re, s_im = _s5_core(v, vcol, wi_re, wi_im, wo_re, wo_im, pow_re, pow_im, d_skip,
                             state_re.reshape(DEC_BATCH, 1, S5_HDIM),
                             state_im.reshape(DEC_BATCH, 1, S5_HDIM), prompt=False, y_init=y)
    h = _matmul(y, w_out.astype(BF16), D_MODEL, res=h, glu_off=D_MODEL // MM_TN_MID, tn=MM_TN_MID)
    shp = lambda a, n: a.reshape(n, S5_GROUPS, S5_STATE)
    return h, shp(p_re, BATCH), shp(p_im, BATCH), shp(s_re, DEC_BATCH), shp(s_im, DEC_BATCH)


def _to_pairs(s):
    n = s.shape[0]
    s = s.reshape(n, RWKV_PAIRS, 2, RWKV_HEAD_DIM, RWKV_HEAD_DIM)
    return jnp.transpose(s, (0, 1, 3, 2, 4)).reshape(n, RWKV_PAIRS, RWKV_HEAD_DIM, LANES)


def _from_pairs(s):
    n = s.shape[0]
    s = s.reshape(n, RWKV_PAIRS, RWKV_HEAD_DIM, 2, RWKV_HEAD_DIM)
    return jnp.transpose(s, (0, 1, 3, 2, 4)).reshape(n, RWKV_HEADS, RWKV_HEAD_DIM, RWKV_HEAD_DIM)


def _pad_lora(w_down, w_up):
    n = w_down.shape[1]
    return (jnp.pad(w_down, ((0, 0), (0, RWKV_LORA_PAD - n))).astype(BF16),
            jnp.pad(w_up, ((0, RWKV_LORA_PAD - n), (0, 0))).astype(BF16))


def _rwkv_layer(h, valid, norm_w, mu, wr, wk, wv, wo, w0, w1, w2, a0, a1, a2, g1, g2, k_k, k_a, r_k,
                ln_w, ln_b, state, state_shift):
    sample_t0, main0, meta0 = _row_ids()
    u = _rmsnorm(h, norm_w)
    prev = jnp.concatenate([jnp.zeros((1, D_MODEL), F32), u[:-1]], axis=0)
    prev = prev.at[sample_t0].set(state_shift)
    prev = prev.at[main0].set(u[meta0 + N_META - 1])
    prev = prev.at[meta0].set(0.0)
    r, k, v = _rwkv_proj(u, prev, mu, wr.astype(BF16), wk.astype(BF16), wv.astype(BF16))
    w1p, w2p = _pad_lora(w1, w2)
    a1p, a2p = _pad_lora(a1, a2)
    dec, a, g = _rwkv_lora(u, prev, mu, w1p, w2p, a1p, a2p, g1.astype(BF16), g2.astype(BF16), w0, a0)
    vcol = valid.reshape(T_ROWS, 1)
    s0_p = jnp.zeros((BATCH, RWKV_PAIRS, RWKV_HEAD_DIM, LANES), F32)
    y, p_state = _rwkv_core(r, k, v, dec, a, g, vcol, k_k, k_a, r_k, ln_w, ln_b, s0_p, prompt=True,
                            y_init=_mixer_out_init(D_MODEL))
    y, s_state = _rwkv_core(r, k, v, dec, a, g, vcol, k_k, k_a, r_k, ln_w, ln_b, _to_pairs(state),
                            prompt=False, y_init=y)
    h = _matmul(y, wo.astype(BF16), D_MODEL, res=h, tn=MM_TN_MID)
    return (h, _from_pairs(p_state), u[main0 + SEQ - 1], _from_pairs(s_state),
            u[sample_t0 + DEC_SEQ - 1])


def kernel(x_prompt, x_sample, state_ssd, state_ssd_conv, state_s5_re, state_s5_im, state_rwkv,
           state_rwkv_shift, meta_tokens, norm1_w, norm2_w, normf_w, ffn_w_gu, ffn_w_down,
           ssd_w_in, ssd_conv_w, ssd_conv_b, ssd_dt_bias, ssd_a_log, ssd_d, ssd_norm_w, ssd_w_out,
           s5_w_in, s5_lam_re, s5_lam_im, s5_log_dt, s5_b_re, s5_b_im, s5_c_re, s5_c_im, s5_d, s5_w_out,
           rwkv_mu, rwkv_wr, rwkv_wk, rwkv_wv, rwkv_wo, rwkv_w0, rwkv_w1, rwkv_w2, rwkv_a0, rwkv_a1,
           rwkv_a2, rwkv_g1, rwkv_g2, rwkv_k_k, rwkv_k_a, rwkv_r_k, rwkv_ln_w, rwkv_ln_b):
    valid = _valid_mask()
    sample = jnp.pad(x_sample, ((0, 0), (0, SAMPLE_ROWS_PER_SEQ - DEC_SEQ), (0, 0)))
    meta = jnp.pad(jnp.broadcast_to(meta_tokens[None], (BATCH, N_META, D_MODEL)),
                   ((0, 0), (META_PAD, 0), (0, 0)))
    h = jnp.concatenate([sample.reshape(-1, D_MODEL), x_prompt.reshape(-1, D_MODEL),
                         meta.reshape(-1, D_MODEL)], axis=0).astype(F32)
    outs = {name: [] for name in ("p_ssd", "p_conv", "p_s5r", "p_s5i", "p_rwkv", "p_shift",
                                  "s_conv", "s_s5r", "s_s5i", "s_rwkv", "s_shift")}
    s_ssd = None
    for i in range(DEPTH):
        kind, j = i % N_MIXERS, i // N_MIXERS
        if kind == 0:
            h, ps, pc, s_ssd, sc = _ssd_layer(
                h, valid, norm1_w[i], ssd_w_in[j], ssd_conv_w[j], ssd_conv_b[j], ssd_dt_bias[j],
                ssd_a_log[j], ssd_d[j], ssd_norm_w[j], ssd_w_out[j], state_ssd, j, state_ssd_conv[j],
                prev_states=s_ssd)
            outs["p_ssd"].append(ps); outs["p_conv"].append(pc)
            outs["s_conv"].append(sc)
        elif kind == 1:
            h, pr, pi, sr, si = _s5_layer(
                h, valid, norm1_w[i], s5_w_in[j], s5_lam_re[j], s5_lam_im[j], s5_log_dt[j], s5_b_re[j],
                s5_b_im[j], s5_c_re[j], s5_c_im[j], s5_d[j], s5_w_out[j], state_s5_re[j], state_s5_im[j])
            outs["p_s5r"].append(pr); outs["p_s5i"].append(pi)
            outs["s_s5r"].append(sr); outs["s_s5i"].append(si)
        else:
            h, ps, psh, ss, ssh = _rwkv_layer(
                h, valid, norm1_w[i], rwkv_mu[j], rwkv_wr[j], rwkv_wk[j], rwkv_wv[j], rwkv_wo[j],
                rwkv_w0[j], rwkv_w1[j], rwkv_w2[j], rwkv_a0[j], rwkv_a1[j], rwkv_a2[j], rwkv_g1[j],
                rwkv_g2[j], rwkv_k_k[j], rwkv_k_a[j], rwkv_r_k[j], rwkv_ln_w[j], rwkv_ln_b[j],
                state_rwkv[j], state_rwkv_shift[j])
            outs["p_rwkv"].append(ps); outs["p_shift"].append(psh)
            outs["s_rwkv"].append(ss); outs["s_shift"].append(ssh)
        h = _ffn(h, norm2_w[i], ffn_w_gu[i].astype(BF16), ffn_w_down[i].astype(BF16))
    yf = _rmsnorm(h, normf_w)
    y_sample = yf[:DEC_BATCH * SAMPLE_ROWS_PER_SEQ].reshape(DEC_BATCH, SAMPLE_ROWS_PER_SEQ, D_MODEL)
    y_prompt = yf[MAIN_BLK0 * ROW_BLK:META_BLK0 * ROW_BLK].reshape(BATCH, SEQ, D_MODEL)
    st = lambda name: jnp.stack(outs[name])
    return (y_prompt, y_sample[:, :DEC_SEQ],
            st("p_ssd"), st("p_conv"), st("p_s5r"), st("p_s5i"), st("p_rwkv"), st("p_shift"),
            s_ssd, st("s_conv"), st("s_s5r"), st("s_s5i"), st("s_rwkv"), st("s_shift"))
```

```python
import functools

import jax
import jax.numpy as jnp
from jax import lax
from jax.experimental import pallas as pl
from jax.experimental.pallas import tpu as pltpu

F32 = jnp.float32
BF16 = jnp.bfloat16

D_MODEL = 2048
BATCH = 4
SEQ = 2048
DEPTH = 4
DEC_BATCH = 128
DEC_SEQ = 4
N_META = 16
N_MIXERS = 3
NORM_EPS = 1e-6

SSD_D_INNER = 2 * D_MODEL
SSD_HEAD_DIM = 64
SSD_HEADS = SSD_D_INNER // SSD_HEAD_DIM
SSD_STATE = 128
SSD_GROUPS = 8
SSD_HPG = SSD_HEADS // SSD_GROUPS
SSD_GN = SSD_GROUPS * SSD_STATE
SSD_CONV_K = 4
SSD_CONV_DIM = SSD_D_INNER + 2 * SSD_GN
SSD_ZX_DIM = SSD_D_INNER + SSD_CONV_DIM
SSD_GROUP_W = SSD_D_INNER // SSD_GROUPS

S5_GROUP_SIZE = 16
S5_GROUPS = D_MODEL // S5_GROUP_SIZE
S5_STATE = 64
S5_HDIM = S5_GROUPS * S5_STATE
S5_SBLK = 1024
S5_NSB = S5_HDIM // S5_SBLK
S5_VBLK = S5_SBLK // S5_STATE * S5_GROUP_SIZE

RWKV_HEAD_DIM = 64
RWKV_HEADS = D_MODEL // RWKV_HEAD_DIM
RWKV_PAIRS = RWKV_HEADS // 2
RWKV_LORA_PAD = 128
RWKV_G_LORA = 256
RWKV_LN_EPS = 64e-5

FFN_HIDDEN = -(-8 * D_MODEL // (3 * 256)) * 256

LANES = 128
SUBLANES = 8
ROW_BLK = 128
SAMPLE_ROWS_PER_SEQ = SUBLANES
SAMPLE_BLKS = DEC_BATCH * SAMPLE_ROWS_PER_SEQ // ROW_BLK
MAIN_BLK0 = SAMPLE_BLKS
MAIN_BLKS_PER_B = SEQ // ROW_BLK
META_BLK0 = MAIN_BLK0 + BATCH * MAIN_BLKS_PER_B
N_BLKS = META_BLK0 + BATCH
T_ROWS = N_BLKS * ROW_BLK
PROMPT_CHUNKS = 1 + MAIN_BLKS_PER_B
META_PAD = ROW_BLK - N_META

MM_TM = 512
MM_TN = 512
MM_TN_MID = 1024
MM_TN_WIDE = 2048
VMEM_LIMIT = 48 * 2 ** 20


def _prompt_blk(b, c):
    return jnp.where(c == 0, META_BLK0 + b, MAIN_BLK0 + MAIN_BLKS_PER_B * b + c - 1)


def _cparams(*sem):
    return pltpu.CompilerParams(dimension_semantics=sem, vmem_limit_bytes=VMEM_LIMIT)


def _silu(x):
    return x * (0.5 + 0.5 * jnp.tanh(0.5 * x))


def _softplus(x):
    return jnp.maximum(x, 0.0) + jnp.log(1.0 + jnp.exp(-jnp.abs(x)))


def _rms(x, w):
    return x * lax.rsqrt(jnp.mean(x * x, axis=-1, keepdims=True) + NORM_EPS) * w


def _dot(a, b):
    return jnp.dot(a, b, preferred_element_type=F32)


def _dot_nt(a, b):
    return lax.dot_general(a, b, (((1,), (1,)), ((), ())), preferred_element_type=F32)


def _dot_tn(a, b):
    return lax.dot_general(a, b, (((0,), (0,)), ((), ())), preferred_element_type=F32)


def _split_bf16(x):
    hi = x.astype(BF16)
    return hi, (x - hi.astype(F32)).astype(BF16)


def _drop_ref(kernel_fn, idx):
    def wrapped(*refs):
        return kernel_fn(*refs[:idx], *refs[idx + 1:])
    return wrapped


def _mm_kernel(*refs, normalize, glu, has_res):
    it = iter(refs)
    x_ref = next(it)
    nw_ref = next(it) if normalize else None
    w_ref = next(it)
    w2_ref = next(it) if glu else None
    res_ref = next(it) if has_res else None
    o_ref = next(it)
    xn_ref = next(it) if normalize else None
    if normalize:
        @pl.when(pl.program_id(1) == 0)
        def _():
            xn_ref[...] = _rms(x_ref[...], nw_ref[...]).astype(BF16)
        xb = xn_ref[...]
    else:
        xb = x_ref[...].astype(BF16)
    acc = _dot(xb, w_ref[...])
    if glu:
        acc = acc * jax.nn.sigmoid(_dot(xb, w2_ref[...]))
    if has_res:
        acc = acc + res_ref[...]
    o_ref[...] = acc.astype(o_ref.dtype)


def _matmul(x, w, n_out, *, norm_w=None, res=None, glu_off=None, out_dtype=F32, tn=MM_TN):
    m, k = x.shape
    normalize = norm_w is not None
    glu = glu_off is not None
    has_res = res is not None
    in_specs = [pl.BlockSpec((MM_TM, k), lambda i, j: (i, 0))]
    args = [x]
    if normalize:
        in_specs.append(pl.BlockSpec((1, k), lambda i, j: (0, 0)))
        args.append(norm_w.reshape(1, k))
    in_specs.append(pl.BlockSpec((k, tn), lambda i, j: (0, j)))
    args.append(w)
    if glu:
        in_specs.append(pl.BlockSpec((k, tn), lambda i, j: (0, j + glu_off)))
        args.append(w)
    if has_res:
        in_specs.append(pl.BlockSpec((MM_TM, tn), lambda i, j: (i, j)))
        args.append(res)
    return pl.pallas_call(
        functools.partial(_mm_kernel, normalize=normalize, glu=glu, has_res=has_res),
        grid=(m // MM_TM, n_out // tn),
        in_specs=in_specs,
        out_specs=pl.BlockSpec((MM_TM, tn), lambda i, j: (i, j)),
        out_shape=jax.ShapeDtypeStruct((m, n_out), out_dtype),
        scratch_shapes=[pltpu.VMEM((MM_TM, k), BF16)] if normalize else [],
        compiler_params=_cparams("parallel", "arbitrary"),
        name="matmul_n%d%s%s%s" % (n_out, "_norm" * normalize, "_glu" * glu, "_res" * has_res),
    )(*args)


def _rmsnorm_kernel(x_ref, w_ref, o_ref):
    o_ref[...] = _rms(x_ref[...], w_ref[...])


def _rmsnorm(x, w):
    m, k = x.shape
    return pl.pallas_call(
        _rmsnorm_kernel,
        grid=(m // MM_TM,),
        in_specs=[pl.BlockSpec((MM_TM, k), lambda i: (i, 0)), pl.BlockSpec((1, k), lambda i: (0, 0))],
        out_specs=pl.BlockSpec((MM_TM, k), lambda i: (i, 0)),
        out_shape=jax.ShapeDtypeStruct((m, k), F32),
        compiler_params=_cparams("parallel"),
        name="rmsnorm",
    )(x, w.reshape(1, k))


FFN_TF = 512
FFN_NF = FFN_HIDDEN // FFN_TF


def _ffn_kernel(x_ref, nw_ref, wg_ref, wu_ref, wd_ref, o_ref, xn_ref):
    f = pl.program_id(1)

    @pl.when(f == 0)
    def _():
        x = x_ref[...]
        xn_ref[...] = _rms(x, nw_ref[...]).astype(BF16)
        o_ref[...] = x

    xb = xn_ref[...]
    act = _silu(_dot(xb, wg_ref[...])) * _dot(xb, wu_ref[...])
    o_ref[...] += _dot(act.astype(BF16), wd_ref[...])


def _ffn(h, norm_w, w_gu, w_down):
    m = h.shape[0]
    return pl.pallas_call(
        _ffn_kernel,
        grid=(m // MM_TM, FFN_NF),
        in_specs=[
            pl.BlockSpec((MM_TM, D_MODEL), lambda i, f: (i, 0)),
            pl.BlockSpec((1, D_MODEL), lambda i, f: (0, 0)),
            pl.BlockSpec((D_MODEL, FFN_TF), lambda i, f: (0, f)),
            pl.BlockSpec((D_MODEL, FFN_TF), lambda i, f: (0, f + FFN_NF)),
            pl.BlockSpec((FFN_TF, D_MODEL), lambda i, f: (f, 0)),
        ],
        out_specs=pl.BlockSpec((MM_TM, D_MODEL), lambda i, f: (i, 0)),
        out_shape=jax.ShapeDtypeStruct((m, D_MODEL), F32),
        scratch_shapes=[pltpu.VMEM((MM_TM, D_MODEL), BF16)],
        compiler_params=_cparams("parallel", "arbitrary"),
        name="ffn",
    )(h, norm_w.reshape(1, D_MODEL), w_gu, w_gu, w_down)


def _dt_kernel(x_ref, nw_ref, wt_ref, bias_ref, o_ref):
    xb = _rms(x_ref[...], nw_ref[...]).astype(BF16)
    o_ref[...] = _softplus(_dot_nt(wt_ref[...], xb) + bias_ref[...])


def _ssd_dt(h, norm_w, w_dt_t, dt_bias):
    m = h.shape[0]
    return pl.pallas_call(
        _dt_kernel,
        grid=(m // MM_TM,),
        in_specs=[
            pl.BlockSpec((MM_TM, D_MODEL), lambda i: (i, 0)),
            pl.BlockSpec((1, D_MODEL), lambda i: (0, 0)),
            pl.BlockSpec((SSD_HEADS, D_MODEL), lambda i: (0, 0)),
            pl.BlockSpec((SSD_HEADS, 1), lambda i: (0, 0)),
        ],
        out_specs=pl.BlockSpec((SSD_HEADS, MM_TM), lambda i: (0, i)),
        out_shape=jax.ShapeDtypeStruct((SSD_HEADS, m), F32),
        compiler_params=_cparams("parallel"),
        name="ssd_dt",
    )(h, norm_w.reshape(1, D_MODEL), w_dt_t, dt_bias.reshape(SSD_HEADS, 1))


SSD_Q = ROW_BLK
SSD_SAMPLE_SEQS = 4
SSD_MIN_COLS = 16


def _ssd_kernel(z_ref, x_ref, b_ref, c_ref, valid_ref, validt_ref, dtt_ref, alog_ref, dskip_ref,
                cwx_ref, cwb_ref, cwc_ref, cbx_ref, cbb_ref, cbc_ref, cpx_ref, cpb_ref, cpc_ref,
                nw_ref, h0_ref, *rest, rows, nch, nsb, n_prev):
    c = pl.program_id(2)
    q = max(rows, SSD_MIN_COLS)
    chained = nch > 1
    if n_prev:
        pst_ref, y_ref, hl_all_ref, st_ref, xpx_ref, xpb_ref, xpc_ref = rest
        hl_all_ref[0:n_prev] = pst_ref[...]
        hl_ref = hl_all_ref.at[n_prev]
    else:
        y_ref, hl_all_ref, st_ref, xpx_ref, xpb_ref, xpc_ref = rest
        hl_ref = hl_all_ref.at[0]

    if chained:
        @pl.when(c == 0)
        def _():
            st_ref[...] = h0_ref[0]
            xpx_ref[0, 0:SUBLANES] = cpx_ref[0]
            xpb_ref[0, 0:SUBLANES] = cpb_ref[0]
            xpc_ref[0, 0:SUBLANES] = cpc_ref[0]

    def pad_rows(x):
        if rows == q:
            return x
        return jnp.concatenate([x, jnp.zeros((q - rows, x.shape[1]), x.dtype)], axis=0)

    def conv(si, raw, xp_ref, cp_ref, w_ref, bias_ref):
        if not chained:
            xp_ref[si, 0:SUBLANES] = cp_ref[si]
        xp_ref[si, SUBLANES:SUBLANES + rows] = raw
        acc = bias_ref[...]
        for k in range(SSD_CONV_K):
            acc = acc + w_ref[k:k + 1, :] * xp_ref[si, pl.ds(SUBLANES - SSD_CONV_K + 1 + k, rows), :]
        if chained:
            xp_ref[si, 0:SUBLANES] = raw[rows - SUBLANES:rows]
        return _silu(acc)

    a_col = -jnp.exp(alog_ref[0])
    lane = lax.broadcasted_iota(jnp.int32, (SSD_HPG, SSD_Q), 1)
    row_i = lax.broadcasted_iota(jnp.int32, (rows, q), 0)
    col_i = lax.broadcasted_iota(jnp.int32, (rows, q), 1)
    eye = row_i == col_i
    causal = row_i >= col_i
    head_a = lax.broadcasted_iota(jnp.int32, (1, LANES), 1) < SSD_HEAD_DIM
    row_a = lax.broadcasted_iota(jnp.int32, (2 * SSD_HEAD_DIM, 1), 0) < SSD_HEAD_DIM
    dskip = dskip_ref[0]

    def to_col(rowvec):
        return jnp.sum(jnp.where(eye, rowvec, 0.0), axis=1, keepdims=True)

    def pair_stages(si, p, xs, bm_pad, cm, cb, cum, dtt, ys):
        xh = xs[:, LANES * p:LANES * (p + 1)]
        dt_cols = [to_col(dtt[r:r + 1, :]) for r in (2 * p, 2 * p + 1)]
        xdt = xh * jnp.where(head_a, dt_cols[0], dt_cols[1])
        xhb = pad_rows(xdt).astype(BF16)
        if chained:
            st = st_ref[2 * p:2 * p + 2]
        else:
            st = h0_ref[si, 2 * p:2 * p + 2]
        st = st.reshape(2 * SSD_HEAD_DIM, SSD_STATE)
        yo = _dot_nt(cm, st.astype(BF16))
        yd, e_col, w_col, e_last = [], [], [], []
        for r in (2 * p, 2 * p + 1):
            cum_row = cum[r:r + 1, :]
            cum_col = to_col(cum_row)
            lmat = jnp.exp(jnp.where(causal, cum_col - cum_row, -jnp.inf))
            yd.append(_dot((cb * lmat).astype(BF16), xhb))
            c_last = cum_row[:, q - 1:q]
            e_col.append(jnp.exp(cum_col))
            w_col.append(jnp.exp(c_last - cum_col))
            e_last.append(jnp.exp(c_last))
        yield
        d2 = jnp.where(head_a, dskip[:, 2 * p:2 * p + 1], dskip[:, 2 * p + 1:2 * p + 2])
        ys[p] = (jnp.where(head_a, yd[0], yd[1]) + yo * jnp.where(head_a, e_col[0], e_col[1]) + xh * d2)
        xw = pad_rows(xdt * jnp.where(head_a, w_col[0], w_col[1])).astype(BF16)
        st_new = st * jnp.where(row_a, e_last[0], e_last[1]) + _dot_tn(xw, bm_pad)
        yield
        st_new = st_new.reshape(2, SSD_HEAD_DIM, SSD_STATE)
        if chained:
            st_ref[2 * p:2 * p + 2] = st_new
        else:
            hl_ref[si, 2 * p:2 * p + 2] = st_new

    live, outs = [], []
    for si in range(nsb):
        rsl = slice(si * rows, (si + 1) * rows)
        valid = valid_ref[rsl]
        xs = conv(si, x_ref[rsl] * valid, xpx_ref, cpx_ref, cwx_ref, cbx_ref)
        bm = conv(si, b_ref[rsl] * valid, xpb_ref, cpb_ref, cwb_ref, cbb_ref)
        cm = conv(si, c_ref[rsl] * valid, xpc_ref, cpc_ref, cwc_ref, cbc_ref).astype(BF16)
        bm_pad = pad_rows(bm).astype(BF16)
        dtt = dtt_ref[si, 0] * validt_ref[si]
        cum = dtt * a_col
        s = 1
        while s < q:
            cum = cum + jnp.where(lane >= s, pltpu.roll(cum, s, axis=1), 0.0)
            s *= 2
        dtt, cum = dtt[:, :q], cum[:, :q]
        cb = _dot_nt(cm, bm_pad)
        ys = [None] * (SSD_HPG // 2)
        outs.append((rsl, ys))
        live += [pair_stages(si, p, xs, bm_pad, cm, cb, cum, dtt, ys) for p in range(SSD_HPG // 2)]
    while live:
        live = [gen for gen in live if next(gen, True) is None]
    y = jnp.concatenate([jnp.concatenate(ys, axis=1) for _, ys in outs], axis=0)
    y = y * _silu(z_ref[...])
    y_ref[...] = _rms(y, nw_ref[...]).astype(y_ref.dtype)

    if chained:
        @pl.when(c == nch - 1)
        def _():
            hl_ref[0] = st_ref[...]


def _ssd_core(proj, valid, validt, dtt, a_log, d_skip, conv_w, conv_b, conv_prev8, norm_w, h0, *,
              prompt, y_init=None, h0_layer=0, prev_states=None):
    nseq = h0.shape[1]
    if prompt:
        rows, nch, nsb = ROW_BLK, PROMPT_CHUNKS, 1
        blk = _prompt_blk
    else:
        rows, nch, nsb = SAMPLE_ROWS_PER_SEQ, 1, SSD_SAMPLE_SEQS
        blk = lambda b, c: b
    gw = SSD_GROUP_W
    x_off = SSD_D_INNER // gw
    b_off = (2 * SSD_D_INNER) // SSD_STATE
    c_off = b_off + SSD_GROUPS
    cb_off = SSD_D_INNER // SSD_STATE
    cc_off = cb_off + SSD_GROUPS
    row_spec = lambda w, off: pl.BlockSpec((nsb * rows, w), lambda b, g, c: (blk(b, c), g + off))
    par_spec = lambda r, w, off: pl.BlockSpec((r, w), lambda b, g, c: (0, g + off))
    prev_spec = lambda w, off: pl.BlockSpec((nsb, SUBLANES, w), lambda b, g, c: (b, 0, g + off))
    in_specs = [
        row_spec(gw, 0), row_spec(gw, x_off), row_spec(SSD_STATE, b_off), row_spec(SSD_STATE, c_off),
        pl.BlockSpec((nsb * rows, 1), lambda b, g, c: (blk(b, c), 0)),
        pl.BlockSpec((nsb, 1, SSD_Q), lambda b, g, c: (blk(b, c), 0, 0)),
        pl.BlockSpec((nsb, 1, SSD_HPG, SSD_Q), lambda b, g, c: (blk(b, c), g, 0, 0)),
        pl.BlockSpec((1, SSD_HPG, 1), lambda b, g, c: (g, 0, 0)),
        pl.BlockSpec((1, 1, SSD_HPG), lambda b, g, c: (g, 0, 0)),
        par_spec(SSD_CONV_K, gw, 0), par_spec(SSD_CONV_K, SSD_STATE, cb_off),
        par_spec(SSD_CONV_K, SSD_STATE, cc_off),
        par_spec(1, gw, 0), par_spec(1, SSD_STATE, cb_off), par_spec(1, SSD_STATE, cc_off),
        prev_spec(gw, 0), prev_spec(SSD_STATE, cb_off), prev_spec(SSD_STATE, cc_off),
        par_spec(1, gw, 0),
        pl.BlockSpec((None, nsb, SSD_HPG, SSD_HEAD_DIM, SSD_STATE),
                     lambda b, g, c: (h0_layer, b, g, 0, 0)),
    ]
    args = [proj, proj, proj, proj, valid, validt, dtt,
            a_log.reshape(SSD_GROUPS, SSD_HPG, 1), d_skip.reshape(SSD_GROUPS, 1, SSD_HPG),
            conv_w, conv_w, conv_w, conv_b, conv_b, conv_b, conv_prev8, conv_prev8, conv_prev8,
            norm_w, h0]
    n_prev = 0 if prev_states is None else prev_states.shape[0]
    all_spec = lambda n: pl.BlockSpec((n, nsb, SSD_HPG, SSD_HEAD_DIM, SSD_STATE),
                                      lambda b, g, c: (0, b, g, 0, 0))
    if n_prev:
        in_specs.append(all_spec(n_prev))
        args.append(prev_states)
    kernel_fn = functools.partial(_ssd_kernel, rows=rows, nch=nch, nsb=nsb, n_prev=n_prev)
    aliases = {}
    if y_init is not None:
        in_specs.append(pl.BlockSpec(memory_space=pl.ANY))
        args.append(y_init)
        aliases = {len(args) - 1: 0}
        kernel_fn = _drop_ref(kernel_fn, len(args) - 1)
    return pl.pallas_call(
        kernel_fn,
        grid=(nseq // nsb, SSD_GROUPS, nch),
        in_specs=in_specs,
        out_specs=[row_spec(gw, 0), all_spec(n_prev + 1)],
        input_output_aliases=aliases,
        out_shape=[jax.ShapeDtypeStruct((T_ROWS, SSD_D_INNER), BF16),
                   jax.ShapeDtypeStruct((n_prev + 1,) + h0.shape[1:], F32)],
        scratch_shapes=[
            pltpu.VMEM((SSD_HPG, SSD_HEAD_DIM, SSD_STATE), F32),
            pltpu.VMEM((nsb, SUBLANES + rows, gw), F32),
            pltpu.VMEM((nsb, SUBLANES + rows, SSD_STATE), F32),
            pltpu.VMEM((nsb, SUBLANES + rows, SSD_STATE), F32),
        ],
        compiler_params=_cparams("parallel", "parallel", "arbitrary"),
        name="ssd_core_prompt" if prompt else "ssd_core_sample",
    )(*args)


def _s5_abar(lam_re, lam_im, log_dt):
    dt = jnp.exp(log_dt)
    mag = jnp.exp(lam_re * dt)
    ang = lam_im * dt
    return mag * jnp.cos(ang), mag * jnp.sin(ang)


def _s5_pow_kernel(lr_ref, li_ref, ldt_ref, pr_ref, pi_ref):
    ar, ai = _s5_abar(lr_ref[...], li_ref[...], ldt_ref[...])
    row = lax.broadcasted_iota(jnp.int32, (SUBLANES, S5_HDIM), 0)
    pr, pi = ar, ai
    out_r = jnp.broadcast_to(ar, (SUBLANES, S5_HDIM))
    out_i = jnp.broadcast_to(ai, (SUBLANES, S5_HDIM))
    for k in range(1, SUBLANES):
        pr, pi = pr * ar - pi * ai, pr * ai + pi * ar
        out_r = jnp.where(row == k, pr, out_r)
        out_i = jnp.where(row == k, pi, out_i)
    pr_ref[...] = out_r
    pi_ref[...] = out_i


def _s5_bbar_kernel(lr_ref, li_ref, ldt_ref, br_ref, bi_ref, or_ref, oi_ref):
    lr, li = lr_ref[...], li_ref[...]
    ar, ai = _s5_abar(lr, li, ldt_ref[...])
    den = lr * lr + li * li
    f_re = ((ar - 1.0) * lr + ai * li) / den
    f_im = (ai * lr - (ar - 1.0) * li) / den
    br, bi = br_ref[...], bi_ref[...]
    or_ref[...] = f_re * br - f_im * bi
    oi_ref[...] = f_re * bi + f_im * br


def _s5_params(lam_re, lam_im, log_dt, b_re, b_im):
    ldt = jnp.repeat(log_dt, S5_STATE)
    row = lambda a: a.reshape(1, S5_HDIM)
    col = lambda a: a.reshape(S5_HDIM, 1)
    full = lambda shape: pl.BlockSpec(shape, lambda: (0,) * len(shape))
    pow_re, pow_im = pl.pallas_call(
        _s5_pow_kernel,
        in_specs=[full((1, S5_HDIM))] * 3,
        out_specs=[full((SUBLANES, S5_HDIM))] * 2,
        out_shape=[jax.ShapeDtypeStruct((SUBLANES, S5_HDIM), F32)] * 2,
    )(row(lam_re), row(lam_im), row(ldt))
    bb_re, bb_im = pl.pallas_call(
        _s5_bbar_kernel,
        in_specs=[full((S5_HDIM, 1))] * 3 + [full((S5_HDIM, S5_GROUP_SIZE))] * 2,
        out_specs=[full((S5_HDIM, S5_GROUP_SIZE))] * 2,
        out_shape=[jax.ShapeDtypeStruct((S5_HDIM, S5_GROUP_SIZE), F32)] * 2,
    )(col(lam_re), col(lam_im), col(ldt), b_re.reshape(S5_HDIM, S5_GROUP_SIZE),
      b_im.reshape(S5_HDIM, S5_GROUP_SIZE))
    return pow_re, pow_im, bb_re, bb_im


def _s5_block_diag(bb, c):
    gpb = S5_SBLK // S5_STATE
    eye = jnp.eye(gpb, dtype=F32)
    bb = bb.reshape(S5_NSB, gpb, S5_STATE, S5_GROUP_SIZE)
    w_in = eye[None, :, None, :, None] * jnp.transpose(bb, (0, 3, 1, 2))[:, None]
    w_in = w_in.reshape(S5_NSB, S5_VBLK, S5_SBLK)
    c = c.reshape(S5_NSB, gpb, S5_GROUP_SIZE, S5_STATE)
    w_out = eye[None, :, None, :, None] * jnp.transpose(c, (0, 1, 3, 2))[:, :, :, None, :]
    w_out = w_out.reshape(S5_NSB, S5_SBLK, S5_VBLK)
    return w_in.astype(BF16), w_out.astype(BF16)


def _s5_kernel(v_ref, valid_ref, wir_ref, wii_ref, wor_ref, woi_ref, pr_ref, pi_ref, d_ref,
               h0r_ref, h0i_ref, y_ref, hlr_ref, hli_ref, cr_ref, ci_ref, hr_ref, hi_ref, *,
               per_tile, nch, last_row):
    ntiles = ROW_BLK // SUBLANES
    if not per_tile:
        c = pl.program_id(2)

        @pl.when(c == 0)
        def _():
            cr_ref[...] = h0r_ref[0]
            ci_ref[...] = h0i_ref[0]

    v = v_ref[...]
    vb = (v * valid_ref[...]).astype(BF16)
    bu_r = _dot(vb, wir_ref[0])
    bu_i = _dot(vb, wii_ref[0])
    pr, pi = pr_ref[...], pi_ref[...]
    row = lax.broadcasted_iota(jnp.int32, (SUBLANES, S5_SBLK), 0)
    levels = []
    for s in (1, 2, 4):
        levels.append((s, jnp.where(row >= s, pr[s - 1:s], 0.0), jnp.where(row >= s, pi[s - 1:s], 0.0)))
    if not per_tile:
        car_r, car_i = cr_ref[...], ci_ref[...]
    for i in range(ntiles):
        xr = bu_r[SUBLANES * i:SUBLANES * (i + 1)]
        xi = bu_i[SUBLANES * i:SUBLANES * (i + 1)]
        for s, ar, ai in levels:
            sr = pltpu.roll(xr, s, axis=0)
            si = pltpu.roll(xi, s, axis=0)
            xr, xi = xr + ar * sr - ai * si, xi + ar * si + ai * sr
        if per_tile:
            car_r, car_i = h0r_ref[i], h0i_ref[i]
        hr = xr + pr * car_r - pi * car_i
        hi = xi + pr * car_i + pi * car_r
        hr_ref[SUBLANES * i:SUBLANES * (i + 1)] = hr
        hi_ref[SUBLANES * i:SUBLANES * (i + 1)] = hi
        if per_tile:
            hlr_ref[i] = hr[last_row:last_row + 1]
            hli_ref[i] = hi[last_row:last_row + 1]
        else:
            car_r, car_i = hr[last_row:last_row + 1], hi[last_row:last_row + 1]
    y = (_dot(hr_ref[...].astype(BF16), wor_ref[0]) - _dot(hi_ref[...].astype(BF16), woi_ref[0])
         + d_ref[...] * v)
    y_ref[...] = jax.nn.gelu(y).astype(y_ref.dtype)
    if not per_tile:
        cr_ref[...] = car_r
        ci_ref[...] = car_i

        @pl.when(c == nch - 1)
        def _():
            hlr_ref[0] = car_r
            hli_ref[0] = car_i


def _s5_core(v, valid, w_in_re, w_in_im, w_out_re, w_out_im, pow_re, pow_im, d_skip, h0_re, h0_im, *,
             prompt, y_init=None):
    nseq = h0_re.shape[0]
    if prompt:
        grid = (nseq, S5_NSB, PROMPT_CHUNKS)
        blk = _prompt_blk
        sidx = lambda b, s, c: (b, 0, s)
        spb = 1
        kw = dict(per_tile=False, nch=PROMPT_CHUNKS, last_row=SUBLANES - 1)
        sem = ("parallel", "parallel", "arbitrary")
    else:
        grid = (SAMPLE_BLKS, S5_NSB, 1)
        blk = lambda b, c: b
        sidx = lambda b, s, c: (b, 0, s)
        spb = ROW_BLK // SAMPLE_ROWS_PER_SEQ
        kw = dict(per_tile=True, nch=1, last_row=DEC_SEQ - 1)
        sem = ("parallel", "parallel", "arbitrary")
    row_spec = pl.BlockSpec((ROW_BLK, S5_VBLK), lambda b, s, c: (blk(b, c), s))
    st_spec = pl.BlockSpec((spb, 1, S5_SBLK), sidx)
    w_in_spec = pl.BlockSpec((1, S5_VBLK, S5_SBLK), lambda b, s, c: (s, 0, 0))
    w_out_spec = pl.BlockSpec((1, S5_SBLK, S5_VBLK), lambda b, s, c: (s, 0, 0))
    pow_spec = pl.BlockSpec((SUBLANES, S5_SBLK), lambda b, s, c: (0, s))
    in_specs = [row_spec, pl.BlockSpec((ROW_BLK, 1), lambda b, s, c: (blk(b, c), 0)),
                w_in_spec, w_in_spec, w_out_spec, w_out_spec, pow_spec, pow_spec,
                pl.BlockSpec((1, S5_VBLK), lambda b, s, c: (0, s)), st_spec, st_spec]
    args = [v, valid, w_in_re, w_in_im, w_out_re, w_out_im, pow_re, pow_im,
            d_skip.reshape(1, D_MODEL), h0_re, h0_im]
    kernel_fn = functools.partial(_s5_kernel, **kw)
    aliases = {}
    if y_init is not None:
        in_specs.append(pl.BlockSpec(memory_space=pl.ANY))
        args.append(y_init)
        aliases = {len(args) - 1: 0}
        kernel_fn = _drop_ref(kernel_fn, len(args) - 1)
    return pl.pallas_call(
        kernel_fn,
        grid=grid,
        in_specs=in_specs,
        out_specs=[row_spec, st_spec, st_spec],
        input_output_aliases=aliases,
        out_shape=[jax.ShapeDtypeStruct((T_ROWS, D_MODEL), BF16),
                   jax.ShapeDtypeStruct(h0_re.shape, F32), jax.ShapeDtypeStruct(h0_im.shape, F32)],
        scratch_shapes=[pltpu.VMEM((1, S5_SBLK), F32), pltpu.VMEM((1, S5_SBLK), F32),
                        pltpu.VMEM((ROW_BLK, S5_SBLK), F32), pltpu.VMEM((ROW_BLK, S5_SBLK), F32)],
        compiler_params=_cparams(*sem),
        name="s5_core_prompt" if prompt else "s5_core_sample",
    )(*args)


def _rwkv_proj_kernel(u_ref, p_ref, mu_ref, wr_ref, wk_ref, wv_ref, r_ref, k_ref, v_ref,
                      xr_ref, xk_ref, xv_ref):
    @pl.when(pl.program_id(1) == 0)
    def _():
        u = u_ref[...]
        d = p_ref[...] - u
        xr_ref[...] = (u + d * mu_ref[0:1, :]).astype(BF16)
        xk_ref[...] = (u + d * mu_ref[2:3, :]).astype(BF16)
        xv_ref[...] = (u + d * mu_ref[3:4, :]).astype(BF16)

    r_ref[...] = _dot(xr_ref[...], wr_ref[...])
    k_ref[...] = _dot(xk_ref[...], wk_ref[...])
    v_ref[...] = _dot(xv_ref[...], wv_ref[...])


def _rwkv_proj(u, prev, mu, wr, wk, wv):
    m = u.shape[0]
    row = pl.BlockSpec((MM_TM, D_MODEL), lambda i, j: (i, 0))
    wsp = pl.BlockSpec((D_MODEL, MM_TN), lambda i, j: (0, j))
    osp = pl.BlockSpec((MM_TM, MM_TN), lambda i, j: (i, j))
    return pl.pallas_call(
        _rwkv_proj_kernel,
        grid=(m // MM_TM, D_MODEL // MM_TN),
        in_specs=[row, row, pl.BlockSpec((6, D_MODEL), lambda i, j: (0, 0)), wsp, wsp, wsp],
        out_specs=[osp, osp, osp],
        out_shape=[jax.ShapeDtypeStruct((m, D_MODEL), F32)] * 3,
        scratch_shapes=[pltpu.VMEM((MM_TM, D_MODEL), BF16)] * 3,
        compiler_params=_cparams("parallel", "arbitrary"),
        name="rwkv_proj",
    )(u, prev, mu, wr, wk, wv)


RWKV_LORA_TM = 256


def _rwkv_lora_kernel(u_ref, p_ref, mu_ref, w1_ref, w2_ref, a1_ref, a2_ref, g1_ref, g2_ref,
                      w0_ref, a0_ref, lw_ref, a_ref, g_ref):
    u = u_ref[...]
    d = p_ref[...] - u
    xw = (u + d * mu_ref[1:2, :]).astype(BF16)
    xa = (u + d * mu_ref[4:5, :]).astype(BF16)
    xg = (u + d * mu_ref[5:6, :]).astype(BF16)
    wpre = w0_ref[...] + _dot(jnp.tanh(_dot(xw, w1_ref[...])).astype(BF16), w2_ref[...])
    w = -_softplus(-wpre) - 0.5
    lw_ref[...] = -jnp.exp(w)
    a_ref[...] = jax.nn.sigmoid(a0_ref[...] + _dot(_dot(xa, a1_ref[...]).astype(BF16), a2_ref[...]))
    g_ref[...] = _dot(jax.nn.sigmoid(_dot(xg, g1_ref[...])).astype(BF16), g2_ref[...])


def _rwkv_lora(u, prev, mu, w1, w2, a1, a2, g1, g2, w0, a0):
    m = u.shape[0]
    row = pl.BlockSpec((RWKV_LORA_TM, D_MODEL), lambda i: (i, 0))
    full = lambda a: pl.BlockSpec(a.shape, lambda i: (0, 0))
    vec = pl.BlockSpec((1, D_MODEL), lambda i: (0, 0))
    return pl.pallas_call(
        _rwkv_lora_kernel,
        grid=(m // RWKV_LORA_TM,),
        in_specs=[row, row, pl.BlockSpec((6, D_MODEL), lambda i: (0, 0)),
                  full(w1), full(w2), full(a1), full(a2), full(g1), full(g2), vec, vec],
        out_specs=[row, row, row],
        out_shape=[jax.ShapeDtypeStruct((m, D_MODEL), F32)] * 3,
        compiler_params=_cparams("parallel"),
        name="rwkv_lora",
    )(u, prev, mu, w1, w2, a1, a2, g1, g2, w0.reshape(1, D_MODEL), a0.reshape(1, D_MODEL))


def _block_ones():
    r = lax.broadcasted_iota(jnp.int32, (LANES, LANES), 0) // RWKV_HEAD_DIM
    c = lax.broadcasted_iota(jnp.int32, (LANES, LANES), 1) // RWKV_HEAD_DIM
    return (r == c).astype(BF16)


def _head_sum(x, bo):
    hi, lo = _split_bf16(x)
    return _dot(hi, bo) + _dot(lo, bo)


RWKV_PG = 8
RWKV_NPG = RWKV_PAIRS // RWKV_PG
RWKV_PG_W = RWKV_PG * LANES


RWKV_C = 64
RWKV_PROMPT_CHUNKS = (ROW_BLK + SEQ) // RWKV_C


def _prompt_blk64(b, c):
    per = ROW_BLK // RWKV_C
    return jnp.where(c < per, per * (META_BLK0 + b) + c,
                     per * (MAIN_BLK0 + MAIN_BLKS_PER_B * b) + c - per)


def _load_head_pair(ref, si, p):
    return jnp.concatenate([ref[si, 2 * p], ref[si, 2 * p + 1]], axis=1)


def _store_head_pair(ref, si, p, s2):
    ref[si, 2 * p] = s2[:, :RWKV_HEAD_DIM]
    ref[si, 2 * p + 1] = s2[:, RWKV_HEAD_DIM:]


def _pair_bd(x, head_a):
    zero = jnp.zeros_like(x)
    return jnp.concatenate([jnp.where(head_a, x, zero), jnp.where(head_a, zero, x)], axis=0)


def _rwkv_chunk_kernel(r_ref, k_ref, v_ref, lw_ref, a_ref, g_ref, valid_ref, kk_p, ka_p, rk_p,
                       lnw_p, lnb_p, s0_ref, o_ref, sl_ref, s_ref, yraw_s, *, rows, nsb, chained):
    c = pl.program_id(2)
    cl = rows
    cp = RWKV_C
    levels = max(1, (cl - 1).bit_length())

    if chained:
        @pl.when(c == 0)
        def _():
            for p in range(RWKV_PG):
                s_ref[p] = _load_head_pair(s0_ref, 0, p)

    bo = _block_ones()
    lane = lax.broadcasted_iota(jnp.int32, (cl, LANES), 1)
    row = lax.broadcasted_iota(jnp.int32, (cl, LANES), 0)
    head_a = lane < RWKV_HEAD_DIM
    head_a_pad = lax.broadcasted_iota(jnp.int32, (cp, LANES), 1) < RWKV_HEAD_DIM
    s_idx = lane % RWKV_HEAD_DIM
    strict = s_idx < row
    incl = s_idx <= row
    eye2 = (s_idx == row).astype(F32)
    tri = (lax.broadcasted_iota(jnp.int32, (cl, cp), 0)
           >= lax.broadcasted_iota(jnp.int32, (cl, cp), 1)).astype(BF16)
    b16 = lambda x: x.astype(BF16)
    cat = lambda *xs: jnp.concatenate(xs, axis=0)
    halves = lambda x: x[:cl] + x[cl:]

    def pad_rows(x, n):
        if x.shape[0] == n:
            return x
        return cat(x, jnp.zeros((n - x.shape[0], x.shape[1]), x.dtype))

    def split(x):
        hi = b16(x).astype(F32)
        return hi, x - hi

    bd = lambda x: _pair_bd(b16(pad_rows(x, cp)), head_a_pad)
    lhs = lambda *xs: b16(pad_rows(cat(*xs), -(-len(xs) * cl // 16) * 16))

    def pair_stages(si, p):
        sl = slice(LANES * p, LANES * (p + 1))
        rsl = slice(si * cl, (si + 1) * cl)
        valid = valid_ref[rsl]
        k, a, r = k_ref[rsl, sl], a_ref[rsl, sl], r_ref[rsl, sl]
        vm = v_ref[rsl, sl] * valid
        lw = lw_ref[rsl, sl] * valid
        kkr = k * kk_p[:, sl]
        kk = kkr / jnp.maximum(jnp.sqrt(_head_sum(kkr * kkr, bo)), 1e-12) * valid
        bv = kk * a
        km = k * (1.0 + (a - 1.0) * ka_p[:, sl])
        lw_hi, lw_lo = split(lw)
        g = _dot(tri, b16(pad_rows(lw_hi, cp))) + _dot(tri, b16(pad_rows(lw_lo, cp)))
        yield
        g_end = g[cl - 1:cl]
        e_neg = jnp.exp(-g)
        e_end = jnp.exp(g_end - g)
        at = -kk * jnp.exp(g - lw)
        rt = r * jnp.exp(g)
        x2 = lhs(at, rt)
        gb = _dot_nt(x2, bd(bv * e_neg))
        gk = _dot_nt(x2, bd(km * e_neg))
        yield
        n = jnp.where(strict, gb[:cl], 0.0)
        mrb = jnp.where(incl, gb[cl:2 * cl], 0.0)
        lak = jnp.where(strict, gk[:cl], 0.0)
        mrk = jnp.where(incl, gk[cl:2 * cl], 0.0)
        pk, t = n, eye2
        for level in range(levels):
            if level < levels - 1:
                res = _dot(lhs(pk, t), bd(pk))
                pk, t = res[:cl], t + res[cl:2 * cl]
            else:
                t = t + _dot(lhs(t), bd(pk))[:cl]
            yield
        n_hi, n_lo = split(n)
        t_hi, t_lo = split(t)
        nt = _dot(lhs(n_hi, n_lo), bd(t_hi))
        nt = nt[:cl] + nt[cl:2 * cl] + _dot(lhs(n_hi), bd(t_lo))[:cl]
        wy = _dot(lhs(lak, mrk), bd(vm))
        yield
        t = t + _dot(lhs(t_hi), bd(eye2 - t + nt))[:cl]
        yield
        tt = lhs(*split(t))
        pmat = _dot(tt, bd(at))
        pmat = pmat[:cl] + pmat[cl:2 * cl]
        q = _dot(tt, bd(wy[:cl]))
        q = q[:cl] + q[cl:2 * cl]
        yield
        s2 = s_ref[p] if chained else _load_head_pair(s0_ref, si, p)
        res = _dot_nt(lhs(*split(pmat), rt), _pair_bd(b16(s2), head_a_pad))
        yield
        u = res[:cl] + res[cl:2 * cl] + q
        y = res[2 * cl:3 * cl] + _dot(lhs(mrb), bd(u))[:cl] + wy[cl:2 * cl]
        v_hi, v_lo = split(vm)
        kg_hi, kg_lo = split(km * e_end)
        full = _dot_tn(lhs(u, v_hi, v_lo, v_hi), lhs(bv * e_end, kg_hi, kg_hi, kg_lo))
        yield
        s_new = s2 * jnp.exp(g_end) + jnp.where(head_a_pad, full[:RWKV_HEAD_DIM], full[RWKV_HEAD_DIM:])
        if chained:
            s_ref[p] = s_new
        else:
            _store_head_pair(sl_ref, si, p, s_new)
        d = y - _head_sum(y, bo) * inv
        yield
        yn = d * lax.rsqrt(_head_sum(d * d, bo) * inv + RWKV_LN_EPS) * lnw_p[:, sl] + lnb_p[:, sl]
        bonus = _head_sum(r * km * rk_p[:, sl], bo) * vm
        yraw_s[rsl, sl] = (yn + bonus) * g_ref[rsl, sl]

    inv = 1.0 / RWKV_HEAD_DIM
    live = [pair_stages(si, p) for si in range(nsb) for p in range(RWKV_PG)]
    while live:
        live = [gen for gen in live if next(gen, True) is None]
    o_ref[...] = yraw_s[...].astype(o_ref.dtype)

    if chained:
        @pl.when(c == RWKV_PROMPT_CHUNKS - 1)
        def _():
            for p in range(RWKV_PG):
                _store_head_pair(sl_ref, 0, p, s_ref[p])


RWKV_SAMPLE_SEQS = 4
RWKV_SAMPLE_ROWS = RWKV_SAMPLE_SEQS * SAMPLE_ROWS_PER_SEQ


def _rwkv_core(r, k, v, dec, a, g, valid, k_k, k_a, r_k, ln_w, ln_b, s0, *, prompt, y_init=None):
    nseq = s0.shape[0]
    vec = lambda a_: a_.reshape(1, D_MODEL)
    if prompt:
        rows = RWKV_C
        grid = (nseq, RWKV_NPG, RWKV_PROMPT_CHUNKS)
        rmap = lambda b, q, c: (_prompt_blk64(b, c), q)
        vmap = lambda b, q, c: (_prompt_blk64(b, c), 0)
        spb = 1
        kernel_fn = functools.partial(_rwkv_chunk_kernel, rows=RWKV_C, nsb=1, chained=True)
    else:
        rows = RWKV_SAMPLE_ROWS
        grid = (nseq // RWKV_SAMPLE_SEQS, RWKV_NPG, 1)
        rmap = lambda b, q, c: (b, q)
        vmap = lambda b, q, c: (b, 0)
        spb = RWKV_SAMPLE_SEQS
        kernel_fn = functools.partial(_rwkv_chunk_kernel, rows=SAMPLE_ROWS_PER_SEQ, nsb=spb,
                                      chained=False)
    scratch = [pltpu.VMEM((RWKV_PG, RWKV_HEAD_DIM, LANES), F32), pltpu.VMEM((rows, RWKV_PG_W), F32)]
    sem = ("parallel", "parallel", "arbitrary")
    row_spec = pl.BlockSpec((rows, RWKV_PG_W), rmap)
    vec_spec = pl.BlockSpec((1, RWKV_PG_W), lambda b, q, c: (0, q))
    st_spec = pl.BlockSpec((spb, 2 * RWKV_PG, RWKV_HEAD_DIM, RWKV_HEAD_DIM), lambda b, q, c: (b, q, 0, 0))
    in_specs = [row_spec] * 6 + [pl.BlockSpec((rows, 1), vmap)] + [vec_spec] * 5 + [st_spec]
    args = [r, k, v, dec, a, g, valid, vec(k_k), vec(k_a), vec(r_k), vec(ln_w), vec(ln_b), s0]
    aliases = {}
    if y_init is not None:
        in_specs.append(pl.BlockSpec(memory_space=pl.ANY))
        args.append(y_init)
        aliases = {len(args) - 1: 0}
        kernel_fn = _drop_ref(kernel_fn, len(args) - 1)
    return pl.pallas_call(
        kernel_fn,
        grid=grid,
        in_specs=in_specs,
        out_specs=[row_spec, st_spec],
        input_output_aliases=aliases,
        out_shape=[jax.ShapeDtypeStruct((T_ROWS, D_MODEL), BF16), jax.ShapeDtypeStruct(s0.shape, F32)],
        scratch_shapes=scratch,
        compiler_params=_cparams(*sem),
        name="rwkv_core_prompt" if prompt else "rwkv_core_sample",
    )(*args)


def _row_ids():
    sample_t0 = jnp.arange(DEC_BATCH) * SAMPLE_ROWS_PER_SEQ
    main0 = MAIN_BLK0 * ROW_BLK + jnp.arange(BATCH) * SEQ
    meta0 = META_BLK0 * ROW_BLK + jnp.arange(BATCH) * ROW_BLK + META_PAD
    return sample_t0, main0, meta0


def _valid_mask():
    r = jnp.arange(T_ROWS)
    sample = (r < MAIN_BLK0 * ROW_BLK) & (r % SAMPLE_ROWS_PER_SEQ < DEC_SEQ)
    main = (r >= MAIN_BLK0 * ROW_BLK) & (r < META_BLK0 * ROW_BLK)
    meta = (r >= META_BLK0 * ROW_BLK) & (r % ROW_BLK >= META_PAD)
    return (sample | main | meta).astype(F32)


def _mixer_out_init(width):
    return jnp.zeros((T_ROWS, width), BF16)


def _pad_conv_prev(prev):
    return jnp.pad(prev, ((0, 0), (SUBLANES - SSD_CONV_K + 1, 0), (0, 0)))


def _ssd_layer(h, valid, norm_w, w_in, conv_w, conv_b, dt_bias, a_log, d_skip, gnorm_w, w_out,
               state_all, layer, state_conv, prev_states=None):
    sample_t0, main0, meta0 = _row_ids()
    w_in_b = w_in.astype(BF16)
    proj = _matmul(h, w_in_b, SSD_ZX_DIM, norm_w=norm_w, tn=MM_TN_WIDE)
    dtt = _ssd_dt(h, norm_w, w_in_b[:, SSD_ZX_DIM:].T, dt_bias)
    xbc = lambda rows: proj[rows][..., SSD_D_INNER:]
    tail = jnp.arange(SSD_CONV_K - 1)
    validt = valid.reshape(N_BLKS, 1, ROW_BLK)
    dtt_p = jnp.transpose(dtt.reshape(SSD_GROUPS, SSD_HPG, N_BLKS, ROW_BLK), (2, 0, 1, 3))
    gn = gnorm_w.reshape(1, SSD_D_INNER)
    cb = conv_b.reshape(1, SSD_CONV_DIM)
    h0_meta = jnp.zeros((1, BATCH, SSD_HEADS, SSD_HEAD_DIM, SSD_STATE), F32)
    prev_p = jnp.zeros((BATCH, SUBLANES, SSD_CONV_DIM), F32)
    y, p_state = _ssd_core(proj, valid.reshape(T_ROWS, 1), validt, dtt_p, a_log, d_skip, conv_w, cb,
                           prev_p, gn, h0_meta, prompt=True, y_init=_mixer_out_init(SSD_D_INNER))
    ns = SAMPLE_ROWS_PER_SEQ
    dtt_s = dtt[:, :DEC_BATCH * ns].reshape(SSD_GROUPS, SSD_HPG, DEC_BATCH, ns)
    dtt_s = jnp.pad(jnp.transpose(dtt_s, (2, 0, 1, 3)), ((0, 0), (0, 0), (0, 0), (0, SSD_Q - ns)))
    validt_s = jnp.pad(valid[:DEC_BATCH * ns].reshape(DEC_BATCH, 1, ns), ((0, 0), (0, 0), (0, SSD_Q - ns)))
    y, s_state = _ssd_core(proj, valid.reshape(T_ROWS, 1), validt_s, dtt_s, a_log, d_skip, conv_w, cb,
                           _pad_conv_prev(state_conv), gn, state_all, prompt=False, y_init=y,
                           h0_layer=layer, prev_states=prev_states)
    h = _matmul(y, w_out.astype(BF16), D_MODEL, res=h, tn=MM_TN_MID)
    p_conv = xbc(main0[:, None] + SEQ - (SSD_CONV_K - 1) + tail[None, :])
    s_conv = xbc(sample_t0[:, None] + DEC_SEQ - (SSD_CONV_K - 1) + tail[None, :])
    return h, p_state[0], p_conv, s_state, s_conv


def _s5_layer(h, valid, norm_w, w_in, lam_re, lam_im, log_dt, b_re, b_im, c_re, c_im, d_skip, w_out,
              state_re, state_im):
    v = _matmul(h, w_in.astype(BF16), D_MODEL, norm_w=norm_w, tn=MM_TN_MID)
    pow_re, pow_im, bb_re, bb_im = _s5_params(lam_re, lam_im, log_dt, b_re, b_im)
    wi_re, wo_re = _s5_block_diag(bb_re, c_re)
    wi_im, wo_im = _s5_block_diag(bb_im, c_im)
    vcol = valid.reshape(T_ROWS, 1)
    zero = jnp.zeros((BATCH, 1, S5_HDIM), F32)
    y, p_re, p_im = _s5_core(v, vcol, wi_re, wi_im, wo_re, wo_im, pow_re, pow_im, d_skip, zero, zero,
                             prompt=True, y_init=_mixer_out_init(D_MODEL))
    y, s_re, s_im = _s5_core(v, vcol, wi_re, wi_im, wo_re, wo_im, pow_re, pow_im, d_skip,
                             state_re.reshape(DEC_BATCH, 1, S5_HDIM),
                             state_im.reshape(DEC_BATCH, 1, S5_HDIM), prompt=False, y_init=y)
    h = _matmul(y, w_out.astype(BF16), D_MODEL, res=h, glu_off=D_MODEL // MM_TN_MID, tn=MM_TN_MID)
    shp = lambda a, n: a.reshape(n, S5_GROUPS, S5_STATE)
    return h, shp(p_re, BATCH), shp(p_im, BATCH), shp(s_re, DEC_BATCH), shp(s_im, DEC_BATCH)


def _pad_lora(w_down, w_up):
    n = w_down.shape[1]
    return (jnp.pad(w_down, ((0, 0), (0, RWKV_LORA_PAD - n))).astype(BF16),
            jnp.pad(w_up, ((0, RWKV_LORA_PAD - n), (0, 0))).astype(BF16))


def _rwkv_layer(h, valid, norm_w, mu, wr, wk, wv, wo, w0, w1, w2, a0, a1, a2, g1, g2, k_k, k_a, r_k,
                ln_w, ln_b, state, state_shift):
    sample_t0, main0, meta0 = _row_ids()
    u = _rmsnorm(h, norm_w)
    prev = jnp.concatenate([jnp.zeros((1, D_MODEL), F32), u[:-1]], axis=0)
    prev = prev.at[sample_t0].set(state_shift)
    prev = prev.at[main0].set(u[meta0 + N_META - 1])
    prev = prev.at[meta0].set(0.0)
    r, k, v = _rwkv_proj(u, prev, mu, wr.astype(BF16), wk.astype(BF16), wv.astype(BF16))
    w1p, w2p = _pad_lora(w1, w2)
    a1p, a2p = _pad_lora(a1, a2)
    dec, a, g = _rwkv_lora(u, prev, mu, w1p, w2p, a1p, a2p, g1.astype(BF16), g2.astype(BF16), w0, a0)
    vcol = valid.reshape(T_ROWS, 1)
    s0_p = jnp.zeros((BATCH, RWKV_HEADS, RWKV_HEAD_DIM, RWKV_HEAD_DIM), F32)
    y, p_state = _rwkv_core(r, k, v, dec, a, g, vcol, k_k, k_a, r_k, ln_w, ln_b, s0_p, prompt=True,
                            y_init=_mixer_out_init(D_MODEL))
    y, s_state = _rwkv_core(r, k, v, dec, a, g, vcol, k_k, k_a, r_k, ln_w, ln_b, state,
                            prompt=False, y_init=y)
    h = _matmul(y, wo.astype(BF16), D_MODEL, res=h, tn=MM_TN_MID)
    return h, p_state, u[main0 + SEQ - 1], s_state, u[sample_t0 + DEC_SEQ - 1]


def kernel(x_prompt, x_sample, state_ssd, state_ssd_conv, state_s5_re, state_s5_im, state_rwkv,
           state_rwkv_shift, meta_tokens, norm1_w, norm2_w, normf_w, ffn_w_gu, ffn_w_down,
           ssd_w_in, ssd_conv_w, ssd_conv_b, ssd_dt_bias, ssd_a_log, ssd_d, ssd_norm_w, ssd_w_out,
           s5_w_in, s5_lam_re, s5_lam_im, s5_log_dt, s5_b_re, s5_b_im, s5_c_re, s5_c_im, s5_d, s5_w_out,
           rwkv_mu, rwkv_wr, rwkv_wk, rwkv_wv, rwkv_wo, rwkv_w0, rwkv_w1, rwkv_w2, rwkv_a0, rwkv_a1,
           rwkv_a2, rwkv_g1, rwkv_g2, rwkv_k_k, rwkv_k_a, rwkv_r_k, rwkv_ln_w, rwkv_ln_b):
    valid = _valid_mask()
    sample = jnp.pad(x_sample, ((0, 0), (0, SAMPLE_ROWS_PER_SEQ - DEC_SEQ), (0, 0)))
    meta = jnp.pad(jnp.broadcast_to(meta_tokens[None], (BATCH, N_META, D_MODEL)),
                   ((0, 0), (META_PAD, 0), (0, 0)))
    h = jnp.concatenate([sample.reshape(-1, D_MODEL), x_prompt.reshape(-1, D_MODEL),
                         meta.reshape(-1, D_MODEL)], axis=0).astype(F32)
    outs = {name: [] for name in ("p_ssd", "p_conv", "p_s5r", "p_s5i", "p_rwkv", "p_shift",
                                  "s_conv", "s_s5r", "s_s5i", "s_rwkv", "s_shift")}
    s_ssd = None
    for i in range(DEPTH):
        kind, j = i % N_MIXERS, i // N_MIXERS
        if kind == 0:
            h, ps, pc, s_ssd, sc = _ssd_layer(
                h, valid, norm1_w[i], ssd_w_in[j], ssd_conv_w[j], ssd_conv_b[j], ssd_dt_bias[j],
                ssd_a_log[j], ssd_d[j], ssd_norm_w[j], ssd_w_out[j], state_ssd, j, state_ssd_conv[j],
                prev_states=s_ssd)
            outs["p_ssd"].append(ps); outs["p_conv"].append(pc)
            outs["s_conv"].append(sc)
        elif kind == 1:
            h, pr, pi, sr, si = _s5_layer(
                h, valid, norm1_w[i], s5_w_in[j], s5_lam_re[j], s5_lam_im[j], s5_log_dt[j], s5_b_re[j],
                s5_b_im[j], s5_c_re[j], s5_c_im[j], s5_d[j], s5_w_out[j], state_s5_re[j], state_s5_im[j])
            outs["p_s5r"].append(pr); outs["p_s5i"].append(pi)
            outs["s_s5r"].append(sr); outs["s_s5i"].append(si)
        else:
            h, ps, psh, ss, ssh = _rwkv_layer(
                h, valid, norm1_w[i], rwkv_mu[j], rwkv_wr[j], rwkv_wk[j], rwkv_wv[j], rwkv_wo[j],
                rwkv_w0[j], rwkv_w1[j], rwkv_w2[j], rwkv_a0[j], rwkv_a1[j], rwkv_a2[j], rwkv_g1[j],
                rwkv_g2[j], rwkv_k_k[j], rwkv_k_a[j], rwkv_r_k[j], rwkv_ln_w[j], rwkv_ln_b[j],
                state_rwkv[j], state_rwkv_shift[j])
            outs["p_rwkv"].append(ps); outs["p_shift"].append(psh)
            outs["s_rwkv"].append(ss); outs["s_shift"].append(ssh)
        h = _ffn(h, norm2_w[i], ffn_w_gu[i].astype(BF16), ffn_w_down[i].astype(BF16))
    yf = _rmsnorm(h, normf_w)
    y_sample = yf[:DEC_BATCH * SAMPLE_ROWS_PER_SEQ].reshape(DEC_BATCH, SAMPLE_ROWS_PER_SEQ, D_MODEL)
    y_prompt = yf[MAIN_BLK0 * ROW_BLK:META_BLK0 * ROW_BLK].reshape(BATCH, SEQ, D_MODEL)
    st = lambda name: jnp.stack(outs[name])
    return (y_prompt, y_sample[:, :DEC_SEQ],
            st("p_ssd"), st("p_conv"), st("p_s5r"), st("p_s5i"), st("p_rwkv"), st("p_shift"),
            s_ssd, st("s_conv"), st("s_s5r"), st("s_s5i"), st("s_rwkv"), st("s_shift"))
```

```python
import functools

import jax
import jax.numpy as jnp
from jax import lax
from jax.experimental import pallas as pl
from jax.experimental.pallas import tpu as pltpu

F32 = jnp.float32
BF16 = jnp.bfloat16

D_MODEL = 2048
BATCH = 4
SEQ = 2048
DEPTH = 4
DEC_BATCH = 128
DEC_SEQ = 4
N_META = 16
N_MIXERS = 3
NORM_EPS = 1e-6

SSD_D_INNER = 2 * D_MODEL
SSD_HEAD_DIM = 64
SSD_HEADS = SSD_D_INNER // SSD_HEAD_DIM
SSD_STATE = 128
SSD_GROUPS = 8
SSD_HPG = SSD_HEADS // SSD_GROUPS
SSD_GN = SSD_GROUPS * SSD_STATE
SSD_CONV_K = 4
SSD_CONV_DIM = SSD_D_INNER + 2 * SSD_GN
SSD_ZX_DIM = SSD_D_INNER + SSD_CONV_DIM
SSD_GROUP_W = SSD_D_INNER // SSD_GROUPS

S5_GROUP_SIZE = 16
S5_GROUPS = D_MODEL // S5_GROUP_SIZE
S5_STATE = 64
S5_HDIM = S5_GROUPS * S5_STATE
S5_SBLK = 1024
S5_NSB = S5_HDIM // S5_SBLK
S5_VBLK = S5_SBLK // S5_STATE * S5_GROUP_SIZE

RWKV_HEAD_DIM = 64
RWKV_HEADS = D_MODEL // RWKV_HEAD_DIM
RWKV_PAIRS = RWKV_HEADS // 2
RWKV_LORA_PAD = 128
RWKV_G_LORA = 256
RWKV_LN_EPS = 64e-5

FFN_HIDDEN = -(-8 * D_MODEL // (3 * 256)) * 256

LANES = 128
SUBLANES = 8
ROW_BLK = 128
SAMPLE_ROWS_PER_SEQ = SUBLANES
SAMPLE_BLKS = DEC_BATCH * SAMPLE_ROWS_PER_SEQ // ROW_BLK
MAIN_BLK0 = SAMPLE_BLKS
MAIN_BLKS_PER_B = SEQ // ROW_BLK
META_BLK0 = MAIN_BLK0 + BATCH * MAIN_BLKS_PER_B
N_BLKS = META_BLK0 + BATCH
T_ROWS = N_BLKS * ROW_BLK
PROMPT_CHUNKS = 1 + MAIN_BLKS_PER_B
META_PAD = ROW_BLK - N_META

MM_TM = 512
MM_TN = 512
MM_TN_MID = 1024
MM_TN_WIDE = 2048
VMEM_LIMIT = 48 * 2 ** 20


def _prompt_blk(b, c):
    return jnp.where(c == 0, META_BLK0 + b, MAIN_BLK0 + MAIN_BLKS_PER_B * b + c - 1)


def _cparams(*sem):
    return pltpu.CompilerParams(dimension_semantics=sem, vmem_limit_bytes=VMEM_LIMIT)


def _silu(x):
    return x * (0.5 + 0.5 * jnp.tanh(0.5 * x))


def _softplus(x):
    return jnp.maximum(x, 0.0) + jnp.log(1.0 + jnp.exp(-jnp.abs(x)))


def _rms(x, w):
    return x * lax.rsqrt(jnp.mean(x * x, axis=-1, keepdims=True) + NORM_EPS) * w


def _dot(a, b):
    return jnp.dot(a, b, preferred_element_type=F32)


def _dot_nt(a, b):
    return lax.dot_general(a, b, (((1,), (1,)), ((), ())), preferred_element_type=F32)


def _dot_tn(a, b):
    return lax.dot_general(a, b, (((0,), (0,)), ((), ())), preferred_element_type=F32)


def _split_bf16(x):
    hi = x.astype(BF16)
    return hi, (x - hi.astype(F32)).astype(BF16)


def _drop_ref(kernel_fn, idx):
    def wrapped(*refs):
        return kernel_fn(*refs[:idx], *refs[idx + 1:])
    return wrapped


def _mm_kernel(*refs, normalize, glu, has_res):
    it = iter(refs)
    x_ref = next(it)
    nw_ref = next(it) if normalize else None
    w_ref = next(it)
    w2_ref = next(it) if glu else None
    res_ref = next(it) if has_res else None
    o_ref = next(it)
    xn_ref = next(it) if normalize else None
    if normalize:
        @pl.when(pl.program_id(1) == 0)
        def _():
            xn_ref[...] = _rms(x_ref[...], nw_ref[...]).astype(BF16)
        xb = xn_ref[...]
    else:
        xb = x_ref[...].astype(BF16)
    acc = _dot(xb, w_ref[...])
    if glu:
        acc = acc * jax.nn.sigmoid(_dot(xb, w2_ref[...]))
    if has_res:
        acc = acc + res_ref[...]
    o_ref[...] = acc.astype(o_ref.dtype)


def _matmul(x, w, n_out, *, norm_w=None, res=None, glu_off=None, out_dtype=F32, tn=MM_TN):
    m, k = x.shape
    normalize = norm_w is not None
    glu = glu_off is not None
    has_res = res is not None
    in_specs = [pl.BlockSpec((MM_TM, k), lambda i, j: (i, 0))]
    args = [x]
    if normalize:
        in_specs.append(pl.BlockSpec((1, k), lambda i, j: (0, 0)))
        args.append(norm_w.reshape(1, k))
    in_specs.append(pl.BlockSpec((k, tn), lambda i, j: (0, j)))
    args.append(w)
    if glu:
        in_specs.append(pl.BlockSpec((k, tn), lambda i, j: (0, j + glu_off)))
        args.append(w)
    if has_res:
        in_specs.append(pl.BlockSpec((MM_TM, tn), lambda i, j: (i, j)))
        args.append(res)
    return pl.pallas_call(
        functools.partial(_mm_kernel, normalize=normalize, glu=glu, has_res=has_res),
        grid=(m // MM_TM, n_out // tn),
        in_specs=in_specs,
        out_specs=pl.BlockSpec((MM_TM, tn), lambda i, j: (i, j)),
        out_shape=jax.ShapeDtypeStruct((m, n_out), out_dtype),
        scratch_shapes=[pltpu.VMEM((MM_TM, k), BF16)] if normalize else [],
        compiler_params=_cparams("parallel", "arbitrary"),
        name="matmul_n%d%s%s%s" % (n_out, "_norm" * normalize, "_glu" * glu, "_res" * has_res),
    )(*args)


def _rmsnorm_kernel(x_ref, w_ref, o_ref):
    o_ref[...] = _rms(x_ref[...], w_ref[...])


def _rmsnorm(x, w):
    m, k = x.shape
    return pl.pallas_call(
        _rmsnorm_kernel,
        grid=(m // MM_TM,),
        in_specs=[pl.BlockSpec((MM_TM, k), lambda i: (i, 0)), pl.BlockSpec((1, k), lambda i: (0, 0))],
        out_specs=pl.BlockSpec((MM_TM, k), lambda i: (i, 0)),
        out_shape=jax.ShapeDtypeStruct((m, k), F32),
        compiler_params=_cparams("parallel"),
        name="rmsnorm",
    )(x, w.reshape(1, k))


FFN_TF = 512
FFN_NF = FFN_HIDDEN // FFN_TF


def _ffn_kernel(x_ref, nw_ref, wg_ref, wu_ref, wd_ref, o_ref, xn_ref):
    f = pl.program_id(1)

    @pl.when(f == 0)
    def _():
        x = x_ref[...]
        xn_ref[...] = _rms(x, nw_ref[...]).astype(BF16)
        o_ref[...] = x

    xb = xn_ref[...]
    act = _silu(_dot(xb, wg_ref[...])) * _dot(xb, wu_ref[...])
    o_ref[...] += _dot(act.astype(BF16), wd_ref[...])


def _ffn(h, norm_w, w_gu, w_down):
    m = h.shape[0]
    return pl.pallas_call(
        _ffn_kernel,
        grid=(m // MM_TM, FFN_NF),
        in_specs=[
            pl.BlockSpec((MM_TM, D_MODEL), lambda i, f: (i, 0)),
            pl.BlockSpec((1, D_MODEL), lambda i, f: (0, 0)),
            pl.BlockSpec((D_MODEL, FFN_TF), lambda i, f: (0, f)),
            pl.BlockSpec((D_MODEL, FFN_TF), lambda i, f: (0, f + FFN_NF)),
            pl.BlockSpec((FFN_TF, D_MODEL), lambda i, f: (f, 0)),
        ],
        out_specs=pl.BlockSpec((MM_TM, D_MODEL), lambda i, f: (i, 0)),
        out_shape=jax.ShapeDtypeStruct((m, D_MODEL), F32),
        scratch_shapes=[pltpu.VMEM((MM_TM, D_MODEL), BF16)],
        compiler_params=_cparams("parallel", "arbitrary"),
        name="ffn",
    )(h, norm_w.reshape(1, D_MODEL), w_gu, w_gu, w_down)


def _dt_kernel(x_ref, nw_ref, wt_ref, bias_ref, o_ref):
    xb = _rms(x_ref[...], nw_ref[...]).astype(BF16)
    o_ref[...] = _softplus(_dot_nt(wt_ref[...], xb) + bias_ref[...])


def _ssd_dt(h, norm_w, w_dt_t, dt_bias):
    m = h.shape[0]
    return pl.pallas_call(
        _dt_kernel,
        grid=(m // MM_TM,),
        in_specs=[
            pl.BlockSpec((MM_TM, D_MODEL), lambda i: (i, 0)),
            pl.BlockSpec((1, D_MODEL), lambda i: (0, 0)),
            pl.BlockSpec((SSD_HEADS, D_MODEL), lambda i: (0, 0)),
            pl.BlockSpec((SSD_HEADS, 1), lambda i: (0, 0)),
        ],
        out_specs=pl.BlockSpec((SSD_HEADS, MM_TM), lambda i: (0, i)),
        out_shape=jax.ShapeDtypeStruct((SSD_HEADS, m), F32),
        compiler_params=_cparams("parallel"),
        name="ssd_dt",
    )(h, norm_w.reshape(1, D_MODEL), w_dt_t, dt_bias.reshape(SSD_HEADS, 1))


SSD_Q = ROW_BLK
SSD_SAMPLE_SEQS = 4
SSD_MIN_COLS = 16
SSD_PROMPT_GROUPS = 8
SSD_SAMPLE_GROUPS = 2


def _ssd_kernel(z_ref, x_ref, b_ref, c_ref, valid_ref, validt_ref, dtt_ref, alog_ref, dskip_ref,
                cwx_ref, cwb_ref, cwc_ref, cbx_ref, cbb_ref, cbc_ref, cpx_ref, cpb_ref, cpc_ref,
                nw_ref, h0_ref, *rest, rows, nch, nsb, ngb, n_prev):
    c = pl.program_id(2)
    q = max(rows, SSD_MIN_COLS)
    chained = nch > 1
    if n_prev:
        pst_ref, y_ref, hl_all_ref, st_ref, xpx_ref, xpb_ref, xpc_ref = rest
        hl_all_ref[0:n_prev] = pst_ref[...]
        hl_ref = hl_all_ref.at[n_prev]
    else:
        y_ref, hl_all_ref, st_ref, xpx_ref, xpb_ref, xpc_ref = rest
        hl_ref = hl_all_ref.at[0]

    if chained:
        @pl.when(c == 0)
        def _():
            st_ref[...] = h0_ref[0]
            xpx_ref[0, 0:SUBLANES] = cpx_ref[0]
            xpb_ref[0, 0:SUBLANES] = cpb_ref[0]
            xpc_ref[0, 0:SUBLANES] = cpc_ref[0]

    def pad_rows(x):
        if rows == q:
            return x
        return jnp.concatenate([x, jnp.zeros((q - rows, x.shape[1]), x.dtype)], axis=0)

    def conv(si, raw, xp_ref, cp_ref, w_ref, bias_ref):
        if not chained:
            xp_ref[si, 0:SUBLANES] = cp_ref[si]
        xp_ref[si, SUBLANES:SUBLANES + rows] = raw
        acc = bias_ref[...]
        for k in range(SSD_CONV_K):
            acc = acc + w_ref[k:k + 1, :] * xp_ref[si, pl.ds(SUBLANES - SSD_CONV_K + 1 + k, rows), :]
        if chained:
            xp_ref[si, 0:SUBLANES] = raw[rows - SUBLANES:rows]
        return _silu(acc)

    lane = lax.broadcasted_iota(jnp.int32, (SSD_HPG, SSD_Q), 1)
    row_i = lax.broadcasted_iota(jnp.int32, (rows, q), 0)
    col_i = lax.broadcasted_iota(jnp.int32, (rows, q), 1)
    eye = row_i == col_i
    causal = row_i >= col_i
    head_a = lax.broadcasted_iota(jnp.int32, (1, LANES), 1) < SSD_HEAD_DIM
    row_a = lax.broadcasted_iota(jnp.int32, (2 * SSD_HEAD_DIM, 1), 0) < SSD_HEAD_DIM

    def to_col(rowvec):
        return jnp.sum(jnp.where(eye, rowvec, 0.0), axis=1, keepdims=True)

    def pair_stages(si, gi, p, xs, bm_pad, cm, cb, cum, dtt, ys):
        dskip = dskip_ref[gi]
        hsl = slice(gi * SSD_HPG + 2 * p, gi * SSD_HPG + 2 * p + 2)
        xh = xs[:, LANES * p:LANES * (p + 1)]
        dt_cols = [to_col(dtt[r:r + 1, :]) for r in (2 * p, 2 * p + 1)]
        xdt = xh * jnp.where(head_a, dt_cols[0], dt_cols[1])
        xhb = pad_rows(xdt).astype(BF16)
        st = st_ref[hsl] if chained else h0_ref[si, hsl]
        st = st.reshape(2 * SSD_HEAD_DIM, SSD_STATE)
        yo = _dot_nt(cm, st.astype(BF16))
        yd, e_col, w_col, e_last = [], [], [], []
        for r in (2 * p, 2 * p + 1):
            cum_row = cum[r:r + 1, :]
            cum_col = to_col(cum_row)
            lmat = jnp.exp(jnp.where(causal, cum_col - cum_row, -jnp.inf))
            yd.append(_dot((cb * lmat).astype(BF16), xhb))
            c_last = cum_row[:, q - 1:q]
            e_col.append(jnp.exp(cum_col))
            w_col.append(jnp.exp(c_last - cum_col))
            e_last.append(jnp.exp(c_last))
        yield
        d2 = jnp.where(head_a, dskip[:, 2 * p:2 * p + 1], dskip[:, 2 * p + 1:2 * p + 2])
        ys[p] = (jnp.where(head_a, yd[0], yd[1]) + yo * jnp.where(head_a, e_col[0], e_col[1]) + xh * d2)
        xw = pad_rows(xdt * jnp.where(head_a, w_col[0], w_col[1])).astype(BF16)
        st_new = st * jnp.where(row_a, e_last[0], e_last[1]) + _dot_tn(xw, bm_pad)
        yield
        st_new = st_new.reshape(2, SSD_HEAD_DIM, SSD_STATE)
        if chained:
            st_ref[hsl] = st_new
        else:
            hl_ref[si, hsl] = st_new

    gw = SSD_GROUP_W
    live, outs = [], []
    for si in range(nsb):
        rsl = slice(si * rows, (si + 1) * rows)
        valid = valid_ref[rsl]
        xs_all = conv(si, x_ref[rsl] * valid, xpx_ref, cpx_ref, cwx_ref, cbx_ref)
        bm_all = conv(si, b_ref[rsl] * valid, xpb_ref, cpb_ref, cwb_ref, cbb_ref)
        cm_all = conv(si, c_ref[rsl] * valid, xpc_ref, cpc_ref, cwc_ref, cbc_ref)
        seq_ys = []
        for gi in range(ngb):
            bm_pad = pad_rows(bm_all[:, SSD_STATE * gi:SSD_STATE * (gi + 1)]).astype(BF16)
            cm = cm_all[:, SSD_STATE * gi:SSD_STATE * (gi + 1)].astype(BF16)
            dtt = dtt_ref[si, gi] * validt_ref[si]
            cum = dtt * -jnp.exp(alog_ref[gi])
            s = 1
            while s < q:
                cum = cum + jnp.where(lane >= s, pltpu.roll(cum, s, axis=1), 0.0)
                s *= 2
            dtt, cum = dtt[:, :q], cum[:, :q]
            cb = _dot_nt(cm, bm_pad)
            ys = [None] * (SSD_HPG // 2)
            seq_ys.append(ys)
            xs = xs_all[:, gw * gi:gw * (gi + 1)]
            live += [pair_stages(si, gi, p, xs, bm_pad, cm, cb, cum, dtt, ys)
                     for p in range(SSD_HPG // 2)]
        outs.append(seq_ys)
    while live:
        live = [gen for gen in live if next(gen, True) is None]
    for gi in range(ngb):
        gsl = slice(gw * gi, gw * (gi + 1))
        y = jnp.concatenate([jnp.concatenate(seq_ys[gi], axis=1) for seq_ys in outs], axis=0)
        y = y * _silu(z_ref[:, gsl])
        y_ref[:, gsl] = _rms(y, nw_ref[:, gsl]).astype(y_ref.dtype)

    if chained:
        @pl.when(c == nch - 1)
        def _():
            hl_ref[0] = st_ref[...]


def _ssd_core(proj, valid, validt, dtt, a_log, d_skip, conv_w, conv_b, conv_prev8, norm_w, h0, *,
              prompt, y_init=None, h0_layer=0, prev_states=None):
    nseq = h0.shape[1]
    if prompt:
        rows, nch, nsb, ngb = ROW_BLK, PROMPT_CHUNKS, 1, SSD_PROMPT_GROUPS
        blk = _prompt_blk
    else:
        rows, nch, nsb, ngb = SAMPLE_ROWS_PER_SEQ, 1, SSD_SAMPLE_SEQS, SSD_SAMPLE_GROUPS
        blk = lambda b, c: b
    gw = SSD_GROUP_W * ngb
    sw = SSD_STATE * ngb
    hb = SSD_HPG * ngb
    x_off = SSD_D_INNER // gw
    b_off = (2 * SSD_D_INNER) // sw
    c_off = b_off + SSD_GROUPS // ngb
    cb_off = SSD_D_INNER // sw
    cc_off = cb_off + SSD_GROUPS // ngb
    row_spec = lambda w, off: pl.BlockSpec((nsb * rows, w), lambda b, g, c: (blk(b, c), g + off))
    par_spec = lambda r, w, off: pl.BlockSpec((r, w), lambda b, g, c: (0, g + off))
    prev_spec = lambda w, off: pl.BlockSpec((nsb, SUBLANES, w), lambda b, g, c: (b, 0, g + off))
    in_specs = [
        row_spec(gw, 0), row_spec(gw, x_off), row_spec(sw, b_off), row_spec(sw, c_off),
        pl.BlockSpec((nsb * rows, 1), lambda b, g, c: (blk(b, c), 0)),
        pl.BlockSpec((nsb, 1, SSD_Q), lambda b, g, c: (blk(b, c), 0, 0)),
        pl.BlockSpec((nsb, ngb, SSD_HPG, SSD_Q), lambda b, g, c: (blk(b, c), g, 0, 0)),
        pl.BlockSpec((ngb, SSD_HPG, 1), lambda b, g, c: (g, 0, 0)),
        pl.BlockSpec((ngb, 1, SSD_HPG), lambda b, g, c: (g, 0, 0)),
        par_spec(SSD_CONV_K, gw, 0), par_spec(SSD_CONV_K, sw, cb_off),
        par_spec(SSD_CONV_K, sw, cc_off),
        par_spec(1, gw, 0), par_spec(1, sw, cb_off), par_spec(1, sw, cc_off),
        prev_spec(gw, 0), prev_spec(sw, cb_off), prev_spec(sw, cc_off),
        par_spec(1, gw, 0),
        pl.BlockSpec((None, nsb, hb, SSD_HEAD_DIM, SSD_STATE),
                     lambda b, g, c: (h0_layer, b, g, 0, 0)),
    ]
    args = [proj, proj, proj, proj, valid, validt, dtt,
            a_log.reshape(SSD_GROUPS, SSD_HPG, 1), d_skip.reshape(SSD_GROUPS, 1, SSD_HPG),
            conv_w, conv_w, conv_w, conv_b, conv_b, conv_b, conv_prev8, conv_prev8, conv_prev8,
            norm_w, h0]
    n_prev = 0 if prev_states is None else prev_states.shape[0]
    all_spec = lambda n: pl.BlockSpec((n, nsb, hb, SSD_HEAD_DIM, SSD_STATE),
                                      lambda b, g, c: (0, b, g, 0, 0))
    if n_prev:
        in_specs.append(all_spec(n_prev))
        args.append(prev_states)
    kernel_fn = functools.partial(_ssd_kernel, rows=rows, nch=nch, nsb=nsb, ngb=ngb, n_prev=n_prev)
    aliases = {}
    if y_init is not None:
        in_specs.append(pl.BlockSpec(memory_space=pl.ANY))
        args.append(y_init)
        aliases = {len(args) - 1: 0}
        kernel_fn = _drop_ref(kernel_fn, len(args) - 1)
    return pl.pallas_call(
        kernel_fn,
        grid=(nseq // nsb, SSD_GROUPS // ngb, nch),
        in_specs=in_specs,
        out_specs=[row_spec(gw, 0), all_spec(n_prev + 1)],
        input_output_aliases=aliases,
        out_shape=[jax.ShapeDtypeStruct((T_ROWS, SSD_D_INNER), BF16),
                   jax.ShapeDtypeStruct((n_prev + 1,) + h0.shape[1:], F32)],
        scratch_shapes=[
            pltpu.VMEM((hb, SSD_HEAD_DIM, SSD_STATE), F32),
            pltpu.VMEM((nsb, SUBLANES + rows, gw), F32),
            pltpu.VMEM((nsb, SUBLANES + rows, sw), F32),
            pltpu.VMEM((nsb, SUBLANES + rows, sw), F32),
        ],
        compiler_params=_cparams("parallel", "parallel", "arbitrary"),
        name="ssd_core_prompt" if prompt else "ssd_core_sample",
    )(*args)


def _s5_abar(lam_re, lam_im, log_dt):
    dt = jnp.exp(log_dt)
    mag = jnp.exp(lam_re * dt)
    ang = lam_im * dt
    return mag * jnp.cos(ang), mag * jnp.sin(ang)


def _s5_pow_kernel(lr_ref, li_ref, ldt_ref, pr_ref, pi_ref):
    ar, ai = _s5_abar(lr_ref[...], li_ref[...], ldt_ref[...])
    row = lax.broadcasted_iota(jnp.int32, (SUBLANES, S5_HDIM), 0)
    pr, pi = ar, ai
    out_r = jnp.broadcast_to(ar, (SUBLANES, S5_HDIM))
    out_i = jnp.broadcast_to(ai, (SUBLANES, S5_HDIM))
    for k in range(1, SUBLANES):
        pr, pi = pr * ar - pi * ai, pr * ai + pi * ar
        out_r = jnp.where(row == k, pr, out_r)
        out_i = jnp.where(row == k, pi, out_i)
    pr_ref[...] = out_r
    pi_ref[...] = out_i


def _s5_bbar_kernel(lr_ref, li_ref, ldt_ref, br_ref, bi_ref, or_ref, oi_ref):
    lr, li = lr_ref[...], li_ref[...]
    ar, ai = _s5_abar(lr, li, ldt_ref[...])
    den = lr * lr + li * li
    f_re = ((ar - 1.0) * lr + ai * li) / den
    f_im = (ai * lr - (ar - 1.0) * li) / den
    br, bi = br_ref[...], bi_ref[...]
    or_ref[...] = f_re * br - f_im * bi
    oi_ref[...] = f_re * bi + f_im * br


def _s5_params(lam_re, lam_im, log_dt, b_re, b_im):
    ldt = jnp.repeat(log_dt, S5_STATE)
    row = lambda a: a.reshape(1, S5_HDIM)
    col = lambda a: a.reshape(S5_HDIM, 1)
    full = lambda shape: pl.BlockSpec(shape, lambda: (0,) * len(shape))
    pow_re, pow_im = pl.pallas_call(
        _s5_pow_kernel,
        in_specs=[full((1, S5_HDIM))] * 3,
        out_specs=[full((SUBLANES, S5_HDIM))] * 2,
        out_shape=[jax.ShapeDtypeStruct((SUBLANES, S5_HDIM), F32)] * 2,
    )(row(lam_re), row(lam_im), row(ldt))
    bb_re, bb_im = pl.pallas_call(
        _s5_bbar_kernel,
        in_specs=[full((S5_HDIM, 1))] * 3 + [full((S5_HDIM, S5_GROUP_SIZE))] * 2,
        out_specs=[full((S5_HDIM, S5_GROUP_SIZE))] * 2,
        out_shape=[jax.ShapeDtypeStruct((S5_HDIM, S5_GROUP_SIZE), F32)] * 2,
    )(col(lam_re), col(lam_im), col(ldt), b_re.reshape(S5_HDIM, S5_GROUP_SIZE),
      b_im.reshape(S5_HDIM, S5_GROUP_SIZE))
    return pow_re, pow_im, bb_re, bb_im


def _s5_block_diag(bb, c):
    gpb = S5_SBLK // S5_STATE
    eye = jnp.eye(gpb, dtype=F32)
    bb = bb.reshape(S5_NSB, gpb, S5_STATE, S5_GROUP_SIZE)
    w_in = eye[None, :, None, :, None] * jnp.transpose(bb, (0, 3, 1, 2))[:, None]
    w_in = w_in.reshape(S5_NSB, S5_VBLK, S5_SBLK)
    c = c.reshape(S5_NSB, gpb, S5_GROUP_SIZE, S5_STATE)
    w_out = eye[None, :, None, :, None] * jnp.transpose(c, (0, 1, 3, 2))[:, :, :, None, :]
    w_out = w_out.reshape(S5_NSB, S5_SBLK, S5_VBLK)
    return w_in.astype(BF16), w_out.astype(BF16)


def _s5_kernel(v_ref, valid_ref, wir_ref, wii_ref, wor_ref, woi_ref, pr_ref, pi_ref, d_ref,
               h0r_ref, h0i_ref, y_ref, hlr_ref, hli_ref, cr_ref, ci_ref, hr_ref, hi_ref, *,
               per_tile, nch, last_row):
    ntiles = ROW_BLK // SUBLANES
    if not per_tile:
        c = pl.program_id(2)

        @pl.when(c == 0)
        def _():
            cr_ref[...] = h0r_ref[0]
            ci_ref[...] = h0i_ref[0]

    v = v_ref[...]
    vb = (v * valid_ref[...]).astype(BF16)
    bu_r = _dot(vb, wir_ref[0])
    bu_i = _dot(vb, wii_ref[0])
    pr, pi = pr_ref[...], pi_ref[...]
    row = lax.broadcasted_iota(jnp.int32, (SUBLANES, S5_SBLK), 0)
    levels = []
    for s in (1, 2, 4):
        levels.append((s, jnp.where(row >= s, pr[s - 1:s], 0.0), jnp.where(row >= s, pi[s - 1:s], 0.0)))
    if not per_tile:
        car_r, car_i = cr_ref[...], ci_ref[...]
    for i in range(ntiles):
        xr = bu_r[SUBLANES * i:SUBLANES * (i + 1)]
        xi = bu_i[SUBLANES * i:SUBLANES * (i + 1)]
        for s, ar, ai in levels:
            sr = pltpu.roll(xr, s, axis=0)
            si = pltpu.roll(xi, s, axis=0)
            xr, xi = xr + ar * sr - ai * si, xi + ar * si + ai * sr
        if per_tile:
            car_r, car_i = h0r_ref[i], h0i_ref[i]
        hr = xr + pr * car_r - pi * car_i
        hi = xi + pr * car_i + pi * car_r
        hr_ref[SUBLANES * i:SUBLANES * (i + 1)] = hr
        hi_ref[SUBLANES * i:SUBLANES * (i + 1)] = hi
        if per_tile:
            hlr_ref[i] = hr[last_row:last_row + 1]
            hli_ref[i] = hi[last_row:last_row + 1]
        else:
            car_r, car_i = hr[last_row:last_row + 1], hi[last_row:last_row + 1]
    y = (_dot(hr_ref[...].astype(BF16), wor_ref[0]) - _dot(hi_ref[...].astype(BF16), woi_ref[0])
         + d_ref[...] * v)
    y_ref[...] = jax.nn.gelu(y).astype(y_ref.dtype)
    if not per_tile:
        cr_ref[...] = car_r
        ci_ref[...] = car_i

        @pl.when(c == nch - 1)
        def _():
            hlr_ref[0] = car_r
            hli_ref[0] = car_i


def _s5_core(v, valid, w_in_re, w_in_im, w_out_re, w_out_im, pow_re, pow_im, d_skip, h0_re, h0_im, *,
             prompt, y_init=None):
    nseq = h0_re.shape[0]
    if prompt:
        grid = (nseq, S5_NSB, PROMPT_CHUNKS)
        blk = _prompt_blk
        sidx = lambda b, s, c: (b, 0, s)
        spb = 1
        kw = dict(per_tile=False, nch=PROMPT_CHUNKS, last_row=SUBLANES - 1)
        sem = ("parallel", "parallel", "arbitrary")
    else:
        grid = (SAMPLE_BLKS, S5_NSB, 1)
        blk = lambda b, c: b
        sidx = lambda b, s, c: (b, 0, s)
        spb = ROW_BLK // SAMPLE_ROWS_PER_SEQ
        kw = dict(per_tile=True, nch=1, last_row=DEC_SEQ - 1)
        sem = ("parallel", "parallel", "arbitrary")
    row_spec = pl.BlockSpec((ROW_BLK, S5_VBLK), lambda b, s, c: (blk(b, c), s))
    st_spec = pl.BlockSpec((spb, 1, S5_SBLK), sidx)
    w_in_spec = pl.BlockSpec((1, S5_VBLK, S5_SBLK), lambda b, s, c: (s, 0, 0))
    w_out_spec = pl.BlockSpec((1, S5_SBLK, S5_VBLK), lambda b, s, c: (s, 0, 0))
    pow_spec = pl.BlockSpec((SUBLANES, S5_SBLK), lambda b, s, c: (0, s))
    in_specs = [row_spec, pl.BlockSpec((ROW_BLK, 1), lambda b, s, c: (blk(b, c), 0)),
                w_in_spec, w_in_spec, w_out_spec, w_out_spec, pow_spec, pow_spec,
                pl.BlockSpec((1, S5_VBLK), lambda b, s, c: (0, s)), st_spec, st_spec]
    args = [v, valid, w_in_re, w_in_im, w_out_re, w_out_im, pow_re, pow_im,
            d_skip.reshape(1, D_MODEL), h0_re, h0_im]
    kernel_fn = functools.partial(_s5_kernel, **kw)
    aliases = {}
    if y_init is not None:
        in_specs.append(pl.BlockSpec(memory_space=pl.ANY))
        args.append(y_init)
        aliases = {len(args) - 1: 0}
        kernel_fn = _drop_ref(kernel_fn, len(args) - 1)
    return pl.pallas_call(
        kernel_fn,
        grid=grid,
        in_specs=in_specs,
        out_specs=[row_spec, st_spec, st_spec],
        input_output_aliases=aliases,
        out_shape=[jax.ShapeDtypeStruct((T_ROWS, D_MODEL), BF16),
                   jax.ShapeDtypeStruct(h0_re.shape, F32), jax.ShapeDtypeStruct(h0_im.shape, F32)],
        scratch_shapes=[pltpu.VMEM((1, S5_SBLK), F32), pltpu.VMEM((1, S5_SBLK), F32),
                        pltpu.VMEM((ROW_BLK, S5_SBLK), F32), pltpu.VMEM((ROW_BLK, S5_SBLK), F32)],
        compiler_params=_cparams(*sem),
        name="s5_core_prompt" if prompt else "s5_core_sample",
    )(*args)


def _rwkv_proj_kernel(u_ref, p_ref, mu_ref, wr_ref, wk_ref, wv_ref, r_ref, k_ref, v_ref,
                      xr_ref, xk_ref, xv_ref):
    @pl.when(pl.program_id(1) == 0)
    def _():
        u = u_ref[...]
        d = p_ref[...] - u
        xr_ref[...] = (u + d * mu_ref[0:1, :]).astype(BF16)
        xk_ref[...] = (u + d * mu_ref[2:3, :]).astype(BF16)
        xv_ref[...] = (u + d * mu_ref[3:4, :]).astype(BF16)

    r_ref[...] = _dot(xr_ref[...], wr_ref[...])
    k_ref[...] = _dot(xk_ref[...], wk_ref[...])
    v_ref[...] = _dot(xv_ref[...], wv_ref[...])


def _rwkv_proj(u, prev, mu, wr, wk, wv):
    m = u.shape[0]
    row = pl.BlockSpec((MM_TM, D_MODEL), lambda i, j: (i, 0))
    wsp = pl.BlockSpec((D_MODEL, MM_TN), lambda i, j: (0, j))
    osp = pl.BlockSpec((MM_TM, MM_TN), lambda i, j: (i, j))
    return pl.pallas_call(
        _rwkv_proj_kernel,
        grid=(m // MM_TM, D_MODEL // MM_TN),
        in_specs=[row, row, pl.BlockSpec((6, D_MODEL), lambda i, j: (0, 0)), wsp, wsp, wsp],
        out_specs=[osp, osp, osp],
        out_shape=[jax.ShapeDtypeStruct((m, D_MODEL), F32)] * 3,
        scratch_shapes=[pltpu.VMEM((MM_TM, D_MODEL), BF16)] * 3,
        compiler_params=_cparams("parallel", "arbitrary"),
        name="rwkv_proj",
    )(u, prev, mu, wr, wk, wv)


RWKV_LORA_TM = 256


def _rwkv_lora_kernel(u_ref, p_ref, mu_ref, w1_ref, w2_ref, a1_ref, a2_ref, g1_ref, g2_ref,
                      w0_ref, a0_ref, lw_ref, a_ref, g_ref):
    u = u_ref[...]
    d = p_ref[...] - u
    xw = (u + d * mu_ref[1:2, :]).astype(BF16)
    xa = (u + d * mu_ref[4:5, :]).astype(BF16)
    xg = (u + d * mu_ref[5:6, :]).astype(BF16)
    wpre = w0_ref[...] + _dot(jnp.tanh(_dot(xw, w1_ref[...])).astype(BF16), w2_ref[...])
    w = -_softplus(-wpre) - 0.5
    lw_ref[...] = -jnp.exp(w)
    a_ref[...] = jax.nn.sigmoid(a0_ref[...] + _dot(_dot(xa, a1_ref[...]).astype(BF16), a2_ref[...]))
    g_ref[...] = _dot(jax.nn.sigmoid(_dot(xg, g1_ref[...])).astype(BF16), g2_ref[...])


def _rwkv_lora(u, prev, mu, w1, w2, a1, a2, g1, g2, w0, a0):
    m = u.shape[0]
    row = pl.BlockSpec((RWKV_LORA_TM, D_MODEL), lambda i: (i, 0))
    full = lambda a: pl.BlockSpec(a.shape, lambda i: (0, 0))
    vec = pl.BlockSpec((1, D_MODEL), lambda i: (0, 0))
    return pl.pallas_call(
        _rwkv_lora_kernel,
        grid=(m // RWKV_LORA_TM,),
        in_specs=[row, row, pl.BlockSpec((6, D_MODEL), lambda i: (0, 0)),
                  full(w1), full(w2), full(a1), full(a2), full(g1), full(g2), vec, vec],
        out_specs=[row, row, row],
        out_shape=[jax.ShapeDtypeStruct((m, D_MODEL), F32)] * 3,
        compiler_params=_cparams("parallel"),
        name="rwkv_lora",
    )(u, prev, mu, w1, w2, a1, a2, g1, g2, w0.reshape(1, D_MODEL), a0.reshape(1, D_MODEL))


def _block_ones():
    r = lax.broadcasted_iota(jnp.int32, (LANES, LANES), 0) // RWKV_HEAD_DIM
    c = lax.broadcasted_iota(jnp.int32, (LANES, LANES), 1) // RWKV_HEAD_DIM
    return (r == c).astype(BF16)


def _head_sum(x, bo):
    hi, lo = _split_bf16(x)
    return _dot(hi, bo) + _dot(lo, bo)


RWKV_PG = 8
RWKV_NPG = RWKV_PAIRS // RWKV_PG
RWKV_PG_W = RWKV_PG * LANES


RWKV_C = 64
RWKV_PROMPT_CHUNKS = (ROW_BLK + SEQ) // RWKV_C


def _prompt_blk64(b, c):
    per = ROW_BLK // RWKV_C
    return jnp.where(c < per, per * (META_BLK0 + b) + c,
                     per * (MAIN_BLK0 + MAIN_BLKS_PER_B * b) + c - per)


def _load_head_pair(ref, si, p):
    return jnp.concatenate([ref[si, 2 * p], ref[si, 2 * p + 1]], axis=1)


def _store_head_pair(ref, si, p, s2):
    ref[si, 2 * p] = s2[:, :RWKV_HEAD_DIM]
    ref[si, 2 * p + 1] = s2[:, RWKV_HEAD_DIM:]


def _pair_bd(x, head_a):
    zero = jnp.zeros_like(x)
    return jnp.concatenate([jnp.where(head_a, x, zero), jnp.where(head_a, zero, x)], axis=0)


def _rwkv_chunk_kernel(r_ref, k_ref, v_ref, lw_ref, a_ref, g_ref, valid_ref, kk_p, ka_p, rk_p,
                       lnw_p, lnb_p, s0_ref, o_ref, sl_ref, s_ref, yraw_s, *, rows, nsb, chained):
    c = pl.program_id(2)
    cl = rows
    cp = RWKV_C
    levels = max(1, (cl - 1).bit_length())

    if chained:
        @pl.when(c == 0)
        def _():
            for p in range(RWKV_PG):
                s_ref[p] = _load_head_pair(s0_ref, 0, p)

    bo = _block_ones()
    lane = lax.broadcasted_iota(jnp.int32, (cl, LANES), 1)
    row = lax.broadcasted_iota(jnp.int32, (cl, LANES), 0)
    head_a = lane < RWKV_HEAD_DIM
    head_a_pad = lax.broadcasted_iota(jnp.int32, (cp, LANES), 1) < RWKV_HEAD_DIM
    s_idx = lane % RWKV_HEAD_DIM
    strict = s_idx < row
    incl = s_idx <= row
    eye2 = (s_idx == row).astype(F32)
    tri = (lax.broadcasted_iota(jnp.int32, (cl, cp), 0)
           >= lax.broadcasted_iota(jnp.int32, (cl, cp), 1)).astype(BF16)
    b16 = lambda x: x.astype(BF16)
    cat = lambda *xs: jnp.concatenate(xs, axis=0)
    halves = lambda x: x[:cl] + x[cl:]

    def pad_rows(x, n):
        if x.shape[0] == n:
            return x
        return cat(x, jnp.zeros((n - x.shape[0], x.shape[1]), x.dtype))

    def split(x):
        hi = b16(x).astype(F32)
        return hi, x - hi

    bd = lambda x: _pair_bd(b16(pad_rows(x, cp)), head_a_pad)
    lhs = lambda *xs: b16(pad_rows(cat(*xs), -(-len(xs) * cl // 16) * 16))

    def pair_stages(si, p):
        sl = slice(LANES * p, LANES * (p + 1))
        rsl = slice(si * cl, (si + 1) * cl)
        valid = valid_ref[rsl]
        k, a, r = k_ref[rsl, sl], a_ref[rsl, sl], r_ref[rsl, sl]
        vm = v_ref[rsl, sl] * valid
        lw = lw_ref[rsl, sl] * valid
        kkr = k * kk_p[:, sl]
        kk = kkr / jnp.maximum(jnp.sqrt(_head_sum(kkr * kkr, bo)), 1e-12) * valid
        bv = kk * a
        km = k * (1.0 + (a - 1.0) * ka_p[:, sl])
        lw_hi, lw_lo = split(lw)
        g = _dot(tri, b16(pad_rows(lw_hi, cp))) + _dot(tri, b16(pad_rows(lw_lo, cp)))
        yield
        g_end = g[cl - 1:cl]
        e_neg = jnp.exp(-g)
        e_end = jnp.exp(g_end - g)
        at = -kk * jnp.exp(g - lw)
        rt = r * jnp.exp(g)
        x2 = lhs(at, rt)
        gb = _dot_nt(x2, bd(bv * e_neg))
        gk = _dot_nt(x2, bd(km * e_neg))
        yield
        n = jnp.where(strict, gb[:cl], 0.0)
        mrb = jnp.where(incl, gb[cl:2 * cl], 0.0)
        lak = jnp.where(strict, gk[:cl], 0.0)
        mrk = jnp.where(incl, gk[cl:2 * cl], 0.0)
        pk, t = n, eye2
        for level in range(levels):
            if level < levels - 1:
                res = _dot(lhs(pk, t), bd(pk))
                pk, t = res[:cl], t + res[cl:2 * cl]
            else:
                t = t + _dot(lhs(t), bd(pk))[:cl]
            yield
        n_hi, n_lo = split(n)
        t_hi, t_lo = split(t)
        nt = _dot(lhs(n_hi, n_lo), bd(t_hi))
        nt = nt[:cl] + nt[cl:2 * cl] + _dot(lhs(n_hi), bd(t_lo))[:cl]
        wy = _dot(lhs(lak, mrk), bd(vm))
        yield
        t = t + _dot(lhs(t_hi), bd(eye2 - t + nt))[:cl]
        yield
        tt = lhs(*split(t))
        pmat = _dot(tt, bd(at))
        pmat = pmat[:cl] + pmat[cl:2 * cl]
        q = _dot(tt, bd(wy[:cl]))
        q = q[:cl] + q[cl:2 * cl]
        yield
        s2 = s_ref[p] if chained else _load_head_pair(s0_ref, si, p)
        res = _dot_nt(lhs(*split(pmat), rt), _pair_bd(b16(s2), head_a_pad))
        yield
        u = res[:cl] + res[cl:2 * cl] + q
        y = res[2 * cl:3 * cl] + _dot(lhs(mrb), bd(u))[:cl] + wy[cl:2 * cl]
        v_hi, v_lo = split(vm)
        kg_hi, kg_lo = split(km * e_end)
        full = _dot_tn(lhs(u, v_hi, v_lo, v_hi), lhs(bv * e_end, kg_hi, kg_hi, kg_lo))
        yield
        s_new = s2 * jnp.exp(g_end) + jnp.where(head_a_pad, full[:RWKV_HEAD_DIM], full[RWKV_HEAD_DIM:])
        if chained:
            s_ref[p] = s_new
        else:
            _store_head_pair(sl_ref, si, p, s_new)
        d = y - _head_sum(y, bo) * inv
        yield
        yn = d * lax.rsqrt(_head_sum(d * d, bo) * inv + RWKV_LN_EPS) * lnw_p[:, sl] + lnb_p[:, sl]
        bonus = _head_sum(r * km * rk_p[:, sl], bo) * vm
        yraw_s[rsl, sl] = (yn + bonus) * g_ref[rsl, sl]

    inv = 1.0 / RWKV_HEAD_DIM
    live = [pair_stages(si, p) for si in range(nsb) for p in range(RWKV_PG)]
    while live:
        live = [gen for gen in live if next(gen, True) is None]
    o_ref[...] = yraw_s[...].astype(o_ref.dtype)

    if chained:
        @pl.when(c == RWKV_PROMPT_CHUNKS - 1)
        def _():
            for p in range(RWKV_PG):
                _store_head_pair(sl_ref, 0, p, s_ref[p])


RWKV_SAMPLE_SEQS = 4
RWKV_SAMPLE_ROWS = RWKV_SAMPLE_SEQS * SAMPLE_ROWS_PER_SEQ


def _rwkv_core(r, k, v, dec, a, g, valid, k_k, k_a, r_k, ln_w, ln_b, s0, *, prompt, y_init=None):
    nseq = s0.shape[0]
    vec = lambda a_: a_.reshape(1, D_MODEL)
    if prompt:
        rows = RWKV_C
        grid = (nseq, RWKV_NPG, RWKV_PROMPT_CHUNKS)
        rmap = lambda b, q, c: (_prompt_blk64(b, c), q)
        vmap = lambda b, q, c: (_prompt_blk64(b, c), 0)
        spb = 1
        kernel_fn = functools.partial(_rwkv_chunk_kernel, rows=RWKV_C, nsb=1, chained=True)
    else:
        rows = RWKV_SAMPLE_ROWS
        grid = (nseq // RWKV_SAMPLE_SEQS, RWKV_NPG, 1)
        rmap = lambda b, q, c: (b, q)
        vmap = lambda b, q, c: (b, 0)
        spb = RWKV_SAMPLE_SEQS
        kernel_fn = functools.partial(_rwkv_chunk_kernel, rows=SAMPLE_ROWS_PER_SEQ, nsb=spb,
                                      chained=False)
    scratch = [pltpu.VMEM((RWKV_PG, RWKV_HEAD_DIM, LANES), F32), pltpu.VMEM((rows, RWKV_PG_W), F32)]
    sem = ("parallel", "parallel", "arbitrary")
    row_spec = pl.BlockSpec((rows, RWKV_PG_W), rmap)
    vec_spec = pl.BlockSpec((1, RWKV_PG_W), lambda b, q, c: (0, q))
    st_spec = pl.BlockSpec((spb, 2 * RWKV_PG, RWKV_HEAD_DIM, RWKV_HEAD_DIM), lambda b, q, c: (b, q, 0, 0))
    in_specs = [row_spec] * 6 + [pl.BlockSpec((rows, 1), vmap)] + [vec_spec] * 5 + [st_spec]
    args = [r, k, v, dec, a, g, valid, vec(k_k), vec(k_a), vec(r_k), vec(ln_w), vec(ln_b), s0]
    aliases = {}
    if y_init is not None:
        in_specs.append(pl.BlockSpec(memory_space=pl.ANY))
        args.append(y_init)
        aliases = {len(args) - 1: 0}
        kernel_fn = _drop_ref(kernel_fn, len(args) - 1)
    return pl.pallas_call(
        kernel_fn,
        grid=grid,
        in_specs=in_specs,
        out_specs=[row_spec, st_spec],
        input_output_aliases=aliases,
        out_shape=[jax.ShapeDtypeStruct((T_ROWS, D_MODEL), BF16), jax.ShapeDtypeStruct(s0.shape, F32)],
        scratch_shapes=scratch,
        compiler_params=_cparams(*sem),
        name="rwkv_core_prompt" if prompt else "rwkv_core_sample",
    )(*args)


def _row_ids():
    sample_t0 = jnp.arange(DEC_BATCH) * SAMPLE_ROWS_PER_SEQ
    main0 = MAIN_BLK0 * ROW_BLK + jnp.arange(BATCH) * SEQ
    meta0 = META_BLK0 * ROW_BLK + jnp.arange(BATCH) * ROW_BLK + META_PAD
    return sample_t0, main0, meta0


def _valid_mask():
    r = jnp.arange(T_ROWS)
    sample = (r < MAIN_BLK0 * ROW_BLK) & (r % SAMPLE_ROWS_PER_SEQ < DEC_SEQ)
    main = (r >= MAIN_BLK0 * ROW_BLK) & (r < META_BLK0 * ROW_BLK)
    meta = (r >= META_BLK0 * ROW_BLK) & (r % ROW_BLK >= META_PAD)
    return (sample | main | meta).astype(F32)


def _mixer_out_init(width):
    return jnp.zeros((T_ROWS, width), BF16)


def _pad_conv_prev(prev):
    return jnp.pad(prev, ((0, 0), (SUBLANES - SSD_CONV_K + 1, 0), (0, 0)))


def _ssd_layer(h, valid, norm_w, w_in, conv_w, conv_b, dt_bias, a_log, d_skip, gnorm_w, w_out,
               state_all, layer, state_conv, prev_states=None):
    sample_t0, main0, meta0 = _row_ids()
    w_in_b = w_in.astype(BF16)
    proj = _matmul(h, w_in_b, SSD_ZX_DIM, norm_w=norm_w, tn=MM_TN_WIDE)
    dtt = _ssd_dt(h, norm_w, w_in_b[:, SSD_ZX_DIM:].T, dt_bias)
    xbc = lambda rows: proj[rows][..., SSD_D_INNER:]
    tail = jnp.arange(SSD_CONV_K - 1)
    validt = valid.reshape(N_BLKS, 1, ROW_BLK)
    dtt_p = jnp.transpose(dtt.reshape(SSD_GROUPS, SSD_HPG, N_BLKS, ROW_BLK), (2, 0, 1, 3))
    gn = gnorm_w.reshape(1, SSD_D_INNER)
    cb = conv_b.reshape(1, SSD_CONV_DIM)
    h0_meta = jnp.zeros((1, BATCH, SSD_HEADS, SSD_HEAD_DIM, SSD_STATE), F32)
    prev_p = jnp.zeros((BATCH, SUBLANES, SSD_CONV_DIM), F32)
    y, p_state = _ssd_core(proj, valid.reshape(T_ROWS, 1), validt, dtt_p, a_log, d_skip, conv_w, cb,
                           prev_p, gn, h0_meta, prompt=True, y_init=_mixer_out_init(SSD_D_INNER))
    ns = SAMPLE_ROWS_PER_SEQ
    dtt_s = dtt[:, :DEC_BATCH * ns].reshape(SSD_GROUPS, SSD_HPG, DEC_BATCH, ns)
    dtt_s = jnp.pad(jnp.transpose(dtt_s, (2, 0, 1, 3)), ((0, 0), (0, 0), (0, 0), (0, SSD_Q - ns)))
    validt_s = jnp.pad(valid[:DEC_BATCH * ns].reshape(DEC_BATCH, 1, ns), ((0, 0), (0, 0), (0, SSD_Q - ns)))
    y, s_state = _ssd_core(proj, valid.reshape(T_ROWS, 1), validt_s, dtt_s, a_log, d_skip, conv_w, cb,
                           _pad_conv_prev(state_conv), gn, state_all, prompt=False, y_init=y,
                           h0_layer=layer, prev_states=prev_states)
    h = _matmul(y, w_out.astype(BF16), D_MODEL, res=h, tn=MM_TN_MID)
    p_conv = xbc(main0[:, None] + SEQ - (SSD_CONV_K - 1) + tail[None, :])
    s_conv = xbc(sample_t0[:, None] + DEC_SEQ - (SSD_CONV_K - 1) + tail[None, :])
    return h, p_state[0], p_conv, s_state, s_conv


def _s5_layer(h, valid, norm_w, w_in, lam_re, lam_im, log_dt, b_re, b_im, c_re, c_im, d_skip, w_out,
              state_re, state_im):
    v = _matmul(h, w_in.astype(BF16), D_MODEL, norm_w=norm_w, tn=MM_TN_MID)
    pow_re, pow_im, bb_re, bb_im = _s5_params(lam_re, lam_im, log_dt, b_re, b_im)
    wi_re, wo_re = _s5_block_diag(bb_re, c_re)
    wi_im, wo_im = _s5_block_diag(bb_im, c_im)
    vcol = valid.reshape(T_ROWS, 1)
    zero = jnp.zeros((BATCH, 1, S5_HDIM), F32)
    y, p_re, p_im = _s5_core(v, vcol, wi_re, wi_im, wo_re, wo_im, pow_re, pow_im, d_skip, zero, zero,
                             prompt=True, y_init=_mixer_out_init(D_MODEL))
    y, s_re, s_im = _s5_core(v, vcol, wi_re, wi_im, wo_re, wo_im, pow_re, pow_im, d_skip,
                             state_re.reshape(DEC_BATCH, 1, S5_HDIM),
                             state_im.reshape(DEC_BATCH, 1, S5_HDIM), prompt=False, y_init=y)
    h = _matmul(y, w_out.astype(BF16), D_MODEL, res=h, glu_off=D_MODEL // MM_TN_MID, tn=MM_TN_MID)
    shp = lambda a, n: a.reshape(n, S5_GROUPS, S5_STATE)
    return h, shp(p_re, BATCH), shp(p_im, BATCH), shp(s_re, DEC_BATCH), shp(s_im, DEC_BATCH)


def _pad_lora(w_down, w_up):
    n = w_down.shape[1]
    return (jnp.pad(w_down, ((0, 0), (0, RWKV_LORA_PAD - n))).astype(BF16),
            jnp.pad(w_up, ((0, RWKV_LORA_PAD - n), (0, 0))).astype(BF16))


def _rwkv_layer(h, valid, norm_w, mu, wr, wk, wv, wo, w0, w1, w2, a0, a1, a2, g1, g2, k_k, k_a, r_k,
                ln_w, ln_b, state, state_shift):
    sample_t0, main0, meta0 = _row_ids()
    u = _rmsnorm(h, norm_w)
    prev = jnp.concatenate([jnp.zeros((1, D_MODEL), F32), u[:-1]], axis=0)
    prev = prev.at[sample_t0].set(state_shift)
    prev = prev.at[main0].set(u[meta0 + N_META - 1])
    prev = prev.at[meta0].set(0.0)
    r, k, v = _rwkv_proj(u, prev, mu, wr.astype(BF16), wk.astype(BF16), wv.astype(BF16))
    w1p, w2p = _pad_lora(w1, w2)
    a1p, a2p = _pad_lora(a1, a2)
    dec, a, g = _rwkv_lora(u, prev, mu, w1p, w2p, a1p, a2p, g1.astype(BF16), g2.astype(BF16), w0, a0)
    vcol = valid.reshape(T_ROWS, 1)
    s0_p = jnp.zeros((BATCH, RWKV_HEADS, RWKV_HEAD_DIM, RWKV_HEAD_DIM), F32)
    y, p_state = _rwkv_core(r, k, v, dec, a, g, vcol, k_k, k_a, r_k, ln_w, ln_b, s0_p, prompt=True,
                            y_init=_mixer_out_init(D_MODEL))
    y, s_state = _rwkv_core(r, k, v, dec, a, g, vcol, k_k, k_a, r_k, ln_w, ln_b, state,
                            prompt=False, y_init=y)
    h = _matmul(y, wo.astype(BF16), D_MODEL, res=h, tn=MM_TN_MID)
    return h, p_state, u[main0 + SEQ - 1], s_state, u[sample_t0 + DEC_SEQ - 1]


def kernel(x_prompt, x_sample, state_ssd, state_ssd_conv, state_s5_re, state_s5_im, state_rwkv,
           state_rwkv_shift, meta_tokens, norm1_w, norm2_w, normf_w, ffn_w_gu, ffn_w_down,
           ssd_w_in, ssd_conv_w, ssd_conv_b, ssd_dt_bias, ssd_a_log, ssd_d, ssd_norm_w, ssd_w_out,
           s5_w_in, s5_lam_re, s5_lam_im, s5_log_dt, s5_b_re, s5_b_im, s5_c_re, s5_c_im, s5_d, s5_w_out,
           rwkv_mu, rwkv_wr, rwkv_wk, rwkv_wv, rwkv_wo, rwkv_w0, rwkv_w1, rwkv_w2, rwkv_a0, rwkv_a1,
           rwkv_a2, rwkv_g1, rwkv_g2, rwkv_k_k, rwkv_k_a, rwkv_r_k, rwkv_ln_w, rwkv_ln_b):
    valid = _valid_mask()
    sample = jnp.pad(x_sample, ((0, 0), (0, SAMPLE_ROWS_PER_SEQ - DEC_SEQ), (0, 0)))
    meta = jnp.pad(jnp.broadcast_to(meta_tokens[None], (BATCH, N_META, D_MODEL)),
                   ((0, 0), (META_PAD, 0), (0, 0)))
    h = jnp.concatenate([sample.reshape(-1, D_MODEL), x_prompt.reshape(-1, D_MODEL),
                         meta.reshape(-1, D_MODEL)], axis=0).astype(F32)
    outs = {name: [] for name in ("p_ssd", "p_conv", "p_s5r", "p_s5i", "p_rwkv", "p_shift",
                                  "s_conv", "s_s5r", "s_s5i", "s_rwkv", "s_shift")}
    s_ssd = None
    for i in range(DEPTH):
        kind, j = i % N_MIXERS, i // N_MIXERS
        if kind == 0:
            h, ps, pc, s_ssd, sc = _ssd_layer(
                h, valid, norm1_w[i], ssd_w_in[j], ssd_conv_w[j], ssd_conv_b[j], ssd_dt_bias[j],
                ssd_a_log[j], ssd_d[j], ssd_norm_w[j], ssd_w_out[j], state_ssd, j, state_ssd_conv[j],
                prev_states=s_ssd)
            outs["p_ssd"].append(ps); outs["p_conv"].append(pc)
            outs["s_conv"].append(sc)
        elif kind == 1:
            h, pr, pi, sr, si = _s5_layer(
                h, valid, norm1_w[i], s5_w_in[j], s5_lam_re[j], s5_lam_im[j], s5_log_dt[j], s5_b_re[j],
                s5_b_im[j], s5_c_re[j], s5_c_im[j], s5_d[j], s5_w_out[j], state_s5_re[j], state_s5_im[j])
            outs["p_s5r"].append(pr); outs["p_s5i"].append(pi)
            outs["s_s5r"].append(sr); outs["s_s5i"].append(si)
        else:
            h, ps, psh, ss, ssh = _rwkv_layer(
                h, valid, norm1_w[i], rwkv_mu[j], rwkv_wr[j], rwkv_wk[j], rwkv_wv[j], rwkv_wo[j],
                rwkv_w0[j], rwkv_w1[j], rwkv_w2[j], rwkv_a0[j], rwkv_a1[j], rwkv_a2[j], rwkv_g1[j],
                rwkv_g2[j], rwkv_k_k[j], rwkv_k_a[j], rwkv_r_k[j], rwkv_ln_w[j], rwkv_ln_b[j],
                state_rwkv[j], state_rwkv_shift[j])
            outs["p_rwkv"].append(ps); outs["p_shift"].append(psh)
            outs["s_rwkv"].append(ss); outs["s_shift"].append(ssh)
        h = _ffn(h, norm2_w[i], ffn_w_gu[i].astype(BF16), ffn_w_down[i].astype(BF16))
    yf = _rmsnorm(h, normf_w)
    y_sample = yf[:DEC_BATCH * SAMPLE_ROWS_PER_SEQ].reshape(DEC_BATCH, SAMPLE_ROWS_PER_SEQ, D_MODEL)
    y_prompt = yf[MAIN_BLK0 * ROW_BLK:META_BLK0 * ROW_BLK].reshape(BATCH, SEQ, D_MODEL)
    st = lambda name: jnp.stack(outs[name])
    return (y_prompt, y_sample[:, :DEC_SEQ],
            st("p_ssd"), st("p_conv"), st("p_s5r"), st("p_s5i"), st("p_rwkv"), st("p_shift"),
            s_ssd, st("s_conv"), st("s_s5r"), st("s_s5i"), st("s_rwkv"), st("s_shift"))
```

```python
import functools

import jax
import jax.numpy as jnp
from jax import lax
from jax.experimental import pallas as pl
from jax.experimental.pallas import tpu as pltpu

F32 = jnp.float32
BF16 = jnp.bfloat16

D_MODEL = 2048
BATCH = 4
SEQ = 2048
DEPTH = 4
DEC_BATCH = 128
DEC_SEQ = 4
N_META = 16
N_MIXERS = 3
NORM_EPS = 1e-6

SSD_D_INNER = 2 * D_MODEL
SSD_HEAD_DIM = 64
SSD_HEADS = SSD_D_INNER // SSD_HEAD_DIM
SSD_STATE = 128
SSD_GROUPS = 8
SSD_HPG = SSD_HEADS // SSD_GROUPS
SSD_GN = SSD_GROUPS * SSD_STATE
SSD_CONV_K = 4
SSD_CONV_DIM = SSD_D_INNER + 2 * SSD_GN
SSD_ZX_DIM = SSD_D_INNER + SSD_CONV_DIM
SSD_GROUP_W = SSD_D_INNER // SSD_GROUPS

S5_GROUP_SIZE = 16
S5_GROUPS = D_MODEL // S5_GROUP_SIZE
S5_STATE = 64
S5_HDIM = S5_GROUPS * S5_STATE
S5_SBLK = 1024
S5_NSB = S5_HDIM // S5_SBLK
S5_VBLK = S5_SBLK // S5_STATE * S5_GROUP_SIZE

RWKV_HEAD_DIM = 64
RWKV_HEADS = D_MODEL // RWKV_HEAD_DIM
RWKV_PAIRS = RWKV_HEADS // 2
RWKV_LORA_PAD = 128
RWKV_G_LORA = 256
RWKV_LN_EPS = 64e-5

FFN_HIDDEN = -(-8 * D_MODEL // (3 * 256)) * 256

LANES = 128
SUBLANES = 8
ROW_BLK = 128
SAMPLE_ROWS_PER_SEQ = SUBLANES
SAMPLE_BLKS = DEC_BATCH * SAMPLE_ROWS_PER_SEQ // ROW_BLK
MAIN_BLK0 = SAMPLE_BLKS
MAIN_BLKS_PER_B = SEQ // ROW_BLK
META_BLK0 = MAIN_BLK0 + BATCH * MAIN_BLKS_PER_B
N_BLKS = META_BLK0 + BATCH
T_ROWS = N_BLKS * ROW_BLK
PROMPT_CHUNKS = 1 + MAIN_BLKS_PER_B
META_PAD = ROW_BLK - N_META

MM_TM = 512
MM_TN = 512
MM_TN_MID = 1024
MM_TN_WIDE = 2048
VMEM_LIMIT = 48 * 2 ** 20


def _prompt_blk(b, c):
    return jnp.where(c == 0, META_BLK0 + b, MAIN_BLK0 + MAIN_BLKS_PER_B * b + c - 1)


def _cparams(*sem):
    return pltpu.CompilerParams(dimension_semantics=sem, vmem_limit_bytes=VMEM_LIMIT)


def _silu(x):
    return x * (0.5 + 0.5 * jnp.tanh(0.5 * x))


def _softplus(x):
    return jnp.maximum(x, 0.0) + jnp.log(1.0 + jnp.exp(-jnp.abs(x)))


def _rms(x, w):
    return x * lax.rsqrt(jnp.mean(x * x, axis=-1, keepdims=True) + NORM_EPS) * w


def _dot(a, b):
    return jnp.dot(a, b, preferred_element_type=F32)


def _dot_nt(a, b):
    return lax.dot_general(a, b, (((1,), (1,)), ((), ())), preferred_element_type=F32)


def _dot_tn(a, b):
    return lax.dot_general(a, b, (((0,), (0,)), ((), ())), preferred_element_type=F32)


def _split_bf16(x):
    hi = x.astype(BF16)
    return hi, (x - hi.astype(F32)).astype(BF16)


def _drop_ref(kernel_fn, idx):
    def wrapped(*refs):
        return kernel_fn(*refs[:idx], *refs[idx + 1:])
    return wrapped


def _mm_kernel(*refs, normalize, glu, has_res):
    it = iter(refs)
    x_ref = next(it)
    nw_ref = next(it) if normalize else None
    w_ref = next(it)
    w2_ref = next(it) if glu else None
    res_ref = next(it) if has_res else None
    o_ref = next(it)
    xn_ref = next(it) if normalize else None
    if normalize:
        @pl.when(pl.program_id(1) == 0)
        def _():
            xn_ref[...] = _rms(x_ref[...], nw_ref[...]).astype(BF16)
        xb = xn_ref[...]
    else:
        xb = x_ref[...].astype(BF16)
    acc = _dot(xb, w_ref[...])
    if glu:
        acc = acc * jax.nn.sigmoid(_dot(xb, w2_ref[...]))
    if has_res:
        acc = acc + res_ref[...]
    o_ref[...] = acc.astype(o_ref.dtype)


def _matmul(x, w, n_out, *, norm_w=None, res=None, glu_off=None, out_dtype=F32, tn=MM_TN):
    m, k = x.shape
    normalize = norm_w is not None
    glu = glu_off is not None
    has_res = res is not None
    in_specs = [pl.BlockSpec((MM_TM, k), lambda i, j: (i, 0))]
    args = [x]
    if normalize:
        in_specs.append(pl.BlockSpec((1, k), lambda i, j: (0, 0)))
        args.append(norm_w.reshape(1, k))
    in_specs.append(pl.BlockSpec((k, tn), lambda i, j: (0, j)))
    args.append(w)
    if glu:
        in_specs.append(pl.BlockSpec((k, tn), lambda i, j: (0, j + glu_off)))
        args.append(w)
    if has_res:
        in_specs.append(pl.BlockSpec((MM_TM, tn), lambda i, j: (i, j)))
        args.append(res)
    return pl.pallas_call(
        functools.partial(_mm_kernel, normalize=normalize, glu=glu, has_res=has_res),
        grid=(m // MM_TM, n_out // tn),
        in_specs=in_specs,
        out_specs=pl.BlockSpec((MM_TM, tn), lambda i, j: (i, j)),
        out_shape=jax.ShapeDtypeStruct((m, n_out), out_dtype),
        scratch_shapes=[pltpu.VMEM((MM_TM, k), BF16)] if normalize else [],
        compiler_params=_cparams("parallel", "arbitrary"),
        name="matmul_n%d%s%s%s" % (n_out, "_norm" * normalize, "_glu" * glu, "_res" * has_res),
    )(*args)


def _rmsnorm_kernel(x_ref, w_ref, o_ref):
    o_ref[...] = _rms(x_ref[...], w_ref[...])


def _rmsnorm(x, w):
    m, k = x.shape
    return pl.pallas_call(
        _rmsnorm_kernel,
        grid=(m // MM_TM,),
        in_specs=[pl.BlockSpec((MM_TM, k), lambda i: (i, 0)), pl.BlockSpec((1, k), lambda i: (0, 0))],
        out_specs=pl.BlockSpec((MM_TM, k), lambda i: (i, 0)),
        out_shape=jax.ShapeDtypeStruct((m, k), F32),
        compiler_params=_cparams("parallel"),
        name="rmsnorm",
    )(x, w.reshape(1, k))


FFN_TF = 512
FFN_NF = FFN_HIDDEN // FFN_TF


def _ffn_kernel(x_ref, nw_ref, wg_ref, wu_ref, wd_ref, o_ref, xn_ref):
    f = pl.program_id(1)

    @pl.when(f == 0)
    def _():
        x = x_ref[...]
        xn_ref[...] = _rms(x, nw_ref[...]).astype(BF16)
        o_ref[...] = x

    xb = xn_ref[...]
    act = _silu(_dot(xb, wg_ref[...])) * _dot(xb, wu_ref[...])
    o_ref[...] += _dot(act.astype(BF16), wd_ref[...])


def _ffn(h, norm_w, w_gu, w_down):
    m = h.shape[0]
    return pl.pallas_call(
        _ffn_kernel,
        grid=(m // MM_TM, FFN_NF),
        in_specs=[
            pl.BlockSpec((MM_TM, D_MODEL), lambda i, f: (i, 0)),
            pl.BlockSpec((1, D_MODEL), lambda i, f: (0, 0)),
            pl.BlockSpec((D_MODEL, FFN_TF), lambda i, f: (0, f)),
            pl.BlockSpec((D_MODEL, FFN_TF), lambda i, f: (0, f + FFN_NF)),
            pl.BlockSpec((FFN_TF, D_MODEL), lambda i, f: (f, 0)),
        ],
        out_specs=pl.BlockSpec((MM_TM, D_MODEL), lambda i, f: (i, 0)),
        out_shape=jax.ShapeDtypeStruct((m, D_MODEL), F32),
        scratch_shapes=[pltpu.VMEM((MM_TM, D_MODEL), BF16)],
        compiler_params=_cparams("parallel", "arbitrary"),
        name="ffn",
    )(h, norm_w.reshape(1, D_MODEL), w_gu, w_gu, w_down)


def _dt_kernel(x_ref, nw_ref, wt_ref, bias_ref, o_ref):
    xb = _rms(x_ref[...], nw_ref[...]).astype(BF16)
    o_ref[...] = _softplus(_dot_nt(wt_ref[...], xb) + bias_ref[...])


def _ssd_dt(h, norm_w, w_dt_t, dt_bias):
    m = h.shape[0]
    return pl.pallas_call(
        _dt_kernel,
        grid=(m // MM_TM,),
        in_specs=[
            pl.BlockSpec((MM_TM, D_MODEL), lambda i: (i, 0)),
            pl.BlockSpec((1, D_MODEL), lambda i: (0, 0)),
            pl.BlockSpec((SSD_HEADS, D_MODEL), lambda i: (0, 0)),
            pl.BlockSpec((SSD_HEADS, 1), lambda i: (0, 0)),
        ],
        out_specs=pl.BlockSpec((SSD_HEADS, MM_TM), lambda i: (0, i)),
        out_shape=jax.ShapeDtypeStruct((SSD_HEADS, m), F32),
        compiler_params=_cparams("parallel"),
        name="ssd_dt",
    )(h, norm_w.reshape(1, D_MODEL), w_dt_t, dt_bias.reshape(SSD_HEADS, 1))


SSD_Q = ROW_BLK
SSD_SAMPLE_SEQS = 4
SSD_MIN_COLS = 16
SSD_PROMPT_GROUPS = 8
SSD_SAMPLE_GROUPS = 2


def _ssd_kernel(z_ref, x_ref, b_ref, c_ref, valid_ref, validt_ref, dtt_ref, alog_ref, dskip_ref,
                cwx_ref, cwb_ref, cwc_ref, cbx_ref, cbb_ref, cbc_ref, cpx_ref, cpb_ref, cpc_ref,
                nw_ref, h0_ref, *rest, rows, nch, nsb, ngb, n_prev):
    c = pl.program_id(2)
    q = max(rows, SSD_MIN_COLS)
    chained = nch > 1
    if n_prev:
        pst_ref, y_ref, hl_all_ref, st_ref, xpx_ref, xpb_ref, xpc_ref = rest
        hl_all_ref[0:n_prev] = pst_ref[...]
        hl_ref = hl_all_ref.at[n_prev]
    else:
        y_ref, hl_all_ref, st_ref, xpx_ref, xpb_ref, xpc_ref = rest
        hl_ref = hl_all_ref.at[0]

    if chained:
        @pl.when(c == 0)
        def _():
            st_ref[...] = h0_ref[0]
            xpx_ref[0, 0:SUBLANES] = cpx_ref[0]
            xpb_ref[0, 0:SUBLANES] = cpb_ref[0]
            xpc_ref[0, 0:SUBLANES] = cpc_ref[0]

    def pad_rows(x):
        if rows == q:
            return x
        return jnp.concatenate([x, jnp.zeros((q - rows, x.shape[1]), x.dtype)], axis=0)

    def conv(si, raw, xp_ref, cp_ref, w_ref, bias_ref):
        if not chained:
            xp_ref[si, 0:SUBLANES] = cp_ref[si]
        xp_ref[si, SUBLANES:SUBLANES + rows] = raw
        acc = bias_ref[...]
        for k in range(SSD_CONV_K):
            acc = acc + w_ref[k:k + 1, :] * xp_ref[si, pl.ds(SUBLANES - SSD_CONV_K + 1 + k, rows), :]
        if chained:
            xp_ref[si, 0:SUBLANES] = raw[rows - SUBLANES:rows]
        return _silu(acc)

    lane = lax.broadcasted_iota(jnp.int32, (SSD_HPG, SSD_Q), 1)
    row_i = lax.broadcasted_iota(jnp.int32, (rows, q), 0)
    col_i = lax.broadcasted_iota(jnp.int32, (rows, q), 1)
    eye = row_i == col_i
    causal = row_i >= col_i
    head_a = lax.broadcasted_iota(jnp.int32, (1, LANES), 1) < SSD_HEAD_DIM
    row_a = lax.broadcasted_iota(jnp.int32, (2 * SSD_HEAD_DIM, 1), 0) < SSD_HEAD_DIM

    def to_col(rowvec):
        return jnp.sum(jnp.where(eye, rowvec, 0.0), axis=1, keepdims=True)

    def pair_stages(si, gi, p, xs, bm_pad, cm, cb, cum, dtt, ys):
        dskip = dskip_ref[gi]
        hsl = slice(gi * SSD_HPG + 2 * p, gi * SSD_HPG + 2 * p + 2)
        xh = xs[:, LANES * p:LANES * (p + 1)]
        dt_cols = [to_col(dtt[r:r + 1, :]) for r in (2 * p, 2 * p + 1)]
        xdt = xh * jnp.where(head_a, dt_cols[0], dt_cols[1])
        xhb = pad_rows(xdt).astype(BF16)
        st = st_ref[hsl] if chained else h0_ref[si, hsl]
        st = st.reshape(2 * SSD_HEAD_DIM, SSD_STATE)
        yo = _dot_nt(cm, st.astype(BF16))
        yd, e_col, w_col, e_last = [], [], [], []
        for r in (2 * p, 2 * p + 1):
            cum_row = cum[r:r + 1, :]
            cum_col = to_col(cum_row)
            lmat = jnp.exp(jnp.where(causal, cum_col - cum_row, -jnp.inf))
            yd.append(_dot((cb * lmat).astype(BF16), xhb))
            c_last = cum_row[:, q - 1:q]
            e_col.append(jnp.exp(cum_col))
            w_col.append(jnp.exp(c_last - cum_col))
            e_last.append(jnp.exp(c_last))
        yield
        d2 = jnp.where(head_a, dskip[:, 2 * p:2 * p + 1], dskip[:, 2 * p + 1:2 * p + 2])
        ys[p] = (jnp.where(head_a, yd[0], yd[1]) + yo * jnp.where(head_a, e_col[0], e_col[1]) + xh * d2)
        xw = pad_rows(xdt * jnp.where(head_a, w_col[0], w_col[1])).astype(BF16)
        st_new = st * jnp.where(row_a, e_last[0], e_last[1]) + _dot_tn(xw, bm_pad)
        yield
        st_new = st_new.reshape(2, SSD_HEAD_DIM, SSD_STATE)
        if chained:
            st_ref[hsl] = st_new
        else:
            hl_ref[si, hsl] = st_new

    gw = SSD_GROUP_W
    live, outs = [], []
    for si in range(nsb):
        rsl = slice(si * rows, (si + 1) * rows)
        valid = valid_ref[rsl]
        xs_all = conv(si, x_ref[rsl] * valid, xpx_ref, cpx_ref, cwx_ref, cbx_ref)
        bm_all = conv(si, b_ref[rsl] * valid, xpb_ref, cpb_ref, cwb_ref, cbb_ref)
        cm_all = conv(si, c_ref[rsl] * valid, xpc_ref, cpc_ref, cwc_ref, cbc_ref)
        seq_ys = []
        for gi in range(ngb):
            bm_pad = pad_rows(bm_all[:, SSD_STATE * gi:SSD_STATE * (gi + 1)]).astype(BF16)
            cm = cm_all[:, SSD_STATE * gi:SSD_STATE * (gi + 1)].astype(BF16)
            dtt = dtt_ref[si, gi] * validt_ref[si]
            cum = dtt * -jnp.exp(alog_ref[gi])
            s = 1
            while s < q:
                cum = cum + jnp.where(lane >= s, pltpu.roll(cum, s, axis=1), 0.0)
                s *= 2
            dtt, cum = dtt[:, :q], cum[:, :q]
            cb = _dot_nt(cm, bm_pad)
            ys = [None] * (SSD_HPG // 2)
            seq_ys.append(ys)
            xs = xs_all[:, gw * gi:gw * (gi + 1)]
            live += [pair_stages(si, gi, p, xs, bm_pad, cm, cb, cum, dtt, ys)
                     for p in range(SSD_HPG // 2)]
        outs.append(seq_ys)
    while live:
        live = [gen for gen in live if next(gen, True) is None]
    for gi in range(ngb):
        gsl = slice(gw * gi, gw * (gi + 1))
        y = jnp.concatenate([jnp.concatenate(seq_ys[gi], axis=1) for seq_ys in outs], axis=0)
        y = y * _silu(z_ref[:, gsl])
        y_ref[:, gsl] = _rms(y, nw_ref[:, gsl]).astype(y_ref.dtype)

    if chained:
        @pl.when(c == nch - 1)
        def _():
            hl_ref[0] = st_ref[...]


def _ssd_core(proj, valid, validt, dtt, a_log, d_skip, conv_w, conv_b, conv_prev8, norm_w, h0, *,
              prompt, y_init=None, h0_layer=0, prev_states=None):
    nseq = h0.shape[1]
    if prompt:
        rows, nch, nsb, ngb = ROW_BLK, PROMPT_CHUNKS, 1, SSD_PROMPT_GROUPS
        blk = _prompt_blk
    else:
        rows, nch, nsb, ngb = SAMPLE_ROWS_PER_SEQ, 1, SSD_SAMPLE_SEQS, SSD_SAMPLE_GROUPS
        blk = lambda b, c: b
    gw = SSD_GROUP_W * ngb
    sw = SSD_STATE * ngb
    hb = SSD_HPG * ngb
    x_off = SSD_D_INNER // gw
    b_off = (2 * SSD_D_INNER) // sw
    c_off = b_off + SSD_GROUPS // ngb
    cb_off = SSD_D_INNER // sw
    cc_off = cb_off + SSD_GROUPS // ngb
    row_spec = lambda w, off: pl.BlockSpec((nsb * rows, w), lambda b, g, c: (blk(b, c), g + off))
    par_spec = lambda r, w, off: pl.BlockSpec((r, w), lambda b, g, c: (0, g + off))
    prev_spec = lambda w, off: pl.BlockSpec((nsb, SUBLANES, w), lambda b, g, c: (b, 0, g + off))
    in_specs = [
        row_spec(gw, 0), row_spec(gw, x_off), row_spec(sw, b_off), row_spec(sw, c_off),
        pl.BlockSpec((nsb * rows, 1), lambda b, g, c: (blk(b, c), 0)),
        pl.BlockSpec((nsb, 1, SSD_Q), lambda b, g, c: (blk(b, c), 0, 0)),
        pl.BlockSpec((nsb, ngb, SSD_HPG, SSD_Q), lambda b, g, c: (blk(b, c), g, 0, 0)),
        pl.BlockSpec((ngb, SSD_HPG, 1), lambda b, g, c: (g, 0, 0)),
        pl.BlockSpec((ngb, 1, SSD_HPG), lambda b, g, c: (g, 0, 0)),
        par_spec(SSD_CONV_K, gw, 0), par_spec(SSD_CONV_K, sw, cb_off),
        par_spec(SSD_CONV_K, sw, cc_off),
        par_spec(1, gw, 0), par_spec(1, sw, cb_off), par_spec(1, sw, cc_off),
        prev_spec(gw, 0), prev_spec(sw, cb_off), prev_spec(sw, cc_off),
        par_spec(1, gw, 0),
        pl.BlockSpec((None, nsb, hb, SSD_HEAD_DIM, SSD_STATE),
                     lambda b, g, c: (h0_layer, b, g, 0, 0)),
    ]
    args = [proj, proj, proj, proj, valid, validt, dtt,
            a_log.reshape(SSD_GROUPS, SSD_HPG, 1), d_skip.reshape(SSD_GROUPS, 1, SSD_HPG),
            conv_w, conv_w, conv_w, conv_b, conv_b, conv_b, conv_prev8, conv_prev8, conv_prev8,
            norm_w, h0]
    n_prev = 0 if prev_states is None else prev_states.shape[0]
    all_spec = lambda n: pl.BlockSpec((n, nsb, hb, SSD_HEAD_DIM, SSD_STATE),
                                      lambda b, g, c: (0, b, g, 0, 0))
    if n_prev:
        in_specs.append(all_spec(n_prev))
        args.append(prev_states)
    kernel_fn = functools.partial(_ssd_kernel, rows=rows, nch=nch, nsb=nsb, ngb=ngb, n_prev=n_prev)
    aliases = {}
    if y_init is not None:
        in_specs.append(pl.BlockSpec(memory_space=pl.ANY))
        args.append(y_init)
        aliases = {len(args) - 1: 0}
        kernel_fn = _drop_ref(kernel_fn, len(args) - 1)
    return pl.pallas_call(
        kernel_fn,
        grid=(nseq // nsb, SSD_GROUPS // ngb, nch),
        in_specs=in_specs,
        out_specs=[row_spec(gw, 0), all_spec(n_prev + 1)],
        input_output_aliases=aliases,
        out_shape=[jax.ShapeDtypeStruct((T_ROWS, SSD_D_INNER), BF16),
                   jax.ShapeDtypeStruct((n_prev + 1,) + h0.shape[1:], F32)],
        scratch_shapes=[
            pltpu.VMEM((hb, SSD_HEAD_DIM, SSD_STATE), F32),
            pltpu.VMEM((nsb, SUBLANES + rows, gw), F32),
            pltpu.VMEM((nsb, SUBLANES + rows, sw), F32),
            pltpu.VMEM((nsb, SUBLANES + rows, sw), F32),
        ],
        compiler_params=_cparams("parallel", "parallel", "arbitrary"),
        name="ssd_core_prompt" if prompt else "ssd_core_sample",
    )(*args)


def _s5_abar(lam_re, lam_im, log_dt):
    dt = jnp.exp(log_dt)
    mag = jnp.exp(lam_re * dt)
    ang = lam_im * dt
    return mag * jnp.cos(ang), mag * jnp.sin(ang)


def _s5_pow_kernel(lr_ref, li_ref, ldt_ref, pr_ref, pi_ref):
    ar, ai = _s5_abar(lr_ref[...], li_ref[...], ldt_ref[...])
    row = lax.broadcasted_iota(jnp.int32, (SUBLANES, S5_HDIM), 0)
    pr, pi = ar, ai
    out_r = jnp.broadcast_to(ar, (SUBLANES, S5_HDIM))
    out_i = jnp.broadcast_to(ai, (SUBLANES, S5_HDIM))
    for k in range(1, SUBLANES):
        pr, pi = pr * ar - pi * ai, pr * ai + pi * ar
        out_r = jnp.where(row == k, pr, out_r)
        out_i = jnp.where(row == k, pi, out_i)
    pr_ref[...] = out_r
    pi_ref[...] = out_i


def _s5_bbar_kernel(lr_ref, li_ref, ldt_ref, br_ref, bi_ref, or_ref, oi_ref):
    lr, li = lr_ref[...], li_ref[...]
    ar, ai = _s5_abar(lr, li, ldt_ref[...])
    den = lr * lr + li * li
    f_re = ((ar - 1.0) * lr + ai * li) / den
    f_im = (ai * lr - (ar - 1.0) * li) / den
    br, bi = br_ref[...], bi_ref[...]
    or_ref[...] = f_re * br - f_im * bi
    oi_ref[...] = f_re * bi + f_im * br


def _s5_params(lam_re, lam_im, log_dt, b_re, b_im):
    ldt = jnp.repeat(log_dt, S5_STATE)
    row = lambda a: a.reshape(1, S5_HDIM)
    col = lambda a: a.reshape(S5_HDIM, 1)
    full = lambda shape: pl.BlockSpec(shape, lambda: (0,) * len(shape))
    pow_re, pow_im = pl.pallas_call(
        _s5_pow_kernel,
        in_specs=[full((1, S5_HDIM))] * 3,
        out_specs=[full((SUBLANES, S5_HDIM))] * 2,
        out_shape=[jax.ShapeDtypeStruct((SUBLANES, S5_HDIM), F32)] * 2,
    )(row(lam_re), row(lam_im), row(ldt))
    bb_re, bb_im = pl.pallas_call(
        _s5_bbar_kernel,
        in_specs=[full((S5_HDIM, 1))] * 3 + [full((S5_HDIM, S5_GROUP_SIZE))] * 2,
        out_specs=[full((S5_HDIM, S5_GROUP_SIZE))] * 2,
        out_shape=[jax.ShapeDtypeStruct((S5_HDIM, S5_GROUP_SIZE), F32)] * 2,
    )(col(lam_re), col(lam_im), col(ldt), b_re.reshape(S5_HDIM, S5_GROUP_SIZE),
      b_im.reshape(S5_HDIM, S5_GROUP_SIZE))
    return pow_re, pow_im, bb_re, bb_im


def _s5_block_diag(bb, c):
    gpb = S5_SBLK // S5_STATE
    eye = jnp.eye(gpb, dtype=F32)
    bb = bb.reshape(S5_NSB, gpb, S5_STATE, S5_GROUP_SIZE)
    w_in = eye[None, :, None, :, None] * jnp.transpose(bb, (0, 3, 1, 2))[:, None]
    w_in = w_in.reshape(S5_NSB, S5_VBLK, S5_SBLK)
    c = c.reshape(S5_NSB, gpb, S5_GROUP_SIZE, S5_STATE)
    w_out = eye[None, :, None, :, None] * jnp.transpose(c, (0, 1, 3, 2))[:, :, :, None, :]
    w_out = w_out.reshape(S5_NSB, S5_SBLK, S5_VBLK)
    return w_in.astype(BF16), w_out.astype(BF16)


def _s5_kernel(v_ref, valid_ref, wir_ref, wii_ref, wor_ref, woi_ref, pr_ref, pi_ref, d_ref,
               h0r_ref, h0i_ref, y_ref, hlr_ref, hli_ref, cr_ref, ci_ref, hr_ref, hi_ref, *,
               per_tile, nch, last_row):
    ntiles = ROW_BLK // SUBLANES
    if not per_tile:
        c = pl.program_id(2)

        @pl.when(c == 0)
        def _():
            cr_ref[...] = h0r_ref[0]
            ci_ref[...] = h0i_ref[0]

    v = v_ref[...]
    vb = (v * valid_ref[...]).astype(BF16)
    bu_r = _dot(vb, wir_ref[0])
    bu_i = _dot(vb, wii_ref[0])
    pr, pi = pr_ref[...], pi_ref[...]
    row = lax.broadcasted_iota(jnp.int32, (SUBLANES, S5_SBLK), 0)
    levels = []
    for s in (1, 2, 4):
        levels.append((s, jnp.where(row >= s, pr[s - 1:s], 0.0), jnp.where(row >= s, pi[s - 1:s], 0.0)))
    if not per_tile:
        car_r, car_i = cr_ref[...], ci_ref[...]
    for i in range(ntiles):
        xr = bu_r[SUBLANES * i:SUBLANES * (i + 1)]
        xi = bu_i[SUBLANES * i:SUBLANES * (i + 1)]
        for s, ar, ai in levels:
            sr = pltpu.roll(xr, s, axis=0)
            si = pltpu.roll(xi, s, axis=0)
            xr, xi = xr + ar * sr - ai * si, xi + ar * si + ai * sr
        if per_tile:
            car_r, car_i = h0r_ref[i], h0i_ref[i]
        hr = xr + pr * car_r - pi * car_i
        hi = xi + pr * car_i + pi * car_r
        hr_ref[SUBLANES * i:SUBLANES * (i + 1)] = hr
        hi_ref[SUBLANES * i:SUBLANES * (i + 1)] = hi
        if per_tile:
            hlr_ref[i] = hr[last_row:last_row + 1]
            hli_ref[i] = hi[last_row:last_row + 1]
        else:
            car_r, car_i = hr[last_row:last_row + 1], hi[last_row:last_row + 1]
    y = (_dot(hr_ref[...].astype(BF16), wor_ref[0]) - _dot(hi_ref[...].astype(BF16), woi_ref[0])
         + d_ref[...] * v)
    y_ref[...] = jax.nn.gelu(y).astype(y_ref.dtype)
    if not per_tile:
        cr_ref[...] = car_r
        ci_ref[...] = car_i

        @pl.when(c == nch - 1)
        def _():
            hlr_ref[0] = car_r
            hli_ref[0] = car_i


def _s5_core(v, valid, w_in_re, w_in_im, w_out_re, w_out_im, pow_re, pow_im, d_skip, h0_re, h0_im, *,
             prompt, y_init=None):
    nseq = h0_re.shape[0]
    if prompt:
        grid = (nseq, S5_NSB, PROMPT_CHUNKS)
        blk = _prompt_blk
        sidx = lambda b, s, c: (b, 0, s)
        spb = 1
        kw = dict(per_tile=False, nch=PROMPT_CHUNKS, last_row=SUBLANES - 1)
        sem = ("parallel", "parallel", "arbitrary")
    else:
        grid = (SAMPLE_BLKS, S5_NSB, 1)
        blk = lambda b, c: b
        sidx = lambda b, s, c: (b, 0, s)
        spb = ROW_BLK // SAMPLE_ROWS_PER_SEQ
        kw = dict(per_tile=True, nch=1, last_row=DEC_SEQ - 1)
        sem = ("parallel", "parallel", "arbitrary")
    row_spec = pl.BlockSpec((ROW_BLK, S5_VBLK), lambda b, s, c: (blk(b, c), s))
    st_spec = pl.BlockSpec((spb, 1, S5_SBLK), sidx)
    w_in_spec = pl.BlockSpec((1, S5_VBLK, S5_SBLK), lambda b, s, c: (s, 0, 0))
    w_out_spec = pl.BlockSpec((1, S5_SBLK, S5_VBLK), lambda b, s, c: (s, 0, 0))
    pow_spec = pl.BlockSpec((SUBLANES, S5_SBLK), lambda b, s, c: (0, s))
    in_specs = [row_spec, pl.BlockSpec((ROW_BLK, 1), lambda b, s, c: (blk(b, c), 0)),
                w_in_spec, w_in_spec, w_out_spec, w_out_spec, pow_spec, pow_spec,
                pl.BlockSpec((1, S5_VBLK), lambda b, s, c: (0, s)), st_spec, st_spec]
    args = [v, valid, w_in_re, w_in_im, w_out_re, w_out_im, pow_re, pow_im,
            d_skip.reshape(1, D_MODEL), h0_re, h0_im]
    kernel_fn = functools.partial(_s5_kernel, **kw)
    aliases = {}
    if y_init is not None:
        in_specs.append(pl.BlockSpec(memory_space=pl.ANY))
        args.append(y_init)
        aliases = {len(args) - 1: 0}
        kernel_fn = _drop_ref(kernel_fn, len(args) - 1)
    return pl.pallas_call(
        kernel_fn,
        grid=grid,
        in_specs=in_specs,
        out_specs=[row_spec, st_spec, st_spec],
        input_output_aliases=aliases,
        out_shape=[jax.ShapeDtypeStruct((T_ROWS, D_MODEL), BF16),
                   jax.ShapeDtypeStruct(h0_re.shape, F32), jax.ShapeDtypeStruct(h0_im.shape, F32)],
        scratch_shapes=[pltpu.VMEM((1, S5_SBLK), F32), pltpu.VMEM((1, S5_SBLK), F32),
                        pltpu.VMEM((ROW_BLK, S5_SBLK), F32), pltpu.VMEM((ROW_BLK, S5_SBLK), F32)],
        compiler_params=_cparams(*sem),
        name="s5_core_prompt" if prompt else "s5_core_sample",
    )(*args)


def _rwkv_proj_kernel(u_ref, p_ref, mu_ref, wr_ref, wk_ref, wv_ref, r_ref, k_ref, v_ref,
                      xr_ref, xk_ref, xv_ref):
    @pl.when(pl.program_id(1) == 0)
    def _():
        u = u_ref[...]
        d = p_ref[...] - u
        xr_ref[...] = (u + d * mu_ref[0:1, :]).astype(BF16)
        xk_ref[...] = (u + d * mu_ref[2:3, :]).astype(BF16)
        xv_ref[...] = (u + d * mu_ref[3:4, :]).astype(BF16)

    r_ref[...] = _dot(xr_ref[...], wr_ref[...])
    k_ref[...] = _dot(xk_ref[...], wk_ref[...])
    v_ref[...] = _dot(xv_ref[...], wv_ref[...])


def _rwkv_proj(u, prev, mu, wr, wk, wv):
    m = u.shape[0]
    row = pl.BlockSpec((MM_TM, D_MODEL), lambda i, j: (i, 0))
    wsp = pl.BlockSpec((D_MODEL, MM_TN), lambda i, j: (0, j))
    osp = pl.BlockSpec((MM_TM, MM_TN), lambda i, j: (i, j))
    return pl.pallas_call(
        _rwkv_proj_kernel,
        grid=(m // MM_TM, D_MODEL // MM_TN),
        in_specs=[row, row, pl.BlockSpec((6, D_MODEL), lambda i, j: (0, 0)), wsp, wsp, wsp],
        out_specs=[osp, osp, osp],
        out_shape=[jax.ShapeDtypeStruct((m, D_MODEL), F32)] * 3,
        scratch_shapes=[pltpu.VMEM((MM_TM, D_MODEL), BF16)] * 3,
        compiler_params=_cparams("parallel", "arbitrary"),
        name="rwkv_proj",
    )(u, prev, mu, wr, wk, wv)


RWKV_LORA_TM = 256


def _rwkv_lora_kernel(u_ref, p_ref, mu_ref, w1_ref, w2_ref, a1_ref, a2_ref, g1_ref, g2_ref,
                      w0_ref, a0_ref, lw_ref, a_ref, g_ref):
    u = u_ref[...]
    d = p_ref[...] - u
    xw = (u + d * mu_ref[1:2, :]).astype(BF16)
    xa = (u + d * mu_ref[4:5, :]).astype(BF16)
    xg = (u + d * mu_ref[5:6, :]).astype(BF16)
    wpre = w0_ref[...] + _dot(jnp.tanh(_dot(xw, w1_ref[...])).astype(BF16), w2_ref[...])
    w = -_softplus(-wpre) - 0.5
    lw_ref[...] = -jnp.exp(w)
    a_ref[...] = jax.nn.sigmoid(a0_ref[...] + _dot(_dot(xa, a1_ref[...]).astype(BF16), a2_ref[...]))
    g_ref[...] = _dot(jax.nn.sigmoid(_dot(xg, g1_ref[...])).astype(BF16), g2_ref[...])


def _rwkv_lora(u, prev, mu, w1, w2, a1, a2, g1, g2, w0, a0):
    m = u.shape[0]
    row = pl.BlockSpec((RWKV_LORA_TM, D_MODEL), lambda i: (i, 0))
    full = lambda a: pl.BlockSpec(a.shape, lambda i: (0, 0))
    vec = pl.BlockSpec((1, D_MODEL), lambda i: (0, 0))
    return pl.pallas_call(
        _rwkv_lora_kernel,
        grid=(m // RWKV_LORA_TM,),
        in_specs=[row, row, pl.BlockSpec((6, D_MODEL), lambda i: (0, 0)),
                  full(w1), full(w2), full(a1), full(a2), full(g1), full(g2), vec, vec],
        out_specs=[row, row, row],
        out_shape=[jax.ShapeDtypeStruct((m, D_MODEL), F32)] * 3,
        compiler_params=_cparams("parallel"),
        name="rwkv_lora",
    )(u, prev, mu, w1, w2, a1, a2, g1, g2, w0.reshape(1, D_MODEL), a0.reshape(1, D_MODEL))


def _block_ones():
    r = lax.broadcasted_iota(jnp.int32, (LANES, LANES), 0) // RWKV_HEAD_DIM
    c = lax.broadcasted_iota(jnp.int32, (LANES, LANES), 1) // RWKV_HEAD_DIM
    return (r == c).astype(BF16)


def _head_sum(x, bo):
    hi, lo = _split_bf16(x)
    return _dot(hi, bo) + _dot(lo, bo)


RWKV_PG = 16
RWKV_NPG = RWKV_PAIRS // RWKV_PG
RWKV_PG_W = RWKV_PG * LANES


RWKV_C = 64
RWKV_PROMPT_CHUNKS = (ROW_BLK + SEQ) // RWKV_C


def _prompt_blk64(b, c):
    per = ROW_BLK // RWKV_C
    return jnp.where(c < per, per * (META_BLK0 + b) + c,
                     per * (MAIN_BLK0 + MAIN_BLKS_PER_B * b) + c - per)


def _load_head_pair(ref, si, p):
    return jnp.concatenate([ref[si, 2 * p], ref[si, 2 * p + 1]], axis=1)


def _store_head_pair(ref, si, p, s2):
    ref[si, 2 * p] = s2[:, :RWKV_HEAD_DIM]
    ref[si, 2 * p + 1] = s2[:, RWKV_HEAD_DIM:]


def _pair_bd(x, head_a):
    zero = jnp.zeros_like(x)
    return jnp.concatenate([jnp.where(head_a, x, zero), jnp.where(head_a, zero, x)], axis=0)


def _rwkv_chunk_kernel(r_ref, k_ref, v_ref, lw_ref, a_ref, g_ref, valid_ref, kk_p, ka_p, rk_p,
                       lnw_p, lnb_p, s0_ref, o_ref, sl_ref, s_ref, yraw_s, *, rows, nsb, chained):
    c = pl.program_id(2)
    cl = rows
    cp = RWKV_C
    levels = max(1, (cl - 1).bit_length())

    if chained:
        @pl.when(c == 0)
        def _():
            for p in range(RWKV_PG):
                s_ref[p] = _load_head_pair(s0_ref, 0, p)

    bo = _block_ones()
    lane = lax.broadcasted_iota(jnp.int32, (cl, LANES), 1)
    row = lax.broadcasted_iota(jnp.int32, (cl, LANES), 0)
    head_a = lane < RWKV_HEAD_DIM
    head_a_pad = lax.broadcasted_iota(jnp.int32, (cp, LANES), 1) < RWKV_HEAD_DIM
    s_idx = lane % RWKV_HEAD_DIM
    strict = s_idx < row
    incl = s_idx <= row
    eye2 = (s_idx == row).astype(F32)
    tri = (lax.broadcasted_iota(jnp.int32, (cl, cp), 0)
           >= lax.broadcasted_iota(jnp.int32, (cl, cp), 1)).astype(BF16)
    b16 = lambda x: x.astype(BF16)
    cat = lambda *xs: jnp.concatenate(xs, axis=0)
    halves = lambda x: x[:cl] + x[cl:]

    def pad_rows(x, n):
        if x.shape[0] == n:
            return x
        return cat(x, jnp.zeros((n - x.shape[0], x.shape[1]), x.dtype))

    def split(x):
        hi = b16(x).astype(F32)
        return hi, x - hi

    bd = lambda x: _pair_bd(b16(pad_rows(x, cp)), head_a_pad)
    lhs = lambda *xs: b16(pad_rows(cat(*xs), -(-len(xs) * cl // 16) * 16))

    def pair_stages(si, p):
        sl = slice(LANES * p, LANES * (p + 1))
        rsl = slice(si * cl, (si + 1) * cl)
        valid = valid_ref[rsl]
        k, a, r = k_ref[rsl, sl], a_ref[rsl, sl], r_ref[rsl, sl]
        vm = v_ref[rsl, sl] * valid
        lw = lw_ref[rsl, sl] * valid
        kkr = k * kk_p[:, sl]
        kk = kkr / jnp.maximum(jnp.sqrt(_head_sum(kkr * kkr, bo)), 1e-12) * valid
        bv = kk * a
        km = k * (1.0 + (a - 1.0) * ka_p[:, sl])
        lw_hi, lw_lo = split(lw)
        g = _dot(tri, b16(pad_rows(lw_hi, cp))) + _dot(tri, b16(pad_rows(lw_lo, cp)))
        yield
        g_end = g[cl - 1:cl]
        e_neg = jnp.exp(-g)
        e_end = jnp.exp(g_end - g)
        at = -kk * jnp.exp(g - lw)
        rt = r * jnp.exp(g)
        x2 = lhs(at, rt)
        gb = _dot_nt(x2, bd(bv * e_neg))
        gk = _dot_nt(x2, bd(km * e_neg))
        yield
        n = jnp.where(strict, gb[:cl], 0.0)
        mrb = jnp.where(incl, gb[cl:2 * cl], 0.0)
        lak = jnp.where(strict, gk[:cl], 0.0)
        mrk = jnp.where(incl, gk[cl:2 * cl], 0.0)
        pk, t = n, eye2
        for level in range(levels):
            if level < levels - 1:
                res = _dot(lhs(pk, t), bd(pk))
                pk, t = res[:cl], t + res[cl:2 * cl]
            else:
                t = t + _dot(lhs(t), bd(pk))[:cl]
            yield
        n_hi, n_lo = split(n)
        t_hi, t_lo = split(t)
        nt = _dot(lhs(n_hi, n_lo), bd(t_hi))
        nt = nt[:cl] + nt[cl:2 * cl] + _dot(lhs(n_hi), bd(t_lo))[:cl]
        wy = _dot(lhs(lak, mrk), bd(vm))
        yield
        t = t + _dot(lhs(t_hi), bd(eye2 - t + nt))[:cl]
        yield
        tt = lhs(*split(t))
        pmat = _dot(tt, bd(at))
        pmat = pmat[:cl] + pmat[cl:2 * cl]
        q = _dot(tt, bd(wy[:cl]))
        q = q[:cl] + q[cl:2 * cl]
        yield
        s2 = s_ref[p] if chained else _load_head_pair(s0_ref, si, p)
        res = _dot_nt(lhs(*split(pmat), rt), _pair_bd(b16(s2), head_a_pad))
        yield
        u = res[:cl] + res[cl:2 * cl] + q
        y = res[2 * cl:3 * cl] + _dot(lhs(mrb), bd(u))[:cl] + wy[cl:2 * cl]
        v_hi, v_lo = split(vm)
        kg_hi, kg_lo = split(km * e_end)
        full = _dot_tn(lhs(u, v_hi, v_lo, v_hi), lhs(bv * e_end, kg_hi, kg_hi, kg_lo))
        yield
        s_new = s2 * jnp.exp(g_end) + jnp.where(head_a_pad, full[:RWKV_HEAD_DIM], full[RWKV_HEAD_DIM:])
        if chained:
            s_ref[p] = s_new
        else:
            _store_head_pair(sl_ref, si, p, s_new)
        d = y - _head_sum(y, bo) * inv
        yield
        yn = d * lax.rsqrt(_head_sum(d * d, bo) * inv + RWKV_LN_EPS) * lnw_p[:, sl] + lnb_p[:, sl]
        bonus = _head_sum(r * km * rk_p[:, sl], bo) * vm
        yraw_s[rsl, sl] = (yn + bonus) * g_ref[rsl, sl]

    inv = 1.0 / RWKV_HEAD_DIM
    live = [pair_stages(si, p) for si in range(nsb) for p in range(RWKV_PG)]
    while live:
        live = [gen for gen in live if next(gen, True) is None]
    o_ref[...] = yraw_s[...].astype(o_ref.dtype)

    if chained:
        @pl.when(c == RWKV_PROMPT_CHUNKS - 1)
        def _():
            for p in range(RWKV_PG):
                _store_head_pair(sl_ref, 0, p, s_ref[p])


RWKV_SAMPLE_SEQS = 4
RWKV_SAMPLE_ROWS = RWKV_SAMPLE_SEQS * SAMPLE_ROWS_PER_SEQ


def _rwkv_core(r, k, v, dec, a, g, valid, k_k, k_a, r_k, ln_w, ln_b, s0, *, prompt, y_init=None):
    nseq = s0.shape[0]
    vec = lambda a_: a_.reshape(1, D_MODEL)
    if prompt:
        rows = RWKV_C
        grid = (nseq, RWKV_NPG, RWKV_PROMPT_CHUNKS)
        rmap = lambda b, q, c: (_prompt_blk64(b, c), q)
        vmap = lambda b, q, c: (_prompt_blk64(b, c), 0)
        spb = 1
        kernel_fn = functools.partial(_rwkv_chunk_kernel, rows=RWKV_C, nsb=1, chained=True)
    else:
        rows = RWKV_SAMPLE_ROWS
        grid = (nseq // RWKV_SAMPLE_SEQS, RWKV_NPG, 1)
        rmap = lambda b, q, c: (b, q)
        vmap = lambda b, q, c: (b, 0)
        spb = RWKV_SAMPLE_SEQS
        kernel_fn = functools.partial(_rwkv_chunk_kernel, rows=SAMPLE_ROWS_PER_SEQ, nsb=spb,
                                      chained=False)
    scratch = [pltpu.VMEM((RWKV_PG, RWKV_HEAD_DIM, LANES), F32), pltpu.VMEM((rows, RWKV_PG_W), F32)]
    sem = ("parallel", "parallel", "arbitrary")
    row_spec = pl.BlockSpec((rows, RWKV_PG_W), rmap)
    vec_spec = pl.BlockSpec((1, RWKV_PG_W), lambda b, q, c: (0, q))
    st_spec = pl.BlockSpec((spb, 2 * RWKV_PG, RWKV_HEAD_DIM, RWKV_HEAD_DIM), lambda b, q, c: (b, q, 0, 0))
    in_specs = [row_spec] * 6 + [pl.BlockSpec((rows, 1), vmap)] + [vec_spec] * 5 + [st_spec]
    args = [r, k, v, dec, a, g, valid, vec(k_k), vec(k_a), vec(r_k), vec(ln_w), vec(ln_b), s0]
    aliases = {}
    if y_init is not None:
        in_specs.append(pl.BlockSpec(memory_space=pl.ANY))
        args.append(y_init)
        aliases = {len(args) - 1: 0}
        kernel_fn = _drop_ref(kernel_fn, len(args) - 1)
    return pl.pallas_call(
        kernel_fn,
        grid=grid,
        in_specs=in_specs,
        out_specs=[row_spec, st_spec],
        input_output_aliases=aliases,
        out_shape=[jax.ShapeDtypeStruct((T_ROWS, D_MODEL), BF16), jax.ShapeDtypeStruct(s0.shape, F32)],
        scratch_shapes=scratch,
        compiler_params=_cparams(*sem),
        name="rwkv_core_prompt" if prompt else "rwkv_core_sample",
    )(*args)


def _row_ids():
    sample_t0 = jnp.arange(DEC_BATCH) * SAMPLE_ROWS_PER_SEQ
    main0 = MAIN_BLK0 * ROW_BLK + jnp.arange(BATCH) * SEQ
    meta0 = META_BLK0 * ROW_BLK + jnp.arange(BATCH) * ROW_BLK + META_PAD
    return sample_t0, main0, meta0


def _valid_mask():
    r = jnp.arange(T_ROWS)
    sample = (r < MAIN_BLK0 * ROW_BLK) & (r % SAMPLE_ROWS_PER_SEQ < DEC_SEQ)
    main = (r >= MAIN_BLK0 * ROW_BLK) & (r < META_BLK0 * ROW_BLK)
    meta = (r >= META_BLK0 * ROW_BLK) & (r % ROW_BLK >= META_PAD)
    return (sample | main | meta).astype(F32)


def _mixer_out_init(width):
    return jnp.zeros((T_ROWS, width), BF16)


def _pad_conv_prev(prev):
    return jnp.pad(prev, ((0, 0), (SUBLANES - SSD_CONV_K + 1, 0), (0, 0)))


def _ssd_layer(h, valid, norm_w, w_in, conv_w, conv_b, dt_bias, a_log, d_skip, gnorm_w, w_out,
               state_all, layer, state_conv, prev_states=None):
    sample_t0, main0, meta0 = _row_ids()
    w_in_b = w_in.astype(BF16)
    proj = _matmul(h, w_in_b, SSD_ZX_DIM, norm_w=norm_w, tn=MM_TN_WIDE)
    dtt = _ssd_dt(h, norm_w, w_in_b[:, SSD_ZX_DIM:].T, dt_bias)
    xbc = lambda rows: proj[rows][..., SSD_D_INNER:]
    tail = jnp.arange(SSD_CONV_K - 1)
    validt = valid.reshape(N_BLKS, 1, ROW_BLK)
    dtt_p = jnp.transpose(dtt.reshape(SSD_GROUPS, SSD_HPG, N_BLKS, ROW_BLK), (2, 0, 1, 3))
    gn = gnorm_w.reshape(1, SSD_D_INNER)
    cb = conv_b.reshape(1, SSD_CONV_DIM)
    h0_meta = jnp.zeros((1, BATCH, SSD_HEADS, SSD_HEAD_DIM, SSD_STATE), F32)
    prev_p = jnp.zeros((BATCH, SUBLANES, SSD_CONV_DIM), F32)
    y, p_state = _ssd_core(proj, valid.reshape(T_ROWS, 1), validt, dtt_p, a_log, d_skip, conv_w, cb,
                           prev_p, gn, h0_meta, prompt=True, y_init=_mixer_out_init(SSD_D_INNER))
    ns = SAMPLE_ROWS_PER_SEQ
    dtt_s = dtt[:, :DEC_BATCH * ns].reshape(SSD_GROUPS, SSD_HPG, DEC_BATCH, ns)
    dtt_s = jnp.pad(jnp.transpose(dtt_s, (2, 0, 1, 3)), ((0, 0), (0, 0), (0, 0), (0, SSD_Q - ns)))
    validt_s = jnp.pad(valid[:DEC_BATCH * ns].reshape(DEC_BATCH, 1, ns), ((0, 0), (0, 0), (0, SSD_Q - ns)))
    y, s_state = _ssd_core(proj, valid.reshape(T_ROWS, 1), validt_s, dtt_s, a_log, d_skip, conv_w, cb,
                           _pad_conv_prev(state_conv), gn, state_all, prompt=False, y_init=y,
                           h0_layer=layer, prev_states=prev_states)
    h = _matmul(y, w_out.astype(BF16), D_MODEL, res=h, tn=MM_TN_MID)
    p_conv = xbc(main0[:, None] + SEQ - (SSD_CONV_K - 1) + tail[None, :])
    s_conv = xbc(sample_t0[:, None] + DEC_SEQ - (SSD_CONV_K - 1) + tail[None, :])
    return h, p_state[0], p_conv, s_state, s_conv


def _s5_layer(h, valid, norm_w, w_in, lam_re, lam_im, log_dt, b_re, b_im, c_re, c_im, d_skip, w_out,
              state_re, state_im):
    v = _matmul(h, w_in.astype(BF16), D_MODEL, norm_w=norm_w, tn=MM_TN_MID)
    pow_re, pow_im, bb_re, bb_im = _s5_params(lam_re, lam_im, log_dt, b_re, b_im)
    wi_re, wo_re = _s5_block_diag(bb_re, c_re)
    wi_im, wo_im = _s5_block_diag(bb_im, c_im)
    vcol = valid.reshape(T_ROWS, 1)
    zero = jnp.zeros((BATCH, 1, S5_HDIM), F32)
    y, p_re, p_im = _s5_core(v, vcol, wi_re, wi_im, wo_re, wo_im, pow_re, pow_im, d_skip, zero, zero,
                             prompt=True, y_init=_mixer_out_init(D_MODEL))
    y, s_re, s_im = _s5_core(v, vcol, wi_re, wi_im, wo_re, wo_im, pow_re, pow_im, d_skip,
                             state_re.reshape(DEC_BATCH, 1, S5_HDIM),
                             state_im.reshape(DEC_BATCH, 1, S5_HDIM), prompt=False, y_init=y)
    h = _matmul(y, w_out.astype(BF16), D_MODEL, res=h, glu_off=D_MODEL // MM_TN_MID, tn=MM_TN_MID)
    shp = lambda a, n: a.reshape(n, S5_GROUPS, S5_STATE)
    return h, shp(p_re, BATCH), shp(p_im, BATCH), shp(s_re, DEC_BATCH), shp(s_im, DEC_BATCH)


def _pad_lora(w_down, w_up):
    n = w_down.shape[1]
    return (jnp.pad(w_down, ((0, 0), (0, RWKV_LORA_PAD - n))).astype(BF16),
            jnp.pad(w_up, ((0, RWKV_LORA_PAD - n), (0, 0))).astype(BF16))


def _rwkv_layer(h, valid, norm_w, mu, wr, wk, wv, wo, w0, w1, w2, a0, a1, a2, g1, g2, k_k, k_a, r_k,
                ln_w, ln_b, state, state_shift):
    sample_t0, main0, meta0 = _row_ids()
    u = _rmsnorm(h, norm_w)
    prev = jnp.concatenate([jnp.zeros((1, D_MODEL), F32), u[:-1]], axis=0)
    prev = prev.at[sample_t0].set(state_shift)
    prev = prev.at[main0].set(u[meta0 + N_META - 1])
    prev = prev.at[meta0].set(0.0)
    r, k, v = _rwkv_proj(u, prev, mu, wr.astype(BF16), wk.astype(BF16), wv.astype(BF16))
    w1p, w2p = _pad_lora(w1, w2)
    a1p, a2p = _pad_lora(a1, a2)
    dec, a, g = _rwkv_lora(u, prev, mu, w1p, w2p, a1p, a2p, g1.astype(BF16), g2.astype(BF16), w0, a0)
    vcol = valid.reshape(T_ROWS, 1)
    s0_p = jnp.zeros((BATCH, RWKV_HEADS, RWKV_HEAD_DIM, RWKV_HEAD_DIM), F32)
    y, p_state = _rwkv_core(r, k, v, dec, a, g, vcol, k_k, k_a, r_k, ln_w, ln_b, s0_p, prompt=True,
                            y_init=_mixer_out_init(D_MODEL))
    y, s_state = _rwkv_core(r, k, v, dec, a, g, vcol, k_k, k_a, r_k, ln_w, ln_b, state,
                            prompt=False, y_init=y)
    h = _matmul(y, wo.astype(BF16), D_MODEL, res=h, tn=MM_TN_MID)
    return h, p_state, u[main0 + SEQ - 1], s_state, u[sample_t0 + DEC_SEQ - 1]


def kernel(x_prompt, x_sample, state_ssd, state_ssd_conv, state_s5_re, state_s5_im, state_rwkv,
           state_rwkv_shift, meta_tokens, norm1_w, norm2_w, normf_w, ffn_w_gu, ffn_w_down,
           ssd_w_in, ssd_conv_w, ssd_conv_b, ssd_dt_bias, ssd_a_log, ssd_d, ssd_norm_w, ssd_w_out,
           s5_w_in, s5_lam_re, s5_lam_im, s5_log_dt, s5_b_re, s5_b_im, s5_c_re, s5_c_im, s5_d, s5_w_out,
           rwkv_mu, rwkv_wr, rwkv_wk, rwkv_wv, rwkv_wo, rwkv_w0, rwkv_w1, rwkv_w2, rwkv_a0, rwkv_a1,
           rwkv_a2, rwkv_g1, rwkv_g2, rwkv_k_k, rwkv_k_a, rwkv_r_k, rwkv_ln_w, rwkv_ln_b):
    valid = _valid_mask()
    sample = jnp.pad(x_sample, ((0, 0), (0, SAMPLE_ROWS_PER_SEQ - DEC_SEQ), (0, 0)))
    meta = jnp.pad(jnp.broadcast_to(meta_tokens[None], (BATCH, N_META, D_MODEL)),
                   ((0, 0), (META_PAD, 0), (0, 0)))
    h = jnp.concatenate([sample.reshape(-1, D_MODEL), x_prompt.reshape(-1, D_MODEL),
                         meta.reshape(-1, D_MODEL)], axis=0).astype(F32)
    outs = {name: [] for name in ("p_ssd", "p_conv", "p_s5r", "p_s5i", "p_rwkv", "p_shift",
                                  "s_conv", "s_s5r", "s_s5i", "s_rwkv", "s_shift")}
    s_ssd = None
    for i in range(DEPTH):
        kind, j = i % N_MIXERS, i // N_MIXERS
        if kind == 0:
            h, ps, pc, s_ssd, sc = _ssd_layer(
                h, valid, norm1_w[i], ssd_w_in[j], ssd_conv_w[j], ssd_conv_b[j], ssd_dt_bias[j],
                ssd_a_log[j], ssd_d[j], ssd_norm_w[j], ssd_w_out[j], state_ssd, j, state_ssd_conv[j],
                prev_states=s_ssd)
            outs["p_ssd"].append(ps); outs["p_conv"].append(pc)
            outs["s_conv"].append(sc)
        elif kind == 1:
            h, pr, pi, sr, si = _s5_layer(
                h, valid, norm1_w[i], s5_w_in[j], s5_lam_re[j], s5_lam_im[j], s5_log_dt[j], s5_b_re[j],
                s5_b_im[j], s5_c_re[j], s5_c_im[j], s5_d[j], s5_w_out[j], state_s5_re[j], state_s5_im[j])
            outs["p_s5r"].append(pr); outs["p_s5i"].append(pi)
            outs["s_s5r"].append(sr); outs["s_s5i"].append(si)
        else:
            h, ps, psh, ss, ssh = _rwkv_layer(
                h, valid, norm1_w[i], rwkv_mu[j], rwkv_wr[j], rwkv_wk[j], rwkv_wv[j], rwkv_wo[j],
                rwkv_w0[j], rwkv_w1[j], rwkv_w2[j], rwkv_a0[j], rwkv_a1[j], rwkv_a2[j], rwkv_g1[j],
                rwkv_g2[j], rwkv_k_k[j], rwkv_k_a[j], rwkv_r_k[j], rwkv_ln_w[j], rwkv_ln_b[j],
                state_rwkv[j], state_rwkv_shift[j])
            outs["p_rwkv"].append(ps); outs["p_shift"].append(psh)
            outs["s_rwkv"].append(ss); outs["s_shift"].append(ssh)
        h = _ffn(h, norm2_w[i], ffn_w_gu[i].astype(BF16), ffn_w_down[i].astype(BF16))
    yf = _rmsnorm(h, normf_w)
    y_sample = yf[:DEC_BATCH * SAMPLE_ROWS_PER_SEQ].reshape(DEC_BATCH, SAMPLE_ROWS_PER_SEQ, D_MODEL)
    y_prompt = yf[MAIN_BLK0 * ROW_BLK:META_BLK0 * ROW_BLK].reshape(BATCH, SEQ, D_MODEL)
    st = lambda name: jnp.stack(outs[name])
    return (y_prompt, y_sample[:, :DEC_SEQ],
            st("p_ssd"), st("p_conv"), st("p_s5r"), st("p_s5i"), st("p_rwkv"), st("p_shift"),
            s_ssd, st("s_conv"), st("s_s5r"), st("s_s5i"), st("s_rwkv"), st("s_shift"))
```
